```python
import math, functools
import jax, jax.numpy as jnp
from jax import lax
import numpy as np

D_MODEL = 1024
BATCH = 4
SEQ = 8192
DEPTH = 2

N_MEM = 256
D_MAIN = D_MODEL
N_POOL_GROUPS = 4
POOL_WINDOWS = (2, 4, 8, 16)
POOL_GROUP = D_MAIN // N_POOL_GROUPS
FOX_HEADS = 16
FOX_HEAD_DIM = D_MAIN // FOX_HEADS
MEM_HEADS = 4
MEM_HEAD_DIM = 128
D_MEM = MEM_HEADS * MEM_HEAD_DIM
D_MIX = D_MAIN + D_MEM
D_IN = 2 * D_MIX
N_A = DEPTH // 2
N_B = DEPTH - N_A
Q_BLOCK = 128
ALPHA = (2 * DEPTH) ** 0.25
BETA = (8 * DEPTH) ** -0.25
LN_EPS = 1e-5

kernel_name = "yoco_pool_fox_memory_deepnorm"


def layer_norm(x, g, b):
    xf = x.astype(jnp.float32)
    mu = jnp.mean(xf, axis=-1, keepdims=True)
    var = jnp.mean(jnp.square(xf - mu), axis=-1, keepdims=True)
    y = (xf - mu) * lax.rsqrt(var + LN_EPS) * g.astype(jnp.float32) + b.astype(jnp.float32)
    return y.astype(x.dtype)


def causal_multiscale_pool(u, pool_w, pool_scale):
    B, S, _ = u.shape
    ug = u.reshape(B, S, N_POOL_GROUPS, POOL_GROUP)
    cs = jnp.cumsum(ug.astype(jnp.float32), axis=1)
    cs = jnp.pad(cs, ((0, 0), (1, 0), (0, 0), (0, 0)))
    t = jnp.arange(S)
    outs = []
    for gi, w in enumerate(POOL_WINDOWS):
        c = cs[:, :, gi]
        upper = c[:, 1:]
        lower = jnp.concatenate(
            [jnp.zeros((B, w - 1, POOL_GROUP), jnp.float32), c[:, :S + 1 - w]], axis=1)
        count = jnp.minimum(t + 1, w).astype(jnp.float32)[None, :, None]
        outs.append((upper - lower) / count - ug[:, :, gi].astype(jnp.float32))
    pm = jnp.stack(outs, axis=2).astype(u.dtype)
    mixed = jnp.einsum('bsgc,gcd->bsgd', pm, pool_w)
    return mixed.reshape(B, S, D_MAIN) * pool_scale


def shared_kv(x, w_kv_shared, b_forget):
    B, S, _ = x.shape
    h = x @ w_kv_shared
    k = h[..., :D_MAIN].reshape(B, S, FOX_HEADS, FOX_HEAD_DIM)
    v = h[..., D_MAIN:2 * D_MAIN].reshape(B, S, FOX_HEADS, FOX_HEAD_DIM)
    f_logit = h[..., 2 * D_MAIN:].astype(jnp.float32) + b_forget.astype(jnp.float32)
    log_f = jax.nn.log_sigmoid(f_logit)
    cum = jnp.cumsum(log_f, axis=1)
    return k, v, cum


def forgetting_attention(u, k, v, cum):
    B, S, _ = u.shape
    q = u.reshape(B, S, FOX_HEADS, FOX_HEAD_DIM)
    nb = S // Q_BLOCK
    qb = q.reshape(B, nb, Q_BLOCK, FOX_HEADS, FOX_HEAD_DIM).transpose(1, 0, 2, 3, 4)
    cb = cum.reshape(B, nb, Q_BLOCK, FOX_HEADS).transpose(1, 0, 3, 2)
    cum_k = cum.transpose(0, 2, 1)
    starts = jnp.arange(nb) * Q_BLOCK
    kpos = jnp.arange(S)
    scale = FOX_HEAD_DIM ** -0.5

    def block(args):
        qi, ci, s0 = args
        logits = jnp.einsum('bqhd,bkhd->bhqk', qi, k,
                            preferred_element_type=jnp.float32) * scale
        logits = logits + (ci[..., :, None] - cum_k[:, :, None, :])
        qpos = s0 + jnp.arange(Q_BLOCK)
        mask = kpos[None, :] <= qpos[:, None]
        logits = jnp.where(mask[None, None], logits, -jnp.inf)
        p = jax.nn.softmax(logits, axis=-1)
        return jnp.einsum('bhqk,bkhd->bqhd', p.astype(v.dtype), v)

    out = lax.map(block, (qb, cb, starts))
    return out.transpose(1, 0, 2, 3, 4).reshape(B, S, D_MAIN)


def memory_attention(q_mem, mem, w_mem_kv):
    B, S, _ = q_mem.shape
    M = mem.shape[1]
    mkv = mem @ w_mem_kv
    mk = mkv[..., :D_MEM].reshape(B, M, MEM_HEADS, MEM_HEAD_DIM)
    mv = mkv[..., D_MEM:].reshape(B, M, MEM_HEADS, MEM_HEAD_DIM)
    q = q_mem.reshape(B, S, MEM_HEADS, MEM_HEAD_DIM)
    logits = jnp.einsum('bshd,bmhd->bhsm', q, mk,
                        preferred_element_type=jnp.float32) * (MEM_HEAD_DIM ** -0.5)
    p = jax.nn.softmax(logits, axis=-1)
    return jnp.einsum('bhsm,bmhd->bshd', p.astype(mv.dtype), mv).reshape(B, S, D_MEM)


def mixer_sublayer(x, mem, w_in, w_mem_kv, w_out, main_fn):
    h = x @ w_in
    u_main = h[..., :D_MAIN]
    q_mem = h[..., D_MAIN:D_MIX]
    g_main = h[..., D_MIX:D_MIX + D_MAIN]
    g_mem = h[..., D_MIX + D_MAIN:]
    y_main = main_fn(u_main)
    y_mem = memory_attention(q_mem, mem, w_mem_kv)
    y = jnp.concatenate([y_main * jax.nn.silu(g_main), y_mem * jax.nn.silu(g_mem)], axis=-1)
    return y @ w_out


def setup_inputs(seed: int = 0) -> dict:
    key = jax.random.key(seed)
    ks = jax.random.split(key, 12)
    x = jax.random.normal(ks[0], (BATCH, SEQ, D_MODEL), jnp.float32)
    mem = jax.random.normal(ks[1], (BATCH, N_MEM, D_MODEL), jnp.float32)
    w_in = jax.random.normal(ks[2], (DEPTH, D_MODEL, D_IN), jnp.float32) * D_MODEL ** -0.5
    w_mem_kv = jax.random.normal(ks[3], (DEPTH, D_MODEL, 2 * D_MEM), jnp.float32) * D_MODEL ** -0.5
    w_out = jax.random.normal(ks[4], (DEPTH, D_MIX, D_MODEL), jnp.float32) * (D_MIX ** -0.5 * BETA)
    ln_g = 1.0 + 0.02 * jax.random.normal(ks[5], (DEPTH, D_MODEL), jnp.float32)
    ln_b = 0.02 * jax.random.normal(ks[6], (DEPTH, D_MODEL), jnp.float32)
    pool_w = jax.random.normal(ks[7], (N_A, N_POOL_GROUPS, POOL_GROUP, POOL_GROUP), jnp.float32) * POOL_GROUP ** -0.5
    pool_scale = 1.0 + 0.1 * jax.random.normal(ks[8], (N_A, D_MAIN), jnp.float32)
    w_kv_shared = jax.random.normal(ks[9], (D_MODEL, 2 * D_MAIN + FOX_HEADS), jnp.float32) * D_MODEL ** -0.5
    b_forget = jax.random.uniform(ks[10], (FOX_HEADS,), jnp.float32, 1.0, 5.0)
    return {"x": x, "mem": mem, "w_in": w_in, "w_mem_kv": w_mem_kv, "w_out": w_out,
            "ln_g": ln_g, "ln_b": ln_b, "pool_w": pool_w, "pool_scale": pool_scale,
            "w_kv_shared": w_kv_shared, "b_forget": b_forget}


def reference(x, mem, w_in, w_mem_kv, w_out, ln_g, ln_b, pool_w, pool_scale,
              w_kv_shared, b_forget):
    k_sh = v_sh = cum_sh = None
    for layer in range(DEPTH):
        if layer < N_A:
            main_fn = functools.partial(causal_multiscale_pool,
                                        pool_w=pool_w[layer], pool_scale=pool_scale[layer])
        else:
            if layer == N_A:
                k_sh, v_sh, cum_sh = shared_kv(x, w_kv_shared, b_forget)
            main_fn = functools.partial(forgetting_attention, k=k_sh, v=v_sh, cum=cum_sh)
        y = mixer_sublayer(x, mem, w_in[layer], w_mem_kv[layer], w_out[layer], main_fn)
        x = layer_norm(ALPHA * x + y, ln_g[layer], ln_b[layer])
    return x
```

```python
import functools

import numpy as np
import jax
import jax.numpy as jnp
from jax import lax
from jax.experimental import pallas as pl
from jax.experimental.pallas import tpu as pltpu

D_MODEL = 1024
N_MEM = 256
D_MAIN = 1024
POOL_WINDOWS = (2, 4, 8, 16)
POOL_GROUP = 256
FOX_HEADS = 16
FOX_DH = 64
MEM_HEADS = 4
MEM_DH = 128
D_MEM = MEM_HEADS * MEM_DH
D_MIX = D_MAIN + D_MEM
DEPTH = 2
ALPHA = (2 * DEPTH) ** 0.25
LN_EPS = 1e-5

LANES = 128
KPAD = 128
AUG0 = FOX_DH
AUG_ROWS = 16
NEG = -1e30
VMEM_LIMIT = 56 * 1024 * 1024

BF = jnp.bfloat16
F32 = jnp.float32


def _mm(a, b):
    return jnp.dot(a, b, preferred_element_type=F32)


def _nt(a, b):
    return lax.dot_general(a, b, (((1,), (1,)), ((), ())), preferred_element_type=F32)


def _split3(v):
    hi = v.astype(BF)
    r1 = v - hi.astype(F32)
    mid = r1.astype(BF)
    lo = (r1 - mid.astype(F32)).astype(BF)
    return hi, mid, lo


def _const_spec(shape):
    nd = len(shape)
    return pl.BlockSpec(shape, lambda *_: (0,) * nd)


def _mem_kv_kernel(mem_ref, w_ref, mk_ref, mvT_ref):
    memb = mem_ref[0].astype(BF)
    mkv = _mm(memb, w_ref[0])
    for h in range(MEM_HEADS):
        mk_ref[0, 0, h] = mkv[:, h * MEM_DH:(h + 1) * MEM_DH].astype(BF)
        mv = mkv[:, D_MEM + h * MEM_DH:D_MEM + (h + 1) * MEM_DH]
        mvT_ref[0, 0, h] = mv.T.astype(BF)


def _mem_kv(mem, w_mem_kv_bf):
    B = mem.shape[0]
    L = w_mem_kv_bf.shape[0]
    return pl.pallas_call(
        _mem_kv_kernel,
        grid=(L, B),
        in_specs=[
            pl.BlockSpec((1, N_MEM, D_MODEL), lambda l, b: (b, 0, 0)),
            pl.BlockSpec((1, D_MODEL, 2 * D_MEM), lambda l, b: (l, 0, 0)),
        ],
        out_specs=[
            pl.BlockSpec((1, 1, MEM_HEADS, N_MEM, MEM_DH), lambda l, b: (l, b, 0, 0, 0)),
            pl.BlockSpec((1, 1, MEM_HEADS, MEM_DH, N_MEM), lambda l, b: (l, b, 0, 0, 0)),
        ],
        out_shape=[
            jax.ShapeDtypeStruct((L, B, MEM_HEADS, N_MEM, MEM_DH), BF),
            jax.ShapeDtypeStruct((L, B, MEM_HEADS, MEM_DH, N_MEM), BF),
        ],
        compiler_params=pltpu.CompilerParams(
            dimension_semantics=("arbitrary", "arbitrary"), vmem_limit_bytes=VMEM_LIMIT),
        name="mem_kv",
    )(mem, w_mem_kv_bf)


def _finish_layer(x, mainT, qmT, gT, mk_ref, mvT_ref, w_outT_ref, lng_ref, lnb_ref, o_ref, yg_ref):
    yg_ref[0:D_MAIN, :] = (mainT * jax.nn.silu(gT[0:D_MAIN])).astype(BF)
    for h in range(MEM_HEADS):
        rows = slice(h * MEM_DH, (h + 1) * MEM_DH)
        qh = qmT[rows].astype(BF)
        lg = _mm(mk_ref[0, 0, h], qh) * (MEM_DH ** -0.5)
        mx = jnp.max(lg, axis=0, keepdims=True)
        e = jnp.exp(lg - mx)
        den = jnp.sum(e, axis=0, keepdims=True)
        ym = _mm(mvT_ref[0, 0, h], e.astype(BF)) / den
        gm = gT[D_MAIN + h * MEM_DH:D_MAIN + (h + 1) * MEM_DH]
        yg_ref[D_MAIN + h * MEM_DH:D_MAIN + (h + 1) * MEM_DH, :] = (ym * jax.nn.silu(gm)).astype(BF)
    outT = _mm(w_outT_ref[...], yg_ref[...])
    z = ALPHA * x + outT.T
    mu = jnp.mean(z, axis=-1, keepdims=True)
    zc = z - mu
    var = jnp.mean(zc * zc, axis=-1, keepdims=True)
    o_ref[0] = zc * lax.rsqrt(var + LN_EPS) * lng_ref[...] + lnb_ref[...]


def _layer0_kernel(x_ref, w_inT_ref, pool_wT_ref, ps_ref, mk_ref, mvT_ref, w_outT_ref,
                   lng_ref, lnb_ref, o_ref, tail_ref, main_ref, yg_ref):
    t = pl.program_id(1)
    x = x_ref[0]
    tm = x.shape[0]
    xb = x.astype(BF)
    uT = _nt(w_inT_ref[0:D_MAIN, :], xb)
    qmT = _nt(w_inT_ref[D_MAIN:D_MIX, :], xb)
    gT = _nt(w_inT_ref[D_MIX:, :], xb)

    @pl.when(t == 0)
    def _():
        tail_ref[...] = jnp.zeros_like(tail_ref)

    tpos = t * tm + lax.broadcasted_iota(jnp.int32, (1, tm), 1)
    for g, w in enumerate(POOL_WINDOWS):
        rows = slice(g * POOL_GROUP, (g + 1) * POOL_GROUP)
        u = uT[rows]
        acc = jnp.concatenate([tail_ref[rows, :], u], axis=1)
        sh = 1
        while sh < w:
            acc = acc + pltpu.roll(acc, sh, 1)
            sh *= 2
        cnt = jnp.minimum(tpos + 1, w).astype(F32)
        pm = acc[:, LANES:] / cnt - u
        main_ref[rows, :] = _mm(pool_wT_ref[g], pm.astype(BF)) * ps_ref[rows, :]
    tail_ref[...] = uT[:, tm - LANES:]

    _finish_layer(x, main_ref[...], qmT, gT, mk_ref, mvT_ref, w_outT_ref, lng_ref, lnb_ref,
                  o_ref, yg_ref)


def _layer0(x, w_inT, pool_wT, ps, mk, mvT, w_outT, lng, lnb, tm):
    B, S, _ = x.shape
    return pl.pallas_call(
        _layer0_kernel,
        grid=(B, S // tm),
        in_specs=[
            pl.BlockSpec((1, tm, D_MODEL), lambda b, t: (b, t, 0)),
            _const_spec(w_inT.shape),
            _const_spec(pool_wT.shape),
            _const_spec(ps.shape),
            pl.BlockSpec((1, 1, MEM_HEADS, N_MEM, MEM_DH), lambda b, t: (0, b, 0, 0, 0)),
            pl.BlockSpec((1, 1, MEM_HEADS, MEM_DH, N_MEM), lambda b, t: (0, b, 0, 0, 0)),
            _const_spec(w_outT.shape),
            _const_spec(lng.shape),
            _const_spec(lnb.shape),
        ],
        out_specs=pl.BlockSpec((1, tm, D_MODEL), lambda b, t: (b, t, 0)),
        out_shape=jax.ShapeDtypeStruct((B, S, D_MODEL), F32),
        scratch_shapes=[
            pltpu.VMEM((D_MAIN, LANES), F32),
            pltpu.VMEM((D_MAIN, tm), F32),
            pltpu.VMEM((D_MIX, tm), BF),
        ],
        compiler_params=pltpu.CompilerParams(
            dimension_semantics=("arbitrary", "arbitrary"), vmem_limit_bytes=VMEM_LIMIT),
        name="layer0",
    )(x, w_inT, pool_wT, ps, mk, mvT, w_outT, lng, lnb)


def _shared_kv_kernel(x_ref, wf3_ref, bias3_ref, wka_ref, wqT_ref, wvT_ref,
                      k_ref, qT_ref, vT_ref, cumT_ref, rb_ref, carry_ref):
    t = pl.program_id(1)

    @pl.when(t == 0)
    def _():
        carry_ref[...] = jnp.zeros_like(carry_ref)

    xb = x_ref[0].astype(BF)
    tm = xb.shape[0]
    lane = lax.broadcasted_iota(jnp.int32, (tm, LANES), 1)
    f3 = _mm(xb, wf3_ref[...]) + bias3_ref[...]
    lf = jnp.where(lane < 3 * FOX_HEADS, jax.nn.log_sigmoid(f3), 0.0)

    cb = min(256, tm)
    ri = lax.broadcasted_iota(jnp.int32, (cb, cb), 0)
    ci = lax.broadcasted_iota(jnp.int32, (cb, cb), 1)
    tri = jnp.where(ci <= ri, 1.0, 0.0).astype(BF)
    off = jnp.zeros((1, LANES), F32)
    blocks = []
    for i in range(tm // cb):
        hi, mid, lo = _split3(lf[i * cb:(i + 1) * cb])
        c = _mm(tri, hi) + _mm(tri, mid) + _mm(tri, lo) + off
        off = c[cb - 1:cb, :]
        blocks.append(c)
    c_loc = jnp.concatenate(blocks, axis=0)
    cum = c_loc + carry_ref[...]
    carry_ref[...] = cum[tm - 1:tm, :]

    hi, mid, lo = (v.astype(F32) for v in _split3(c_loc[0:1, :] - c_loc))
    pcs = jnp.where(lane < FOX_HEADS, hi,
                    jnp.where(lane < 2 * FOX_HEADS, mid,
                              jnp.where(lane < 3 * FOX_HEADS, lo,
                                        jnp.where(lane == 3 * FOX_HEADS, 1.0, 0.0))))
    xa = jnp.concatenate([xb, pcs.astype(BF)], axis=1)
    kf = _mm(xa, wka_ref[...])
    for h in range(FOX_HEADS):
        k_ref[0, h] = kf[:, h * KPAD:(h + 1) * KPAD].astype(BF)

    qT = (_nt(wqT_ref[...], xb) * (FOX_DH ** -0.5)).astype(BF)
    vT = _nt(wvT_ref[...], xb).astype(BF)
    for h in range(FOX_HEADS):
        qT_ref[0, h] = qT[h * FOX_DH:(h + 1) * FOX_DH]
        vT_ref[0, h] = vT[h * FOX_DH:(h + 1) * FOX_DH]

    cT = cum.T
    cumT_ref[0] = cT[0:FOX_HEADS]
    rb_ref[0, 0] = jnp.broadcast_to(cT[0:FOX_HEADS, 0:1], (FOX_HEADS, LANES))


def _shared_kv(x1, wf3, bias3, wka, wqT, wvT, tm):
    B, S, _ = x1.shape
    nt = S // tm
    hd = lambda b, t: (b, 0, 0, t)
    return pl.pallas_call(
        _shared_kv_kernel,
        grid=(B, nt),
        in_specs=[
            pl.BlockSpec((1, tm, D_MODEL), lambda b, t: (b, t, 0)),
            _const_spec(wf3.shape),
            _const_spec(bias3.shape),
            _const_spec(wka.shape),
            _const_spec(wqT.shape),
            _const_spec(wvT.shape),
        ],
        out_specs=[
            pl.BlockSpec((1, FOX_HEADS, tm, KPAD), lambda b, t: (b, 0, t, 0)),
            pl.BlockSpec((1, FOX_HEADS, FOX_DH, tm), hd),
            pl.BlockSpec((1, FOX_HEADS, FOX_DH, tm), hd),
            pl.BlockSpec((1, FOX_HEADS, tm), lambda b, t: (b, 0, t)),
            pl.BlockSpec((1, 1, FOX_HEADS, LANES), lambda b, t: (b, t, 0, 0)),
        ],
        out_shape=[
            jax.ShapeDtypeStruct((B, FOX_HEADS, S, KPAD), BF),
            jax.ShapeDtypeStruct((B, FOX_HEADS, FOX_DH, S), BF),
            jax.ShapeDtypeStruct((B, FOX_HEADS, FOX_DH, S), BF),
            jax.ShapeDtypeStruct((B, FOX_HEADS, S), F32),
            jax.ShapeDtypeStruct((B, nt, FOX_HEADS, LANES), F32),
        ],
        scratch_shapes=[pltpu.VMEM((1, LANES), F32)],
        compiler_params=pltpu.CompilerParams(
            dimension_semantics=("arbitrary", "arbitrary"), vmem_limit_bytes=VMEM_LIMIT),
        name="shared_kv",
    )(x1, wf3, bias3, wka, wqT, wvT)


def _fox_kernel(ii_ref, jj_ref, qT_ref, cumT_ref, rb_ref, k_ref, vT_ref, y_ref,
                qa_ref, o_ref, m_ref, l_ref):
    p = pl.program_id(1)
    i = ii_ref[p]
    j = jj_ref[p]
    tq = qa_ref.shape[2]
    tk = k_ref.shape[2]

    @pl.when(j == 0)
    def _():
        for h in range(FOX_HEADS):
            qa_ref[h, 0:FOX_DH, :] = qT_ref[0, h]
            qa_ref[h, AUG0 + AUG_ROWS:, :] = jnp.zeros((KPAD - AUG0 - AUG_ROWS, tq), BF)
        m_ref[...] = jnp.full(m_ref.shape, NEG, F32)
        l_ref[...] = jnp.zeros_like(l_ref)
        o_ref[...] = jnp.zeros_like(o_ref)

    a = cumT_ref[0] - jnp.tile(rb_ref[0, 0], (1, tq // LANES))
    hi, mid, lo = (v.astype(F32) for v in _split3(a))
    rowi = lax.broadcasted_iota(jnp.int32, (AUG_ROWS, tq), 0)
    for h in range(FOX_HEADS):
        bc = lambda v: jnp.broadcast_to(v[h:h + 1, :], (AUG_ROWS, tq))
        blk = jnp.where(rowi < 3, 1.0,
                        jnp.where(rowi == 3, bc(hi),
                                  jnp.where(rowi == 4, bc(mid),
                                            jnp.where(rowi == 5, bc(lo), 0.0))))
        qa_ref[h, AUG0:AUG0 + AUG_ROWS, :] = blk.astype(BF)

    def head(h, carry, masked):
        s = _mm(k_ref[0, h], qa_ref[h])
        if masked:
            r = lax.broadcasted_iota(jnp.int32, (tk, tq), 0)
            c = lax.broadcasted_iota(jnp.int32, (tk, tq), 1)
            s = jnp.where(r <= c, s, NEG)
        m_prev = m_ref[h]
        m_cur = jnp.maximum(m_prev, jnp.max(s, axis=0, keepdims=True))
        alpha = jnp.exp(m_prev - m_cur)
        pe = jnp.exp(s - m_cur)
        l_ref[h] = alpha * l_ref[h] + jnp.sum(pe, axis=0, keepdims=True)
        o_ref[h] = alpha * o_ref[h] + _mm(vT_ref[0, h], pe.astype(BF))
        m_ref[h] = m_cur
        return carry

    @pl.when(j < i)
    def _():
        lax.fori_loop(0, FOX_HEADS, functools.partial(head, masked=False), 0)

    @pl.when(j == i)
    def _():
        lax.fori_loop(0, FOX_HEADS, functools.partial(head, masked=True), 0)
        for h in range(FOX_HEADS):
            y_ref[0, h] = (o_ref[h] / l_ref[h]).astype(BF)


def _fox_attention(qT, cumT, rb, k, vT, tq):
    B, H, dh, S = qT.shape
    nq = S // tq
    pairs = [(i, j) for i in range(nq) for j in range(i + 1)]
    ii = jnp.asarray(np.array([p[0] for p in pairs], np.int32))
    jj = jnp.asarray(np.array([p[1] for p in pairs], np.int32))
    grid_spec = pltpu.PrefetchScalarGridSpec(
        num_scalar_prefetch=2,
        grid=(B, len(pairs)),
        in_specs=[
            pl.BlockSpec((1, H, dh, tq), lambda b, p, ii, jj: (b, 0, 0, ii[p])),
            pl.BlockSpec((1, H, tq), lambda b, p, ii, jj: (b, 0, ii[p])),
            pl.BlockSpec((1, 1, H, LANES), lambda b, p, ii, jj: (b, jj[p], 0, 0)),
            pl.BlockSpec((1, H, tq, KPAD), lambda b, p, ii, jj: (b, 0, jj[p], 0)),
            pl.BlockSpec((1, H, dh, tq), lambda b, p, ii, jj: (b, 0, 0, jj[p])),
        ],
        out_specs=pl.BlockSpec((1, H, dh, tq), lambda b, p, ii, jj: (b, 0, 0, ii[p])),
        scratch_shapes=[
            pltpu.VMEM((H, KPAD, tq), BF),
            pltpu.VMEM((H, dh, tq), F32),
            pltpu.VMEM((H, 1, tq), F32),
            pltpu.VMEM((H, 1, tq), F32),
        ],
    )
    return pl.pallas_call(
        _fox_kernel,
        grid_spec=grid_spec,
        out_shape=jax.ShapeDtypeStruct((B, H, dh, S), BF),
        compiler_params=pltpu.CompilerParams(
            dimension_semantics=("arbitrary", "arbitrary"), vmem_limit_bytes=VMEM_LIMIT),
        name="fox_attn",
    )(ii, jj, qT, cumT, rb, k, vT)


def _layer1_kernel(x_ref, yT_ref, w_inT_ref, mk_ref, mvT_ref, w_outT_ref, lng_ref, lnb_ref,
                   o_ref, yg_ref):
    x = x_ref[0]
    xb = x.astype(BF)
    qmT = _nt(w_inT_ref[0:D_MEM, :], xb)
    gT = _nt(w_inT_ref[D_MEM:, :], xb)
    _finish_layer(x, yT_ref[0].astype(F32), qmT, gT, mk_ref, mvT_ref, w_outT_ref, lng_ref, lnb_ref,
                  o_ref, yg_ref)


def _layer1(x1, yT, w_inT, mk, mvT, w_outT, lng, lnb, tm):
    B, S, _ = x1.shape
    return pl.pallas_call(
        _layer1_kernel,
        grid=(B, S // tm),
        in_specs=[
            pl.BlockSpec((1, tm, D_MODEL), lambda b, t: (b, t, 0)),
            pl.BlockSpec((1, D_MAIN, tm), lambda b, t: (b, 0, t)),
            _const_spec(w_inT.shape),
            pl.BlockSpec((1, 1, MEM_HEADS, N_MEM, MEM_DH), lambda b, t: (1, b, 0, 0, 0)),
            pl.BlockSpec((1, 1, MEM_HEADS, MEM_DH, N_MEM), lambda b, t: (1, b, 0, 0, 0)),
            _const_spec(w_outT.shape),
            _const_spec(lng.shape),
            _const_spec(lnb.shape),
        ],
        out_specs=pl.BlockSpec((1, tm, D_MODEL), lambda b, t: (b, t, 0)),
        out_shape=jax.ShapeDtypeStruct((B, S, D_MODEL), F32),
        scratch_shapes=[pltpu.VMEM((D_MIX, tm), BF)],
        compiler_params=pltpu.CompilerParams(
            dimension_semantics=("arbitrary", "arbitrary"), vmem_limit_bytes=VMEM_LIMIT),
        name="layer1",
    )(x1, yT, w_inT, mk, mvT, w_outT, lng, lnb)


def _decay_placement():
    e = np.zeros((LANES, FOX_HEADS * KPAD), np.float32)
    for h in range(FOX_HEADS):
        for piece in range(3):
            e[piece * FOX_HEADS + h, h * KPAD + AUG0 + piece] = 1.0
            e[3 * FOX_HEADS, h * KPAD + AUG0 + 3 + piece] = 1.0
    return e


def kernel(x, mem, w_in, w_mem_kv, w_out, ln_g, ln_b, pool_w, pool_scale, w_kv_shared, b_forget):
    B, S, _ = x.shape
    tm = min(512, S)

    w_inT = jnp.swapaxes(w_in, 1, 2).astype(BF)
    w_outT = jnp.swapaxes(w_out, 1, 2).astype(BF)
    pool_wT = jnp.swapaxes(pool_w[0], 1, 2).astype(BF)
    ps = pool_scale[0].reshape(D_MAIN, 1)
    lng = ln_g.reshape(DEPTH, 1, D_MODEL)
    lnb = ln_b.reshape(DEPTH, 1, D_MODEL)
    wk = w_kv_shared[:, :D_MAIN].reshape(D_MODEL, FOX_HEADS, FOX_DH)
    wk_aug = jnp.pad(wk, ((0, 0), (0, 0), (0, KPAD - FOX_DH))).reshape(D_MODEL, FOX_HEADS * KPAD)
    wka = jnp.concatenate([wk_aug, jnp.asarray(_decay_placement())], axis=0).astype(BF)
    wvT = w_kv_shared[:, D_MAIN:2 * D_MAIN].T.astype(BF)
    wf = w_kv_shared[:, 2 * D_MAIN:]
    wf3 = jnp.concatenate([wf, wf, wf, jnp.zeros((D_MODEL, LANES - 3 * FOX_HEADS), F32)], axis=1).astype(BF)
    bias3 = jnp.concatenate([b_forget, b_forget, b_forget,
                             jnp.zeros((LANES - 3 * FOX_HEADS,), F32)]).reshape(1, LANES)

    mk, mvT = _mem_kv(mem, w_mem_kv.astype(BF))
    x1 = _layer0(x, w_inT[0], pool_wT, ps, mk, mvT, w_outT[0], lng[0], lnb[0], tm)
    k, qT, vT, cumT, rb = _shared_kv(x1, wf3, bias3, wka, w_inT[1, :D_MAIN], wvT, tm)
    yT = _fox_attention(qT, cumT, rb, k, vT, tm)
    return _layer1(x1, yT.reshape(B, D_MAIN, S), w_inT[1, D_MAIN:], mk, mvT, w_outT[1],
                   lng[1], lnb[1], tm)
```

```python
import numpy as np
import jax
import jax.numpy as jnp
from jax import lax
from jax.experimental import pallas as pl
from jax.experimental.pallas import tpu as pltpu

D_MODEL = 1024
N_MEM = 256
D_MAIN = 1024
POOL_WINDOWS = (2, 4, 8, 16)
POOL_GROUP = 256
FOX_HEADS = 16
FOX_DH = 64
MEM_HEADS = 4
MEM_DH = 128
D_MEM = MEM_HEADS * MEM_DH
D_MIX = D_MAIN + D_MEM
DEPTH = 2
ALPHA = (2 * DEPTH) ** 0.25
LN_EPS = 1e-5

LANES = 128
KPAD = 128
AUG0 = FOX_DH
AUG_ROWS = 16
NEG = -1e30
LOG2E = 1.4426950408889634
VMEM_LIMIT = 56 * 1024 * 1024

BF = jnp.bfloat16
F32 = jnp.float32


def _mm(a, b):
    return jnp.dot(a, b, preferred_element_type=F32)


def _nt(a, b):
    return lax.dot_general(a, b, (((1,), (1,)), ((), ())), preferred_element_type=F32)


def _split3(v):
    hi = v.astype(BF)
    r1 = v - hi.astype(F32)
    mid = r1.astype(BF)
    lo = (r1 - mid.astype(F32)).astype(BF)
    return hi, mid, lo


def _const_spec(shape):
    nd = len(shape)
    return pl.BlockSpec(shape, lambda *_: (0,) * nd)


def _mem_kv_kernel(mem_ref, w_ref, mk_ref, mvT_ref):
    memb = mem_ref[0].astype(BF)
    mkv = _mm(memb, w_ref[0])
    for h in range(MEM_HEADS):
        mk_ref[0, 0, h] = mkv[:, h * MEM_DH:(h + 1) * MEM_DH].astype(BF)
        mv = mkv[:, D_MEM + h * MEM_DH:D_MEM + (h + 1) * MEM_DH]
        mvT_ref[0, 0, h] = mv.T.astype(BF)


def _mem_kv(mem, w_mem_kv_bf):
    B = mem.shape[0]
    L = w_mem_kv_bf.shape[0]
    return pl.pallas_call(
        _mem_kv_kernel,
        grid=(L, B),
        in_specs=[
            pl.BlockSpec((1, N_MEM, D_MODEL), lambda l, b: (b, 0, 0)),
            pl.BlockSpec((1, D_MODEL, 2 * D_MEM), lambda l, b: (l, 0, 0)),
        ],
        out_specs=[
            pl.BlockSpec((1, 1, MEM_HEADS, N_MEM, MEM_DH), lambda l, b: (l, b, 0, 0, 0)),
            pl.BlockSpec((1, 1, MEM_HEADS, MEM_DH, N_MEM), lambda l, b: (l, b, 0, 0, 0)),
        ],
        out_shape=[
            jax.ShapeDtypeStruct((L, B, MEM_HEADS, N_MEM, MEM_DH), BF),
            jax.ShapeDtypeStruct((L, B, MEM_HEADS, MEM_DH, N_MEM), BF),
        ],
        compiler_params=pltpu.CompilerParams(
            dimension_semantics=("arbitrary", "arbitrary"), vmem_limit_bytes=VMEM_LIMIT),
        name="mem_kv",
    )(mem, w_mem_kv_bf)


def _finish_layer(x, mainT, qmT, gT, mk_ref, mvT_ref, w_outT_ref, lng_ref, lnb_ref, o_ref, yg_ref):
    yg_ref[0:D_MAIN, :] = (mainT * jax.nn.silu(gT[0:D_MAIN])).astype(BF)
    for h in range(MEM_HEADS):
        rows = slice(h * MEM_DH, (h + 1) * MEM_DH)
        qh = qmT[rows].astype(BF)
        lg = _mm(mk_ref[0, 0, h], qh) * (MEM_DH ** -0.5)
        mx = jnp.max(lg, axis=0, keepdims=True)
        e = jnp.exp(lg - mx)
        den = jnp.sum(e, axis=0, keepdims=True)
        ym = _mm(mvT_ref[0, 0, h], e.astype(BF)) / den
        gm = gT[D_MAIN + h * MEM_DH:D_MAIN + (h + 1) * MEM_DH]
        yg_ref[D_MAIN + h * MEM_DH:D_MAIN + (h + 1) * MEM_DH, :] = (ym * jax.nn.silu(gm)).astype(BF)
    outT = _mm(w_outT_ref[...], yg_ref[...])
    z = ALPHA * x + outT.T
    mu = jnp.mean(z, axis=-1, keepdims=True)
    zc = z - mu
    var = jnp.mean(zc * zc, axis=-1, keepdims=True)
    o_ref[0] = zc * lax.rsqrt(var + LN_EPS) * lng_ref[...] + lnb_ref[...]


def _layer0_kernel(x_ref, w_inT_ref, pool_wT_ref, ps_ref, mk_ref, mvT_ref, w_outT_ref,
                   lng_ref, lnb_ref, o_ref, tail_ref, main_ref, yg_ref):
    t = pl.program_id(1)
    x = x_ref[0]
    tm = x.shape[0]
    xb = x.astype(BF)
    uT = _nt(w_inT_ref[0:D_MAIN, :], xb)
    qmT = _nt(w_inT_ref[D_MAIN:D_MIX, :], xb)
    gT = _nt(w_inT_ref[D_MIX:, :], xb)

    @pl.when(t == 0)
    def _():
        tail_ref[...] = jnp.zeros_like(tail_ref)

    tpos = t * tm + lax.broadcasted_iota(jnp.int32, (1, tm), 1)
    for g, w in enumerate(POOL_WINDOWS):
        rows = slice(g * POOL_GROUP, (g + 1) * POOL_GROUP)
        u = uT[rows]
        acc = jnp.concatenate([tail_ref[rows, :], u], axis=1)
        sh = 1
        while sh < w:
            acc = acc + pltpu.roll(acc, sh, 1)
            sh *= 2
        cnt = jnp.minimum(tpos + 1, w).astype(F32)
        pm = acc[:, LANES:] / cnt - u
        main_ref[rows, :] = _mm(pool_wT_ref[g], pm.astype(BF)) * ps_ref[rows, :]
    tail_ref[...] = uT[:, tm - LANES:]

    _finish_layer(x, main_ref[...], qmT, gT, mk_ref, mvT_ref, w_outT_ref, lng_ref, lnb_ref,
                  o_ref, yg_ref)


def _layer0(x, w_inT, pool_wT, ps, mk, mvT, w_outT, lng, lnb, tm):
    B, S, _ = x.shape
    return pl.pallas_call(
        _layer0_kernel,
        grid=(B, S // tm),
        in_specs=[
            pl.BlockSpec((1, tm, D_MODEL), lambda b, t: (b, t, 0)),
            _const_spec(w_inT.shape),
            _const_spec(pool_wT.shape),
            _const_spec(ps.shape),
            pl.BlockSpec((1, 1, MEM_HEADS, N_MEM, MEM_DH), lambda b, t: (0, b, 0, 0, 0)),
            pl.BlockSpec((1, 1, MEM_HEADS, MEM_DH, N_MEM), lambda b, t: (0, b, 0, 0, 0)),
            _const_spec(w_outT.shape),
            _const_spec(lng.shape),
            _const_spec(lnb.shape),
        ],
        out_specs=pl.BlockSpec((1, tm, D_MODEL), lambda b, t: (b, t, 0)),
        out_shape=jax.ShapeDtypeStruct((B, S, D_MODEL), F32),
        scratch_shapes=[
            pltpu.VMEM((D_MAIN, LANES), F32),
            pltpu.VMEM((D_MAIN, tm), F32),
            pltpu.VMEM((D_MIX, tm), BF),
        ],
        compiler_params=pltpu.CompilerParams(
            dimension_semantics=("arbitrary", "arbitrary"), vmem_limit_bytes=VMEM_LIMIT),
        name="layer0",
    )(x, w_inT, pool_wT, ps, mk, mvT, w_outT, lng, lnb)


def _shared_kv_kernel(x_ref, wf3_ref, bias3_ref, wka_ref, wqT_ref, wvT_ref,
                      k_ref, qT_ref, vT_ref, cumT_ref, rb_ref, carry_ref):
    t = pl.program_id(1)

    @pl.when(t == 0)
    def _():
        carry_ref[...] = jnp.zeros_like(carry_ref)

    xb = x_ref[0].astype(BF)
    tm = xb.shape[0]
    lane = lax.broadcasted_iota(jnp.int32, (tm, LANES), 1)
    f3 = _mm(xb, wf3_ref[...]) + bias3_ref[...]
    lf = jnp.where(lane < 3 * FOX_HEADS, jax.nn.log_sigmoid(f3), 0.0)

    cb = min(256, tm)
    ri = lax.broadcasted_iota(jnp.int32, (cb, cb), 0)
    ci = lax.broadcasted_iota(jnp.int32, (cb, cb), 1)
    tri = jnp.where(ci <= ri, 1.0, 0.0).astype(BF)
    off = jnp.zeros((1, LANES), F32)
    blocks = []
    for i in range(tm // cb):
        hi, mid, lo = _split3(lf[i * cb:(i + 1) * cb])
        c = _mm(tri, hi) + _mm(tri, mid) + _mm(tri, lo) + off
        off = c[cb - 1:cb, :]
        blocks.append(c)
    c_loc = jnp.concatenate(blocks, axis=0) * LOG2E
    cum = c_loc + carry_ref[...]
    carry_ref[...] = cum[tm - 1:tm, :]

    hi, mid, lo = (v.astype(F32) for v in _split3(c_loc[0:1, :] - c_loc))
    pcs = jnp.where(lane < FOX_HEADS, hi,
                    jnp.where(lane < 2 * FOX_HEADS, mid,
                              jnp.where(lane < 3 * FOX_HEADS, lo,
                                        jnp.where(lane == 3 * FOX_HEADS, 1.0, 0.0))))
    xa = jnp.concatenate([xb, pcs.astype(BF)], axis=1)
    kf = _mm(xa, wka_ref[...])
    for h in range(FOX_HEADS):
        k_ref[0, h] = kf[:, h * KPAD:(h + 1) * KPAD].astype(BF)

    qT = (_nt(wqT_ref[...], xb) * (FOX_DH ** -0.5 * LOG2E)).astype(BF)
    vT = _nt(wvT_ref[...], xb).astype(BF)
    for h in range(FOX_HEADS):
        qT_ref[0, h] = qT[h * FOX_DH:(h + 1) * FOX_DH]
        vT_ref[0, h] = vT[h * FOX_DH:(h + 1) * FOX_DH]

    cT = cum.T
    cumT_ref[0] = cT[0:FOX_HEADS]
    rb_ref[0, 0] = jnp.broadcast_to(cT[0:FOX_HEADS, 0:1], (FOX_HEADS, LANES))


def _shared_kv(x1, wf3, bias3, wka, wqT, wvT, tm):
    B, S, _ = x1.shape
    nt = S // tm
    hd = lambda b, t: (b, 0, 0, t)
    return pl.pallas_call(
        _shared_kv_kernel,
        grid=(B, nt),
        in_specs=[
            pl.BlockSpec((1, tm, D_MODEL), lambda b, t: (b, t, 0)),
            _const_spec(wf3.shape),
            _const_spec(bias3.shape),
            _const_spec(wka.shape),
            _const_spec(wqT.shape),
            _const_spec(wvT.shape),
        ],
        out_specs=[
            pl.BlockSpec((1, FOX_HEADS, tm, KPAD), lambda b, t: (b, 0, t, 0)),
            pl.BlockSpec((1, FOX_HEADS, FOX_DH, tm), hd),
            pl.BlockSpec((1, FOX_HEADS, FOX_DH, tm), hd),
            pl.BlockSpec((1, FOX_HEADS, tm), lambda b, t: (b, 0, t)),
            pl.BlockSpec((1, 1, FOX_HEADS, LANES), lambda b, t: (b, t, 0, 0)),
        ],
        out_shape=[
            jax.ShapeDtypeStruct((B, FOX_HEADS, S, KPAD), BF),
            jax.ShapeDtypeStruct((B, FOX_HEADS, FOX_DH, S), BF),
            jax.ShapeDtypeStruct((B, FOX_HEADS, FOX_DH, S), BF),
            jax.ShapeDtypeStruct((B, FOX_HEADS, S), F32),
            jax.ShapeDtypeStruct((B, nt, FOX_HEADS, LANES), F32),
        ],
        scratch_shapes=[pltpu.VMEM((1, LANES), F32)],
        compiler_params=pltpu.CompilerParams(
            dimension_semantics=("arbitrary", "arbitrary"), vmem_limit_bytes=VMEM_LIMIT),
        name="shared_kv",
    )(x1, wf3, bias3, wka, wqT, wvT)


def _fox_kernel(ii_ref, jj_ref, qT_ref, cumT_ref, rb_ref, k_ref, vT_ref, y_ref,
                qa_ref, o_ref, m_ref, l_ref, s0_ref, s1_ref, p0_ref, p1_ref):
    p = pl.program_id(1)
    i = ii_ref[p]
    j = jj_ref[p]
    tq = qa_ref.shape[2]
    tk = k_ref.shape[2]

    @pl.when(j == 0)
    def _():
        for h in range(FOX_HEADS):
            qa_ref[h, 0:FOX_DH, :] = qT_ref[0, h]
            qa_ref[h, AUG0 + AUG_ROWS:, :] = jnp.zeros((KPAD - AUG0 - AUG_ROWS, tq), BF)
        m_ref[...] = jnp.full(m_ref.shape, NEG, F32)
        l_ref[...] = jnp.zeros_like(l_ref)
        o_ref[...] = jnp.zeros_like(o_ref)

    a = cumT_ref[0] - jnp.tile(rb_ref[0, 0], (1, tq // LANES))
    hi, mid, lo = (v.astype(F32) for v in _split3(a))
    rowi = lax.broadcasted_iota(jnp.int32, (AUG_ROWS, tq), 0)
    for h in range(FOX_HEADS):
        bc = lambda v: jnp.broadcast_to(v[h:h + 1, :], (AUG_ROWS, tq))
        blk = jnp.where(rowi < 3, 1.0,
                        jnp.where(rowi == 3, bc(hi),
                                  jnp.where(rowi == 4, bc(mid),
                                            jnp.where(rowi == 5, bc(lo), 0.0))))
        qa_ref[h, AUG0:AUG0 + AUG_ROWS, :] = blk.astype(BF)

    def logits(h, m_prev, s_ref, masked):
        s = _mm(k_ref[0, h], qa_ref[h])
        if masked:
            r = lax.broadcasted_iota(jnp.int32, (tk, tq), 0)
            c = lax.broadcasted_iota(jnp.int32, (tk, tq), 1)
            s = jnp.where(r <= c, s, NEG)
        m_cur = jnp.maximum(m_prev, jnp.max(s, axis=0, keepdims=True))
        s_ref[...] = s
        return m_cur, jnp.exp2(m_prev - m_cur)

    def probs(s_ref, p_ref, m, al, l_prev):
        pe = jnp.exp2(s_ref[...] - m)
        p_ref[...] = pe.astype(BF)
        return al * l_prev + jnp.sum(pe, axis=0, keepdims=True)

    def values(h, p_ref, al):
        o_ref[h] = al * o_ref[h] + _mm(vT_ref[0, h], p_ref[...])

    def sweep(masked):
        n = FOX_HEADS
        m_e, a_e = logits(0, m_ref[0], s0_ref, masked)
        m_o, a_o = logits(1, m_ref[1], s1_ref, masked)
        l_ref[0] = probs(s0_ref, p0_ref, m_e, a_e, l_ref[0])
        m_ref[0] = m_e
        a_c = a_e
        m_e, a_e = logits(2, m_ref[2], s0_ref, masked)
        l_ref[1] = probs(s1_ref, p1_ref, m_o, a_o, l_ref[1])
        m_ref[1] = m_o
        values(0, p0_ref, a_c)
        a_c = a_o
        m_o, a_o = logits(3, m_ref[3], s1_ref, masked)

        def pair(g, carry):
            m_e, a_e, m_o, a_o, a_c = carry
            h = 2 * g
            mp_e, mp_o, lp_e, lp_o = m_ref[h + 2], m_ref[h + 3], l_ref[h], l_ref[h + 1]
            l_e = probs(s0_ref, p0_ref, m_e, a_e, lp_e)
            m_e2, a_e2 = logits(h + 2, mp_e, s0_ref, masked)
            values(h - 1, p1_ref, a_c)
            l_o = probs(s1_ref, p1_ref, m_o, a_o, lp_o)
            m_o2, a_o2 = logits(h + 3, mp_o, s1_ref, masked)
            values(h, p0_ref, a_e)
            m_ref[h] = m_e
            l_ref[h] = l_e
            m_ref[h + 1] = m_o
            l_ref[h + 1] = l_o
            return m_e2, a_e2, m_o2, a_o2, a_o

        m_e, a_e, m_o, a_o, a_c = lax.fori_loop(1, n // 2 - 1, pair, (m_e, a_e, m_o, a_o, a_c))
        l_ref[n - 2] = probs(s0_ref, p0_ref, m_e, a_e, l_ref[n - 2])
        m_ref[n - 2] = m_e
        values(n - 3, p1_ref, a_c)
        l_ref[n - 1] = probs(s1_ref, p1_ref, m_o, a_o, l_ref[n - 1])
        m_ref[n - 1] = m_o
        values(n - 2, p0_ref, a_e)
        values(n - 1, p1_ref, a_o)

    @pl.when(j < i)
    def _():
        sweep(False)

    @pl.when(j == i)
    def _():
        sweep(True)
        for h in range(FOX_HEADS):
            y_ref[0, h] = (o_ref[h] / l_ref[h]).astype(BF)


def _fox_attention(qT, cumT, rb, k, vT, tq):
    B, H, dh, S = qT.shape
    nq = S // tq
    pairs = [(i, j) for i in range(nq) for j in range(i + 1)]
    ii = jnp.asarray(np.array([p[0] for p in pairs], np.int32))
    jj = jnp.asarray(np.array([p[1] for p in pairs], np.int32))
    grid_spec = pltpu.PrefetchScalarGridSpec(
        num_scalar_prefetch=2,
        grid=(B, len(pairs)),
        in_specs=[
            pl.BlockSpec((1, H, dh, tq), lambda b, p, ii, jj: (b, 0, 0, ii[p])),
            pl.BlockSpec((1, H, tq), lambda b, p, ii, jj: (b, 0, ii[p])),
            pl.BlockSpec((1, 1, H, LANES), lambda b, p, ii, jj: (b, jj[p], 0, 0)),
            pl.BlockSpec((1, H, tq, KPAD), lambda b, p, ii, jj: (b, 0, jj[p], 0)),
            pl.BlockSpec((1, H, dh, tq), lambda b, p, ii, jj: (b, 0, 0, jj[p])),
        ],
        out_specs=pl.BlockSpec((1, H, dh, tq), lambda b, p, ii, jj: (b, 0, 0, ii[p])),
        scratch_shapes=[
            pltpu.VMEM((H, KPAD, tq), BF),
            pltpu.VMEM((H, dh, tq), F32),
            pltpu.VMEM((H, 1, tq), F32),
            pltpu.VMEM((H, 1, tq), F32),
            pltpu.VMEM((tq, tq), F32),
            pltpu.VMEM((tq, tq), F32),
            pltpu.VMEM((tq, tq), BF),
            pltpu.VMEM((tq, tq), BF),
        ],
    )
    return pl.pallas_call(
        _fox_kernel,
        grid_spec=grid_spec,
        out_shape=jax.ShapeDtypeStruct((B, H, dh, S), BF),
        compiler_params=pltpu.CompilerParams(
            dimension_semantics=("arbitrary", "arbitrary"), vmem_limit_bytes=VMEM_LIMIT),
        name="fox_attn",
    )(ii, jj, qT, cumT, rb, k, vT)


def _layer1_kernel(x_ref, yT_ref, w_inT_ref, mk_ref, mvT_ref, w_outT_ref, lng_ref, lnb_ref,
                   o_ref, yg_ref):
    x = x_ref[0]
    xb = x.astype(BF)
    qmT = _nt(w_inT_ref[0:D_MEM, :], xb)
    gT = _nt(w_inT_ref[D_MEM:, :], xb)
    _finish_layer(x, yT_ref[0].astype(F32), qmT, gT, mk_ref, mvT_ref, w_outT_ref, lng_ref, lnb_ref,
                  o_ref, yg_ref)


def _layer1(x1, yT, w_inT, mk, mvT, w_outT, lng, lnb, tm):
    B, S, _ = x1.shape
    return pl.pallas_call(
        _layer1_kernel,
        grid=(B, S // tm),
        in_specs=[
            pl.BlockSpec((1, tm, D_MODEL), lambda b, t: (b, t, 0)),
            pl.BlockSpec((1, D_MAIN, tm), lambda b, t: (b, 0, t)),
            _const_spec(w_inT.shape),
            pl.BlockSpec((1, 1, MEM_HEADS, N_MEM, MEM_DH), lambda b, t: (1, b, 0, 0, 0)),
            pl.BlockSpec((1, 1, MEM_HEADS, MEM_DH, N_MEM), lambda b, t: (1, b, 0, 0, 0)),
            _const_spec(w_outT.shape),
            _const_spec(lng.shape),
            _const_spec(lnb.shape),
        ],
        out_specs=pl.BlockSpec((1, tm, D_MODEL), lambda b, t: (b, t, 0)),
        out_shape=jax.ShapeDtypeStruct((B, S, D_MODEL), F32),
        scratch_shapes=[pltpu.VMEM((D_MIX, tm), BF)],
        compiler_params=pltpu.CompilerParams(
            dimension_semantics=("arbitrary", "arbitrary"), vmem_limit_bytes=VMEM_LIMIT),
        name="layer1",
    )(x1, yT, w_inT, mk, mvT, w_outT, lng, lnb)


def _decay_placement():
    e = np.zeros((LANES, FOX_HEADS * KPAD), np.float32)
    for h in range(FOX_HEADS):
        for piece in range(3):
            e[piece * FOX_HEADS + h, h * KPAD + AUG0 + piece] = 1.0
            e[3 * FOX_HEADS, h * KPAD + AUG0 + 3 + piece] = 1.0
    return e


def kernel(x, mem, w_in, w_mem_kv, w_out, ln_g, ln_b, pool_w, pool_scale, w_kv_shared, b_forget):
    B, S, _ = x.shape
    tm = min(512, S)

    w_inT = jnp.swapaxes(w_in, 1, 2).astype(BF)
    w_outT = jnp.swapaxes(w_out, 1, 2).astype(BF)
    pool_wT = jnp.swapaxes(pool_w[0], 1, 2).astype(BF)
    ps = pool_scale[0].reshape(D_MAIN, 1)
    lng = ln_g.reshape(DEPTH, 1, D_MODEL)
    lnb = ln_b.reshape(DEPTH, 1, D_MODEL)
    wk = w_kv_shared[:, :D_MAIN].reshape(D_MODEL, FOX_HEADS, FOX_DH)
    wk_aug = jnp.pad(wk, ((0, 0), (0, 0), (0, KPAD - FOX_DH))).reshape(D_MODEL, FOX_HEADS * KPAD)
    wka = jnp.concatenate([wk_aug, jnp.asarray(_decay_placement())], axis=0).astype(BF)
    wvT = w_kv_shared[:, D_MAIN:2 * D_MAIN].T.astype(BF)
    wf = w_kv_shared[:, 2 * D_MAIN:]
    wf3 = jnp.concatenate([wf, wf, wf, jnp.zeros((D_MODEL, LANES - 3 * FOX_HEADS), F32)], axis=1).astype(BF)
    bias3 = jnp.concatenate([b_forget, b_forget, b_forget,
                             jnp.zeros((LANES - 3 * FOX_HEADS,), F32)]).reshape(1, LANES)

    mk, mvT = _mem_kv(mem, w_mem_kv.astype(BF))
    x1 = _layer0(x, w_inT[0], pool_wT, ps, mk, mvT, w_outT[0], lng[0], lnb[0], tm)
    k, qT, vT, cumT, rb = _shared_kv(x1, wf3, bias3, wka, w_inT[1, :D_MAIN], wvT, tm)
    yT = _fox_attention(qT, cumT, rb, k, vT, tm)
    return _layer1(x1, yT.reshape(B, D_MAIN, S), w_inT[1, D_MAIN:], mk, mvT, w_outT[1],
                   lng[1], lnb[1], tm)
```

```python
import numpy as np
import jax
import jax.numpy as jnp
from jax import lax
from jax.experimental import pallas as pl
from jax.experimental.pallas import tpu as pltpu

D_MODEL = 1024
N_MEM = 256
D_MAIN = 1024
POOL_WINDOWS = (2, 4, 8, 16)
POOL_GROUP = 256
FOX_HEADS = 16
FOX_DH = 64
MEM_HEADS = 4
MEM_DH = 128
D_MEM = MEM_HEADS * MEM_DH
D_MIX = D_MAIN + D_MEM
DEPTH = 2
ALPHA = (2 * DEPTH) ** 0.25
LN_EPS = 1e-5

LANES = 128
BF16_ROWS = 16
KPAD = 128
AUG0 = FOX_DH
AUG_ROWS = BF16_ROWS
VROWS = FOX_DH + BF16_ROWS
NEG = -1e30
LOG2E = 1.4426950408889634
UNDERFLOW_MARGIN = 160.0
NORM_SLACK = 1.02
VMEM_LIMIT = 56 * 1024 * 1024

BF = jnp.bfloat16
F32 = jnp.float32


def _mm(a, b):
    return jnp.dot(a, b, preferred_element_type=F32)


def _nt(a, b):
    return lax.dot_general(a, b, (((1,), (1,)), ((), ())), preferred_element_type=F32)


def _split3(v):
    hi = v.astype(BF)
    r1 = v - hi.astype(F32)
    mid = r1.astype(BF)
    lo = (r1 - mid.astype(F32)).astype(BF)
    return hi, mid, lo


def _const_spec(shape):
    nd = len(shape)
    return pl.BlockSpec(shape, lambda *_: (0,) * nd)


def _mem_kv_kernel(mem_ref, w_ref, mk_ref, mvT_ref):
    memb = mem_ref[0].astype(BF)
    mkv = _mm(memb, w_ref[0])
    for h in range(MEM_HEADS):
        mk_ref[0, 0, h] = mkv[:, h * MEM_DH:(h + 1) * MEM_DH].astype(BF)
        mv = mkv[:, D_MEM + h * MEM_DH:D_MEM + (h + 1) * MEM_DH]
        mvT_ref[0, 0, h] = mv.T.astype(BF)


def _mem_kv(mem, w_mem_kv_bf):
    B = mem.shape[0]
    L = w_mem_kv_bf.shape[0]
    return pl.pallas_call(
        _mem_kv_kernel,
        grid=(L, B),
        in_specs=[
            pl.BlockSpec((1, N_MEM, D_MODEL), lambda l, b: (b, 0, 0)),
            pl.BlockSpec((1, D_MODEL, 2 * D_MEM), lambda l, b: (l, 0, 0)),
        ],
        out_specs=[
            pl.BlockSpec((1, 1, MEM_HEADS, N_MEM, MEM_DH), lambda l, b: (l, b, 0, 0, 0)),
            pl.BlockSpec((1, 1, MEM_HEADS, MEM_DH, N_MEM), lambda l, b: (l, b, 0, 0, 0)),
        ],
        out_shape=[
            jax.ShapeDtypeStruct((L, B, MEM_HEADS, N_MEM, MEM_DH), BF),
            jax.ShapeDtypeStruct((L, B, MEM_HEADS, MEM_DH, N_MEM), BF),
        ],
        compiler_params=pltpu.CompilerParams(
            dimension_semantics=("arbitrary", "arbitrary"), vmem_limit_bytes=VMEM_LIMIT),
        name="mem_kv",
    )(mem, w_mem_kv_bf)


def _finish_layer(x, mainT, qmT, gT, mk_ref, mvT_ref, w_outT_ref, lng_ref, lnb_ref, o_ref, yg_ref):
    yg_ref[0:D_MAIN, :] = (mainT * jax.nn.silu(gT[0:D_MAIN])).astype(BF)
    for h in range(MEM_HEADS):
        rows = slice(h * MEM_DH, (h + 1) * MEM_DH)
        qh = qmT[rows].astype(BF)
        lg = _mm(mk_ref[0, 0, h], qh) * (MEM_DH ** -0.5)
        mx = jnp.max(lg, axis=0, keepdims=True)
        e = jnp.exp(lg - mx)
        den = jnp.sum(e, axis=0, keepdims=True)
        ym = _mm(mvT_ref[0, 0, h], e.astype(BF)) / den
        gm = gT[D_MAIN + h * MEM_DH:D_MAIN + (h + 1) * MEM_DH]
        yg_ref[D_MAIN + h * MEM_DH:D_MAIN + (h + 1) * MEM_DH, :] = (ym * jax.nn.silu(gm)).astype(BF)
    outT = _mm(w_outT_ref[...], yg_ref[...])
    z = ALPHA * x + outT.T
    mu = jnp.mean(z, axis=-1, keepdims=True)
    zc = z - mu
    var = jnp.mean(zc * zc, axis=-1, keepdims=True)
    o_ref[0] = zc * lax.rsqrt(var + LN_EPS) * lng_ref[...] + lnb_ref[...]


def _layer0_kernel(x_ref, w_inT_ref, pool_wT_ref, ps_ref, mk_ref, mvT_ref, w_outT_ref,
                   lng_ref, lnb_ref, o_ref, tail_ref, main_ref, yg_ref):
    t = pl.program_id(1)
    x = x_ref[0]
    tm = x.shape[0]
    xb = x.astype(BF)
    uT = _nt(w_inT_ref[0:D_MAIN, :], xb)
    qmT = _nt(w_inT_ref[D_MAIN:D_MIX, :], xb)
    gT = _nt(w_inT_ref[D_MIX:, :], xb)

    @pl.when(t == 0)
    def _():
        tail_ref[...] = jnp.zeros_like(tail_ref)

    tpos = t * tm + lax.broadcasted_iota(jnp.int32, (1, tm), 1)
    for g, w in enumerate(POOL_WINDOWS):
        rows = slice(g * POOL_GROUP, (g + 1) * POOL_GROUP)
        u = uT[rows]
        acc = jnp.concatenate([tail_ref[rows, :], u], axis=1)
        sh = 1
        while sh < w:
            acc = acc + pltpu.roll(acc, sh, 1)
            sh *= 2
        cnt = jnp.minimum(tpos + 1, w).astype(F32)
        pm = acc[:, LANES:] / cnt - u
        main_ref[rows, :] = _mm(pool_wT_ref[g], pm.astype(BF)) * ps_ref[rows, :]
    tail_ref[...] = uT[:, tm - LANES:]

    _finish_layer(x, main_ref[...], qmT, gT, mk_ref, mvT_ref, w_outT_ref, lng_ref, lnb_ref,
                  o_ref, yg_ref)


def _layer0(x, w_inT, pool_wT, ps, mk, mvT, w_outT, lng, lnb, tm):
    B, S, _ = x.shape
    return pl.pallas_call(
        _layer0_kernel,
        grid=(B, S // tm),
        in_specs=[
            pl.BlockSpec((1, tm, D_MODEL), lambda b, t: (b, t, 0)),
            _const_spec(w_inT.shape),
            _const_spec(pool_wT.shape),
            _const_spec(ps.shape),
            pl.BlockSpec((1, 1, MEM_HEADS, N_MEM, MEM_DH), lambda b, t: (0, b, 0, 0, 0)),
            pl.BlockSpec((1, 1, MEM_HEADS, MEM_DH, N_MEM), lambda b, t: (0, b, 0, 0, 0)),
            _const_spec(w_outT.shape),
            _const_spec(lng.shape),
            _const_spec(lnb.shape),
        ],
        out_specs=pl.BlockSpec((1, tm, D_MODEL), lambda b, t: (b, t, 0)),
        out_shape=jax.ShapeDtypeStruct((B, S, D_MODEL), F32),
        scratch_shapes=[
            pltpu.VMEM((D_MAIN, LANES), F32),
            pltpu.VMEM((D_MAIN, tm), F32),
            pltpu.VMEM((D_MIX, tm), BF),
        ],
        compiler_params=pltpu.CompilerParams(
            dimension_semantics=("arbitrary", "arbitrary"), vmem_limit_bytes=VMEM_LIMIT),
        name="layer0",
    )(x, w_inT, pool_wT, ps, mk, mvT, w_outT, lng, lnb)


def _shared_kv_kernel(x_ref, wf3_ref, bias3_ref, wka_ref, wqT_ref, wvT_ref, hsum_ref,
                      k_ref, qT_ref, vT_ref, cumT_ref, rb_ref, kst_ref, qst_ref, carry_ref):
    t = pl.program_id(1)

    @pl.when(t == 0)
    def _():
        carry_ref[...] = jnp.zeros_like(carry_ref)

    xb = x_ref[0].astype(BF)
    tm = xb.shape[0]
    lane = lax.broadcasted_iota(jnp.int32, (tm, LANES), 1)
    f3 = _mm(xb, wf3_ref[...]) + bias3_ref[...]
    lf = jnp.where(lane < 3 * FOX_HEADS, jax.nn.log_sigmoid(f3), 0.0)

    cb = min(256, tm)
    ri = lax.broadcasted_iota(jnp.int32, (cb, cb), 0)
    ci = lax.broadcasted_iota(jnp.int32, (cb, cb), 1)
    tri = jnp.where(ci <= ri, 1.0, 0.0).astype(BF)
    off = jnp.zeros((1, LANES), F32)
    blocks = []
    for i in range(tm // cb):
        hi, mid, lo = _split3(lf[i * cb:(i + 1) * cb])
        c = _mm(tri, hi) + _mm(tri, mid) + _mm(tri, lo) + off
        off = c[cb - 1:cb, :]
        blocks.append(c)
    c_loc = jnp.concatenate(blocks, axis=0) * LOG2E
    cum = c_loc + carry_ref[...]
    carry_ref[...] = cum[tm - 1:tm, :]

    hi, mid, lo = (v.astype(F32) for v in _split3(c_loc[0:1, :] - c_loc))
    pcs = jnp.where(lane < FOX_HEADS, hi,
                    jnp.where(lane < 2 * FOX_HEADS, mid,
                              jnp.where(lane < 3 * FOX_HEADS, lo,
                                        jnp.where(lane == 3 * FOX_HEADS, 1.0, 0.0))))
    xa = jnp.concatenate([xb, pcs.astype(BF)], axis=1)
    kb = _mm(xa, wka_ref[...]).astype(BF)
    for h in range(FOX_HEADS):
        k_ref[0, h] = kb[:, h * KPAD:(h + 1) * KPAD]

    qT = (_nt(wqT_ref[...], xb) * (FOX_DH ** -0.5 * LOG2E)).astype(BF)
    vT = _nt(wvT_ref[...], xb).astype(BF)

    kf32 = kb.astype(F32)
    kn2 = jnp.max(_mm((kf32 * kf32).astype(BF), hsum_ref[...]), axis=0, keepdims=True)
    cmax = jnp.max(cum, axis=0, keepdims=True)
    cmin = jnp.min(cum, axis=0, keepdims=True)
    srow = lax.broadcasted_iota(jnp.int32, (8, LANES), 0)
    kst_ref[0, 0] = jnp.where(srow == 0, cmax, jnp.where(srow == 1, cmin, jnp.where(srow == 2, kn2, 0.0)))
    qf32 = qT.astype(F32)
    qn2 = jnp.sum((qf32 * qf32).reshape(FOX_HEADS, FOX_DH, tm), axis=1)
    qst_ref[0, 0] = jnp.broadcast_to(jnp.max(qn2, axis=1, keepdims=True), (FOX_HEADS, LANES))
    ones_rows = jnp.where(lax.broadcasted_iota(jnp.int32, (BF16_ROWS, tm), 0) == 0, 1.0, 0.0).astype(BF)
    for h in range(FOX_HEADS):
        qT_ref[0, h] = qT[h * FOX_DH:(h + 1) * FOX_DH]
        vT_ref[0, h, 0:FOX_DH, :] = vT[h * FOX_DH:(h + 1) * FOX_DH]
        vT_ref[0, h, FOX_DH:, :] = ones_rows

    cT = cum.T
    cumT_ref[0] = cT[0:FOX_HEADS]
    rb_ref[0, 0] = jnp.broadcast_to(cT[0:FOX_HEADS, 0:1], (FOX_HEADS, LANES))


def _head_sum_matrix():
    g = np.zeros((FOX_HEADS * KPAD, LANES), np.float32)
    for h in range(FOX_HEADS):
        g[h * KPAD:h * KPAD + FOX_DH, h] = 1.0
    return g


def _shared_kv(x1, wf3, bias3, wka, wqT, wvT, tm):
    B, S, _ = x1.shape
    nt = S // tm
    hd = lambda b, t: (b, 0, 0, t)
    hsum = jnp.asarray(_head_sum_matrix(), BF)
    return pl.pallas_call(
        _shared_kv_kernel,
        grid=(B, nt),
        in_specs=[
            pl.BlockSpec((1, tm, D_MODEL), lambda b, t: (b, t, 0)),
            _const_spec(wf3.shape),
            _const_spec(bias3.shape),
            _const_spec(wka.shape),
            _const_spec(wqT.shape),
            _const_spec(wvT.shape),
            _const_spec(hsum.shape),
        ],
        out_specs=[
            pl.BlockSpec((1, FOX_HEADS, tm, KPAD), lambda b, t: (b, 0, t, 0)),
            pl.BlockSpec((1, FOX_HEADS, FOX_DH, tm), hd),
            pl.BlockSpec((1, FOX_HEADS, VROWS, tm), hd),
            pl.BlockSpec((1, FOX_HEADS, tm), lambda b, t: (b, 0, t)),
            pl.BlockSpec((1, 1, FOX_HEADS, LANES), lambda b, t: (b, t, 0, 0)),
            pl.BlockSpec((1, 1, 8, LANES), lambda b, t: (b, t, 0, 0)),
            pl.BlockSpec((1, 1, FOX_HEADS, LANES), lambda b, t: (b, t, 0, 0)),
        ],
        out_shape=[
            jax.ShapeDtypeStruct((B, FOX_HEADS, S, KPAD), BF),
            jax.ShapeDtypeStruct((B, FOX_HEADS, FOX_DH, S), BF),
            jax.ShapeDtypeStruct((B, FOX_HEADS, VROWS, S), BF),
            jax.ShapeDtypeStruct((B, FOX_HEADS, S), F32),
            jax.ShapeDtypeStruct((B, nt, FOX_HEADS, LANES), F32),
            jax.ShapeDtypeStruct((B, nt, 8, LANES), F32),
            jax.ShapeDtypeStruct((B, nt, FOX_HEADS, LANES), F32),
        ],
        scratch_shapes=[pltpu.VMEM((1, LANES), F32)],
        compiler_params=pltpu.CompilerParams(
            dimension_semantics=("arbitrary", "arbitrary"), vmem_limit_bytes=VMEM_LIMIT),
        name="shared_kv",
    )(x1, wf3, bias3, wka, wqT, wvT, hsum)


def _fox_kernel(ii_ref, jj_ref, je_ref, n_ref, hl_ref, qT_ref, cumT_ref, rb_ref, k_ref, vT_ref, y_ref,
                qa_ref, o_ref, m_ref, s_ref, p_ref):
    b = pl.program_id(0)
    p = pl.program_id(1)
    i = ii_ref[p]
    j = jj_ref[p]
    step = b * pl.num_programs(1) + p
    n = n_ref[step]
    tq = qa_ref.shape[2]
    tk = k_ref.shape[2]

    def head(idx):
        return hl_ref[step * FOX_HEADS + idx]

    @pl.when(j == 0)
    def _():
        for h in range(FOX_HEADS):
            qa_ref[h, 0:FOX_DH, :] = qT_ref[0, h]
            qa_ref[h, AUG0 + AUG_ROWS:, :] = jnp.zeros((KPAD - AUG0 - AUG_ROWS, tq), BF)
        m_ref[...] = jnp.full(m_ref.shape, NEG, F32)
        o_ref[...] = jnp.zeros_like(o_ref)

    @pl.when(n > 0)
    def _():
        a = cumT_ref[0] - jnp.tile(rb_ref[0, 0], (1, tq // LANES))
        hi, mid, lo = (v.astype(F32) for v in _split3(a))
        rowi = lax.broadcasted_iota(jnp.int32, (AUG_ROWS, tq), 0)
        for h in range(FOX_HEADS):
            bc = lambda v: jnp.broadcast_to(v[h:h + 1, :], (AUG_ROWS, tq))
            blk = jnp.where(rowi < 3, 1.0,
                            jnp.where(rowi == 3, bc(hi),
                                      jnp.where(rowi == 4, bc(mid),
                                                jnp.where(rowi == 5, bc(lo), 0.0))))
            qa_ref[h, AUG0:AUG0 + AUG_ROWS, :] = blk.astype(BF)

    def logits(h, m_prev, slot, masked):
        s = _mm(k_ref[0, h], qa_ref[h])
        if masked:
            r = lax.broadcasted_iota(jnp.int32, (tk, tq), 0)
            c = lax.broadcasted_iota(jnp.int32, (tk, tq), 1)
            s = jnp.where(r <= c, s, NEG)
        m_cur = jnp.maximum(m_prev, jnp.max(s, axis=0, keepdims=True))
        s_ref[slot] = s
        return m_cur, jnp.exp2(m_prev - m_cur)

    def probs(slot, m):
        p_ref[slot] = jnp.exp2(s_ref[slot] - m).astype(BF)

    def values(h, slot, al):
        o_ref[h] = al * o_ref[h] + _mm(vT_ref[0, h], p_ref[slot])

    def sweep(masked):
        h0, h1, h2, h3 = head(0), head(1), head(2), head(3)
        m_e, a_e = logits(h0, m_ref[h0], 0, masked)
        m_o, a_o = logits(h1, m_ref[h1], 1, masked)
        probs(0, m_e)
        m_ref[h0] = m_e
        a_c = a_e
        m_e, a_e = logits(h2, m_ref[h2], 0, masked)
        probs(1, m_o)
        m_ref[h1] = m_o
        values(h0, 0, a_c)
        a_c = a_o
        m_o, a_o = logits(h3, m_ref[h3], 1, masked)

        def pair(g, carry):
            m_e, a_e, m_o, a_o, a_c = carry
            t = 2 * g
            hp, he, ho, he2, ho2 = head(t - 1), head(t), head(t + 1), head(t + 2), head(t + 3)
            mp_e, mp_o = m_ref[he2], m_ref[ho2]
            probs(0, m_e)
            m_e2, a_e2 = logits(he2, mp_e, 0, masked)
            values(hp, 1, a_c)
            probs(1, m_o)
            m_o2, a_o2 = logits(ho2, mp_o, 1, masked)
            values(he, 0, a_e)
            m_ref[he] = m_e
            m_ref[ho] = m_o
            return m_e2, a_e2, m_o2, a_o2, a_o

        m_e, a_e, m_o, a_o, a_c = lax.fori_loop(1, n // 2 - 1, pair, (m_e, a_e, m_o, a_o, a_c))
        hp, he, ho = head(n - 3), head(n - 2), head(n - 1)
        probs(0, m_e)
        m_ref[he] = m_e
        values(hp, 1, a_c)
        probs(1, m_o)
        m_ref[ho] = m_o
        values(he, 0, a_e)
        values(ho, 1, a_o)

    @pl.when(jnp.logical_and(j < i, n > 0))
    def _():
        sweep(False)

    @pl.when(j == i)
    def _():
        sweep(True)
        for h in range(FOX_HEADS):
            y_ref[0, h] = (o_ref[h, 0:FOX_DH, :] / o_ref[h, FOX_DH:FOX_DH + 1, :]).astype(BF)


def _fox_schedule(kst, qst, ii, jj):
    cmax = kst[:, :, 0, :FOX_HEADS]
    cmin = kst[:, :, 1, :FOX_HEADS]
    kn = jnp.sqrt(kst[:, :, 2, :FOX_HEADS]) * NORM_SLACK
    qn = jnp.sqrt(qst[:, :, :, 0]) * NORM_SLACK
    upper = qn[:, ii] * kn[:, jj] + cmax[:, ii] - cmin[:, jj]
    lower = -qn[:, ii] * kn[:, ii]
    need = (upper > lower - UNDERFLOW_MARGIN) | (ii == jj)[None, :, None]
    cnt = jnp.sum(need, axis=-1).astype(jnp.int32)
    n = jnp.where(cnt > 0, jnp.maximum(cnt + (cnt & 1), 4), 0)
    heads = jnp.argsort(jnp.logical_not(need), axis=-1, stable=True).astype(jnp.int32)
    pos = jnp.arange(ii.shape[0], dtype=jnp.int32)
    last = lax.cummax(jnp.where(cnt > 0, pos[None, :], -1), axis=1)
    je = jnp.take(jj, jnp.maximum(last, 0))
    return je.reshape(-1), n.reshape(-1), heads.reshape(-1)


def _fox_attention(qT, cumT, rb, kst, qst, k, vT, tq):
    B, H, dh, S = qT.shape
    nq = S // tq
    pairs = [(i, j) for i in range(nq) for j in range(i + 1)]
    P = len(pairs)
    ii = jnp.asarray(np.array([p[0] for p in pairs], np.int32))
    jj = jnp.asarray(np.array([p[1] for p in pairs], np.int32))
    je, n, heads = _fox_schedule(kst, qst, ii, jj)
    kv_tile = lambda b, p, ii, jj, je, n, hl: je[b * P + p]
    grid_spec = pltpu.PrefetchScalarGridSpec(
        num_scalar_prefetch=5,
        grid=(B, P),
        in_specs=[
            pl.BlockSpec((1, H, dh, tq), lambda b, p, ii, *_: (b, 0, 0, ii[p])),
            pl.BlockSpec((1, H, tq), lambda b, p, ii, *_: (b, 0, ii[p])),
            pl.BlockSpec((1, 1, H, LANES), lambda b, p, *s: (b, kv_tile(b, p, *s), 0, 0)),
            pl.BlockSpec((1, H, tq, KPAD), lambda b, p, *s: (b, 0, kv_tile(b, p, *s), 0)),
            pl.BlockSpec((1, H, VROWS, tq), lambda b, p, *s: (b, 0, 0, kv_tile(b, p, *s))),
        ],
        out_specs=pl.BlockSpec((1, H, dh, tq), lambda b, p, ii, *_: (b, 0, 0, ii[p])),
        scratch_shapes=[
            pltpu.VMEM((H, KPAD, tq), BF),
            pltpu.VMEM((H, VROWS, tq), F32),
            pltpu.VMEM((H, 1, tq), F32),
            pltpu.VMEM((2, tq, tq), F32),
            pltpu.VMEM((2, tq, tq), BF),
        ],
    )
    return pl.pallas_call(
        _fox_kernel,
        grid_spec=grid_spec,
        out_shape=jax.ShapeDtypeStruct((B, H, dh, S), BF),
        compiler_params=pltpu.CompilerParams(
            dimension_semantics=("arbitrary", "arbitrary"), vmem_limit_bytes=VMEM_LIMIT),
        name="fox_attn",
    )(ii, jj, je, n, heads, qT, cumT, rb, k, vT)


def _layer1_kernel(x_ref, yT_ref, w_inT_ref, mk_ref, mvT_ref, w_outT_ref, lng_ref, lnb_ref,
                   o_ref, yg_ref):
    x = x_ref[0]
    xb = x.astype(BF)
    qmT = _nt(w_inT_ref[0:D_MEM, :], xb)
    gT = _nt(w_inT_ref[D_MEM:, :], xb)
    _finish_layer(x, yT_ref[0].astype(F32), qmT, gT, mk_ref, mvT_ref, w_outT_ref, lng_ref, lnb_ref,
                  o_ref, yg_ref)


def _layer1(x1, yT, w_inT, mk, mvT, w_outT, lng, lnb, tm):
    B, S, _ = x1.shape
    return pl.pallas_call(
        _layer1_kernel,
        grid=(B, S // tm),
        in_specs=[
            pl.BlockSpec((1, tm, D_MODEL), lambda b, t: (b, t, 0)),
            pl.BlockSpec((1, D_MAIN, tm), lambda b, t: (b, 0, t)),
            _const_spec(w_inT.shape),
            pl.BlockSpec((1, 1, MEM_HEADS, N_MEM, MEM_DH), lambda b, t: (1, b, 0, 0, 0)),
            pl.BlockSpec((1, 1, MEM_HEADS, MEM_DH, N_MEM), lambda b, t: (1, b, 0, 0, 0)),
            _const_spec(w_outT.shape),
            _const_spec(lng.shape),
            _const_spec(lnb.shape),
        ],
        out_specs=pl.BlockSpec((1, tm, D_MODEL), lambda b, t: (b, t, 0)),
        out_shape=jax.ShapeDtypeStruct((B, S, D_MODEL), F32),
        scratch_shapes=[pltpu.VMEM((D_MIX, tm), BF)],
        compiler_params=pltpu.CompilerParams(
            dimension_semantics=("arbitrary", "arbitrary"), vmem_limit_bytes=VMEM_LIMIT),
        name="layer1",
    )(x1, yT, w_inT, mk, mvT, w_outT, lng, lnb)


def _decay_placement():
    e = np.zeros((LANES, FOX_HEADS * KPAD), np.float32)
    for h in range(FOX_HEADS):
        for piece in range(3):
            e[piece * FOX_HEADS + h, h * KPAD + AUG0 + piece] = 1.0
            e[3 * FOX_HEADS, h * KPAD + AUG0 + 3 + piece] = 1.0
    return e


def kernel(x, mem, w_in, w_mem_kv, w_out, ln_g, ln_b, pool_w, pool_scale, w_kv_shared, b_forget):
    B, S, _ = x.shape
    tm = min(512, S)

    w_inT = jnp.swapaxes(w_in, 1, 2).astype(BF)
    w_outT = jnp.swapaxes(w_out, 1, 2).astype(BF)
    pool_wT = jnp.swapaxes(pool_w[0], 1, 2).astype(BF)
    ps = pool_scale[0].reshape(D_MAIN, 1)
    lng = ln_g.reshape(DEPTH, 1, D_MODEL)
    lnb = ln_b.reshape(DEPTH, 1, D_MODEL)
    wk = w_kv_shared[:, :D_MAIN].reshape(D_MODEL, FOX_HEADS, FOX_DH)
    wk_aug = jnp.pad(wk, ((0, 0), (0, 0), (0, KPAD - FOX_DH))).reshape(D_MODEL, FOX_HEADS * KPAD)
    wka = jnp.concatenate([wk_aug, jnp.asarray(_decay_placement())], axis=0).astype(BF)
    wvT = w_kv_shared[:, D_MAIN:2 * D_MAIN].T.astype(BF)
    wf = w_kv_shared[:, 2 * D_MAIN:]
    wf3 = jnp.concatenate([wf, wf, wf, jnp.zeros((D_MODEL, LANES - 3 * FOX_HEADS), F32)], axis=1).astype(BF)
    bias3 = jnp.concatenate([b_forget, b_forget, b_forget,
                             jnp.zeros((LANES - 3 * FOX_HEADS,), F32)]).reshape(1, LANES)

    mk, mvT = _mem_kv(mem, w_mem_kv.astype(BF))
    x1 = _layer0(x, w_inT[0], pool_wT, ps, mk, mvT, w_outT[0], lng[0], lnb[0], tm)
    k, qT, vT, cumT, rb, kst, qst = _shared_kv(x1, wf3, bias3, wka, w_inT[1, :D_MAIN], wvT, tm)
    yT = _fox_attention(qT, cumT, rb, kst, qst, k, vT, tm)
    return _layer1(x1, yT.reshape(B, D_MAIN, S), w_inT[1, D_MAIN:], mk, mvT, w_outT[1],
                   lng[1], lnb[1], tm)
```

```python
import numpy as np
import jax
import jax.numpy as jnp
from jax import lax
from jax.experimental import pallas as pl
from jax.experimental.pallas import tpu as pltpu

D_MODEL = 1024
N_MEM = 256
D_MAIN = 1024
POOL_WINDOWS = (2, 4, 8, 16)
POOL_GROUP = 256
FOX_HEADS = 16
FOX_DH = 64
MEM_HEADS = 4
MEM_DH = 128
D_MEM = MEM_HEADS * MEM_DH
D_MIX = D_MAIN + D_MEM
DEPTH = 2
ALPHA = (2 * DEPTH) ** 0.25
LN_EPS = 1e-5

LANES = 128
BF16_ROWS = 16
HEAD_PAIRS = FOX_HEADS // 2
QA_ROWS = 2 * LANES
VROWS = FOX_DH + BF16_ROWS
NEG = -1e30
LOG2E = 1.4426950408889634
UNDERFLOW_MARGIN = 160.0
NORM_SLACK = 1.02
TOKEN_SPLIT = 2
VMEM_LIMIT = 56 * 1024 * 1024

BF = jnp.bfloat16
F32 = jnp.float32


def _mm(a, b):
    return jnp.dot(a, b, preferred_element_type=F32)


def _nt(a, b):
    return lax.dot_general(a, b, (((1,), (1,)), ((), ())), preferred_element_type=F32)


def _split3(v):
    hi = v.astype(BF)
    r1 = v - hi.astype(F32)
    mid = r1.astype(BF)
    lo = (r1 - mid.astype(F32)).astype(BF)
    return hi, mid, lo


def _const_spec(shape):
    nd = len(shape)
    return pl.BlockSpec(shape, lambda *_: (0,) * nd)


def _mem_kv_kernel(mem_ref, w_ref, mk_ref, mvT_ref):
    memb = mem_ref[0].astype(BF)
    mkv = _mm(memb, w_ref[0])
    for h in range(MEM_HEADS):
        mk_ref[0, 0, h] = mkv[:, h * MEM_DH:(h + 1) * MEM_DH].astype(BF)
        mv = mkv[:, D_MEM + h * MEM_DH:D_MEM + (h + 1) * MEM_DH]
        mvT_ref[0, 0, h] = mv.T.astype(BF)


def _mem_kv(mem, w_mem_kv_bf):
    B = mem.shape[0]
    L = w_mem_kv_bf.shape[0]
    return pl.pallas_call(
        _mem_kv_kernel,
        grid=(L, B),
        in_specs=[
            pl.BlockSpec((1, N_MEM, D_MODEL), lambda l, b: (b, 0, 0)),
            pl.BlockSpec((1, D_MODEL, 2 * D_MEM), lambda l, b: (l, 0, 0)),
        ],
        out_specs=[
            pl.BlockSpec((1, 1, MEM_HEADS, N_MEM, MEM_DH), lambda l, b: (l, b, 0, 0, 0)),
            pl.BlockSpec((1, 1, MEM_HEADS, MEM_DH, N_MEM), lambda l, b: (l, b, 0, 0, 0)),
        ],
        out_shape=[
            jax.ShapeDtypeStruct((L, B, MEM_HEADS, N_MEM, MEM_DH), BF),
            jax.ShapeDtypeStruct((L, B, MEM_HEADS, MEM_DH, N_MEM), BF),
        ],
        compiler_params=pltpu.CompilerParams(
            dimension_semantics=("arbitrary", "arbitrary"), vmem_limit_bytes=VMEM_LIMIT),
        name="mem_kv",
    )(mem, w_mem_kv_bf)


def _finish_layer(x, mainT, qmT, gT, mk_ref, mvT_ref, w_outT_ref, lng_ref, lnb_ref, o_ref, yg_ref, cols):
    yg_ref = yg_ref.at[:, cols]
    yg_ref[0:D_MAIN, :] = (mainT * jax.nn.silu(gT[0:D_MAIN])).astype(BF)
    for h in range(MEM_HEADS):
        rows = slice(h * MEM_DH, (h + 1) * MEM_DH)
        qh = qmT[rows].astype(BF)
        lg = _mm(mk_ref[0, 0, h], qh) * (MEM_DH ** -0.5)
        mx = jnp.max(lg, axis=0, keepdims=True)
        e = jnp.exp(lg - mx)
        den = jnp.sum(e, axis=0, keepdims=True)
        ym = _mm(mvT_ref[0, 0, h], e.astype(BF)) / den
        gm = gT[D_MAIN + h * MEM_DH:D_MAIN + (h + 1) * MEM_DH]
        yg_ref[D_MAIN + h * MEM_DH:D_MAIN + (h + 1) * MEM_DH, :] = (ym * jax.nn.silu(gm)).astype(BF)
    outT = _mm(w_outT_ref[...], yg_ref[...])
    z = ALPHA * x + outT.T
    mu = jnp.mean(z, axis=-1, keepdims=True)
    zc = z - mu
    var = jnp.mean(zc * zc, axis=-1, keepdims=True)
    o_ref[0, cols, :] = zc * lax.rsqrt(var + LN_EPS) * lng_ref[...] + lnb_ref[...]


def _layer0_kernel(x_ref, w_inT_ref, pool_wT_ref, ps_ref, mk_ref, mvT_ref, w_outT_ref,
                   lng_ref, lnb_ref, o_ref, tail_ref, main_ref, yg_ref):
    t = pl.program_id(1)
    x = x_ref[0]
    tm = x.shape[0]
    xb = x.astype(BF)
    uT = _nt(w_inT_ref[0:D_MAIN, :], xb)
    qmT = _nt(w_inT_ref[D_MAIN:D_MIX, :], xb)
    gT = _nt(w_inT_ref[D_MIX:, :], xb)

    @pl.when(t == 0)
    def _():
        tail_ref[...] = jnp.zeros_like(tail_ref)

    tpos = t * tm + lax.broadcasted_iota(jnp.int32, (1, tm), 1)
    for g, w in enumerate(POOL_WINDOWS):
        rows = slice(g * POOL_GROUP, (g + 1) * POOL_GROUP)
        u = uT[rows]
        acc = jnp.concatenate([tail_ref[rows, :], u], axis=1)
        sh = 1
        while sh < w:
            acc = acc + pltpu.roll(acc, sh, 1)
            sh *= 2
        cnt = jnp.minimum(tpos + 1, w).astype(F32)
        pm = acc[:, LANES:] / cnt - u
        main_ref[rows, :] = _mm(pool_wT_ref[g], pm.astype(BF)) * ps_ref[rows, :]
    tail_ref[...] = uT[:, tm - LANES:]

    _finish_layer(x, main_ref[...], qmT, gT, mk_ref, mvT_ref, w_outT_ref, lng_ref, lnb_ref,
                  o_ref, yg_ref, slice(0, tm))


def _layer0(x, w_inT, pool_wT, ps, mk, mvT, w_outT, lng, lnb, tm):
    B, S, _ = x.shape
    return pl.pallas_call(
        _layer0_kernel,
        grid=(B, S // tm),
        in_specs=[
            pl.BlockSpec((1, tm, D_MODEL), lambda b, t: (b, t, 0)),
            _const_spec(w_inT.shape),
            _const_spec(pool_wT.shape),
            _const_spec(ps.shape),
            pl.BlockSpec((1, 1, MEM_HEADS, N_MEM, MEM_DH), lambda b, t: (0, b, 0, 0, 0)),
            pl.BlockSpec((1, 1, MEM_HEADS, MEM_DH, N_MEM), lambda b, t: (0, b, 0, 0, 0)),
            _const_spec(w_outT.shape),
            _const_spec(lng.shape),
            _const_spec(lnb.shape),
        ],
        out_specs=pl.BlockSpec((1, tm, D_MODEL), lambda b, t: (b, t, 0)),
        out_shape=jax.ShapeDtypeStruct((B, S, D_MODEL), F32),
        scratch_shapes=[
            pltpu.VMEM((D_MAIN, LANES), F32),
            pltpu.VMEM((D_MAIN, tm), F32),
            pltpu.VMEM((D_MIX, tm), BF),
        ],
        compiler_params=pltpu.CompilerParams(
            dimension_semantics=("arbitrary", "arbitrary"), vmem_limit_bytes=VMEM_LIMIT),
        name="layer0",
    )(x, w_inT, pool_wT, ps, mk, mvT, w_outT, lng, lnb)


def _shared_kv_kernel(x_ref, wf3_ref, bias3_ref, wk_ref, wqT_ref, wvT_ref, hsum_ref,
                      k_ref, dk_ref, qT_ref, vT_ref, cumT_ref, rb_ref, kst_ref, qst_ref, carry_ref):
    t = pl.program_id(1)

    @pl.when(t == 0)
    def _():
        carry_ref[...] = jnp.zeros_like(carry_ref)

    xb = x_ref[0].astype(BF)
    tm = xb.shape[0]
    lane = lax.broadcasted_iota(jnp.int32, (tm, LANES), 1)
    f3 = _mm(xb, wf3_ref[...]) + bias3_ref[...]
    lf = jnp.where(lane < 3 * FOX_HEADS, jax.nn.log_sigmoid(f3), 0.0)

    cb = min(256, tm)
    ri = lax.broadcasted_iota(jnp.int32, (cb, cb), 0)
    ci = lax.broadcasted_iota(jnp.int32, (cb, cb), 1)
    tri = jnp.where(ci <= ri, 1.0, 0.0).astype(BF)
    off = jnp.zeros((1, LANES), F32)
    blocks = []
    for i in range(tm // cb):
        hi, mid, lo = _split3(lf[i * cb:(i + 1) * cb])
        c = _mm(tri, hi) + _mm(tri, mid) + _mm(tri, lo) + off
        off = c[cb - 1:cb, :]
        blocks.append(c)
    c_loc = jnp.concatenate(blocks, axis=0) * LOG2E
    cum = c_loc + carry_ref[...]
    carry_ref[...] = cum[tm - 1:tm, :]

    hi, mid, lo = (v.astype(F32) for v in _split3(c_loc[0:1, :] - c_loc))
    dk_ref[0, 0] = jnp.where(lane < FOX_HEADS, hi,
                             jnp.where(lane < 2 * FOX_HEADS, mid,
                                       jnp.where(lane < 3 * FOX_HEADS, lo, 0.0))).astype(BF)
    kb = _mm(xb, wk_ref[...]).astype(BF)
    for g in range(HEAD_PAIRS):
        k_ref[0, g, 0] = kb[:, g * LANES:(g + 1) * LANES]

    qT = (_nt(wqT_ref[...], xb) * (FOX_DH ** -0.5 * LOG2E)).astype(BF)
    vT = _nt(wvT_ref[...], xb).astype(BF)

    kf32 = kb.astype(F32)
    kn2 = jnp.max(_mm((kf32 * kf32).astype(BF), hsum_ref[...]), axis=0, keepdims=True)
    cmax = jnp.max(cum, axis=0, keepdims=True)
    cmin = jnp.min(cum, axis=0, keepdims=True)
    srow = lax.broadcasted_iota(jnp.int32, (8, LANES), 0)
    kst_ref[0, 0] = jnp.where(srow == 0, cmax, jnp.where(srow == 1, cmin, jnp.where(srow == 2, kn2, 0.0)))
    qf32 = qT.astype(F32)
    qn2 = jnp.sum((qf32 * qf32).reshape(FOX_HEADS, FOX_DH, tm), axis=1)
    qst_ref[0, 0] = jnp.broadcast_to(jnp.max(qn2, axis=1, keepdims=True), (FOX_HEADS, LANES))
    ones_rows = jnp.where(lax.broadcasted_iota(jnp.int32, (BF16_ROWS, tm), 0) == 0, 1.0, 0.0).astype(BF)
    for h in range(FOX_HEADS):
        qT_ref[0, h] = qT[h * FOX_DH:(h + 1) * FOX_DH]
        vT_ref[0, h, 0, 0:FOX_DH, :] = vT[h * FOX_DH:(h + 1) * FOX_DH]
        vT_ref[0, h, 0, FOX_DH:, :] = ones_rows

    cT = cum.T
    for h in range(FOX_HEADS):
        cumT_ref[0, h] = cT[h:h + 1]
        rb_ref[0, h] = jnp.broadcast_to(cT[h:h + 1, 0:1], (1, LANES))


def _head_sum_matrix():
    g = np.zeros((D_MAIN, LANES), np.float32)
    for h in range(FOX_HEADS):
        g[h * FOX_DH:(h + 1) * FOX_DH, h] = 1.0
    return g


def _shared_kv(x1, wf3, bias3, wk, wqT, wvT, tm):
    B, S, _ = x1.shape
    nt = S // tm
    hsum = jnp.asarray(_head_sum_matrix(), BF)
    return pl.pallas_call(
        _shared_kv_kernel,
        grid=(B, nt),
        in_specs=[
            pl.BlockSpec((1, tm, D_MODEL), lambda b, t: (b, t, 0)),
            _const_spec(wf3.shape),
            _const_spec(bias3.shape),
            _const_spec(wk.shape),
            _const_spec(wqT.shape),
            _const_spec(wvT.shape),
            _const_spec(hsum.shape),
        ],
        out_specs=[
            pl.BlockSpec((1, HEAD_PAIRS, 1, tm, LANES), lambda b, t: (b, 0, t, 0, 0)),
            pl.BlockSpec((1, 1, tm, LANES), lambda b, t: (b, t, 0, 0)),
            pl.BlockSpec((1, FOX_HEADS, FOX_DH, tm), lambda b, t: (b, 0, 0, t)),
            pl.BlockSpec((1, FOX_HEADS, 1, VROWS, tm), lambda b, t: (b, 0, t, 0, 0)),
            pl.BlockSpec((1, FOX_HEADS, 1, tm), lambda b, t: (b, 0, 0, t)),
            pl.BlockSpec((1, FOX_HEADS, 1, LANES), lambda b, t: (b, t, 0, 0)),
            pl.BlockSpec((1, 1, 8, LANES), lambda b, t: (b, t, 0, 0)),
            pl.BlockSpec((1, 1, FOX_HEADS, LANES), lambda b, t: (b, t, 0, 0)),
        ],
        out_shape=[
            jax.ShapeDtypeStruct((B, HEAD_PAIRS, nt, tm, LANES), BF),
            jax.ShapeDtypeStruct((B, nt, tm, LANES), BF),
            jax.ShapeDtypeStruct((B, FOX_HEADS, FOX_DH, S), BF),
            jax.ShapeDtypeStruct((B, FOX_HEADS, nt, VROWS, tm), BF),
            jax.ShapeDtypeStruct((B, FOX_HEADS, 1, S), F32),
            jax.ShapeDtypeStruct((B, nt * FOX_HEADS, 1, LANES), F32),
            jax.ShapeDtypeStruct((B, nt, 8, LANES), F32),
            jax.ShapeDtypeStruct((B, nt, FOX_HEADS, LANES), F32),
        ],
        scratch_shapes=[pltpu.VMEM((1, LANES), F32)],
        compiler_params=pltpu.CompilerParams(
            dimension_semantics=("arbitrary", "arbitrary"), vmem_limit_bytes=VMEM_LIMIT),
        name="shared_kv",
    )(x1, wf3, bias3, wk, wqT, wvT, hsum)


def _attn_kernel(n1_ref, ul_ref, qT_ref, cumT_ref, rb_ref, k_ref, dk_ref, vT_ref, y_ref,
                 qa_ref, o_ref, m_ref, s_ref, p_ref):
    b = pl.program_id(0)
    i = pl.program_id(1)
    n1 = n1_ref[b * pl.num_programs(1) + i]
    tq = qa_ref.shape[2]
    tk = k_ref.shape[3]

    rowi = lax.broadcasted_iota(jnp.int32, (LANES, tq), 0)
    zeros = jnp.zeros((FOX_DH, tq), BF)
    for h in range(FOX_HEADS):
        lo = (h % 2) * FOX_DH
        qa_ref[h, lo:lo + FOX_DH, :] = qT_ref[0, h]
        qa_ref[h, FOX_DH - lo:2 * FOX_DH - lo, :] = zeros
        piece_rows = (rowi == h) | (rowi == FOX_HEADS + h) | (rowi == 2 * FOX_HEADS + h)
        qa_ref[h, LANES:, :] = jnp.where(piece_rows, 1.0, 0.0).astype(BF)
    m_ref[...] = jnp.full(m_ref.shape, NEG, F32)
    o_ref[...] = jnp.zeros_like(o_ref)

    def logits(h, j, m_prev, slot, masked):
        lhs = jnp.concatenate([k_ref[0, h // 2, j], dk_ref[0, j]], axis=1)
        s = _mm(lhs, qa_ref[h])
        if masked:
            r = lax.broadcasted_iota(jnp.int32, (tk, tq), 0)
            c = lax.broadcasted_iota(jnp.int32, (tk, tq), 1)
            s = jnp.where(r <= c, s, NEG)
        a = cumT_ref[0, h] - jnp.tile(rb_ref[0, j * FOX_HEADS + h], (1, tq // LANES))
        m_cur = jnp.maximum(m_prev, jnp.max(s, axis=0, keepdims=True) + a)
        s_ref[slot] = s
        return m_cur, m_cur - a, jnp.exp2(m_prev - m_cur)

    def probs(slot, shift):
        p_ref[slot] = jnp.exp2(s_ref[slot] - shift).astype(BF)

    def values(h, j, slot, al):
        o_ref[h] = al * o_ref[h] + _mm(vT_ref[0, h, j], p_ref[slot])

    def sweep(unit, n, masked):
        (h0, j0), (h1, j1), (h2, j2), (h3, j3) = unit(0), unit(1), unit(2), unit(3)
        m, sh_e, a_e = logits(h0, j0, m_ref[h0], 0, masked)
        m_ref[h0] = m
        m, sh_o, a_o = logits(h1, j1, m_ref[h1], 1, masked)
        m_ref[h1] = m
        probs(0, sh_e)
        a_c = a_e
        m, sh_e, a_e = logits(h2, j2, m_ref[h2], 0, masked)
        m_ref[h2] = m
        probs(1, sh_o)
        values(h0, j0, 0, a_c)
        a_c = a_o
        m, sh_o, a_o = logits(h3, j3, m_ref[h3], 1, masked)
        m_ref[h3] = m

        def pair(g, carry):
            sh_e, a_e, sh_o, a_o, a_c = carry
            t = 2 * g
            (hp, jp), (he, je) = unit(t - 1), unit(t)
            (he2, je2), (ho2, jo2) = unit(t + 2), unit(t + 3)
            mp_e, mp_o = m_ref[he2], m_ref[ho2]
            probs(0, sh_e)
            m_e2, sh_e2, a_e2 = logits(he2, je2, mp_e, 0, masked)
            values(hp, jp, 1, a_c)
            probs(1, sh_o)
            mp_o = jnp.where(ho2 == he2, m_e2, mp_o)
            m_o2, sh_o2, a_o2 = logits(ho2, jo2, mp_o, 1, masked)
            values(he, je, 0, a_e)
            m_ref[he2] = m_e2
            m_ref[ho2] = m_o2
            return sh_e2, a_e2, sh_o2, a_o2, a_o

        sh_e, a_e, sh_o, a_o, a_c = lax.fori_loop(1, n // 2 - 1, pair, (sh_e, a_e, sh_o, a_o, a_c))
        (hp, jp), (he, je), (ho, jo) = unit(n - 3), unit(n - 2), unit(n - 1)
        probs(0, sh_e)
        values(hp, jp, 1, a_c)
        probs(1, sh_o)
        values(he, je, 0, a_e)
        values(ho, jo, 1, a_o)

    def listed_unit(t):
        code = ul_ref[0, 0, t]
        return code % FOX_HEADS, code // FOX_HEADS

    @pl.when(n1 > 0)
    def _():
        sweep(listed_unit, n1, False)

    sweep(lambda t: (t, i), FOX_HEADS, True)
    for h in range(FOX_HEADS):
        y_ref[0, h] = (o_ref[h, 0:FOX_DH, :] / o_ref[h, FOX_DH:FOX_DH + 1, :]).astype(BF)


def _attn_schedule(kst, qst):
    B, nt = kst.shape[0], kst.shape[1]
    cmax = kst[:, :, 0, :FOX_HEADS]
    cmin = kst[:, :, 1, :FOX_HEADS]
    kn = jnp.sqrt(kst[:, :, 2, :FOX_HEADS]) * NORM_SLACK
    qn = jnp.sqrt(qst[:, :, :, 0]) * NORM_SLACK
    upper = qn[:, :, None] * kn[:, None] + cmax[:, :, None] - cmin[:, None]
    lower = -qn * kn
    need = upper > lower[:, :, None] - UNDERFLOW_MARGIN
    ti = jnp.arange(nt)
    valid = (ti[None, :] < ti[:, None])[None, :, :, None]
    rank = jnp.where(valid & need, 0, jnp.where(valid, 1, 2)).reshape(B, nt, nt * FOX_HEADS)
    units = jnp.argsort(rank, axis=-1, stable=True).astype(jnp.int32)
    cnt = jnp.sum(rank == 0, axis=-1).astype(jnp.int32)
    n1 = jnp.where(cnt > 0, jnp.maximum(cnt + (cnt & 1), 4), 0)
    return n1.reshape(-1), units.reshape(B * nt, 1, nt * FOX_HEADS)


def _attention(qT, cumT, rb, kst, qst, k, dk, vT, tq):
    B, H, dh, S = qT.shape
    nt = S // tq
    n1, units = _attn_schedule(kst, qst)
    resident = dict(pipeline_mode=pl.Buffered(1))
    grid_spec = pltpu.PrefetchScalarGridSpec(
        num_scalar_prefetch=1,
        grid=(B, nt),
        in_specs=[
            pl.BlockSpec((1, 1, nt * H), lambda b, i, n1: (b * nt + i, 0, 0), memory_space=pltpu.SMEM),
            pl.BlockSpec((1, H, dh, tq), lambda b, i, n1: (b, 0, 0, i)),
            pl.BlockSpec((1, H, 1, tq), lambda b, i, n1: (b, 0, 0, i)),
            pl.BlockSpec((1, nt * H, 1, LANES), lambda b, i, n1: (b, 0, 0, 0)),
            pl.BlockSpec((1, HEAD_PAIRS, nt, tq, LANES), lambda b, i, n1: (b, 0, 0, 0, 0), **resident),
            pl.BlockSpec((1, nt, tq, LANES), lambda b, i, n1: (b, 0, 0, 0), **resident),
            pl.BlockSpec((1, H, nt, VROWS, tq), lambda b, i, n1: (b, 0, 0, 0, 0), **resident),
        ],
        out_specs=pl.BlockSpec((1, H, dh, tq), lambda b, i, n1: (b, 0, 0, i)),
        scratch_shapes=[
            pltpu.VMEM((H, QA_ROWS, tq), BF),
            pltpu.VMEM((H, VROWS, tq), F32),
            pltpu.VMEM((H, 1, tq), F32),
            pltpu.VMEM((2, tq, tq), F32),
            pltpu.VMEM((2, tq, tq), BF),
        ],
    )
    return pl.pallas_call(
        _attn_kernel,
        grid_spec=grid_spec,
        out_shape=jax.ShapeDtypeStruct((B, H, dh, S), BF),
        compiler_params=pltpu.CompilerParams(
            dimension_semantics=("arbitrary", "arbitrary"), vmem_limit_bytes=VMEM_LIMIT),
        name="fox_attn",
    )(n1, units, qT, cumT, rb, k, dk, vT)


def _layer1_kernel(x_ref, yT_ref, w_inT_ref, mk_ref, mvT_ref, w_outT_ref, lng_ref, lnb_ref,
                   o_ref, yg_ref):
    tc = x_ref.shape[1] // TOKEN_SPLIT
    for c in range(TOKEN_SPLIT):
        cols = slice(c * tc, (c + 1) * tc)
        x = x_ref[0, cols, :]
        xb = x.astype(BF)
        qmT = _nt(w_inT_ref[0:D_MEM, :], xb)
        gT = _nt(w_inT_ref[D_MEM:, :], xb)
        _finish_layer(x, yT_ref[0, :, cols].astype(F32), qmT, gT, mk_ref, mvT_ref, w_outT_ref,
                      lng_ref, lnb_ref, o_ref, yg_ref, cols)


def _layer1(x1, yT, w_inT, mk, mvT, w_outT, lng, lnb, tm):
    B, S, _ = x1.shape
    return pl.pallas_call(
        _layer1_kernel,
        grid=(B, S // tm),
        in_specs=[
            pl.BlockSpec((1, tm, D_MODEL), lambda b, t: (b, t, 0)),
            pl.BlockSpec((1, D_MAIN, tm), lambda b, t: (b, 0, t)),
            _const_spec(w_inT.shape),
            pl.BlockSpec((1, 1, MEM_HEADS, N_MEM, MEM_DH), lambda b, t: (1, b, 0, 0, 0)),
            pl.BlockSpec((1, 1, MEM_HEADS, MEM_DH, N_MEM), lambda b, t: (1, b, 0, 0, 0)),
            _const_spec(w_outT.shape),
            _const_spec(lng.shape),
            _const_spec(lnb.shape),
        ],
        out_specs=pl.BlockSpec((1, tm, D_MODEL), lambda b, t: (b, t, 0)),
        out_shape=jax.ShapeDtypeStruct((B, S, D_MODEL), F32),
        scratch_shapes=[pltpu.VMEM((D_MIX, tm), BF)],
        compiler_params=pltpu.CompilerParams(
            dimension_semantics=("arbitrary", "arbitrary"), vmem_limit_bytes=VMEM_LIMIT),
        name="layer1",
    )(x1, yT, w_inT, mk, mvT, w_outT, lng, lnb)


def kernel(x, mem, w_in, w_mem_kv, w_out, ln_g, ln_b, pool_w, pool_scale, w_kv_shared, b_forget):
    B, S, _ = x.shape
    tm = min(512, S)

    w_inT = jnp.swapaxes(w_in, 1, 2).astype(BF)
    w_outT = jnp.swapaxes(w_out, 1, 2).astype(BF)
    pool_wT = jnp.swapaxes(pool_w[0], 1, 2).astype(BF)
    ps = pool_scale[0].reshape(D_MAIN, 1)
    lng = ln_g.reshape(DEPTH, 1, D_MODEL)
    lnb = ln_b.reshape(DEPTH, 1, D_MODEL)
    wk = w_kv_shared[:, :D_MAIN].astype(BF)
    wvT = w_kv_shared[:, D_MAIN:2 * D_MAIN].T.astype(BF)
    wf = w_kv_shared[:, 2 * D_MAIN:]
    wf3 = jnp.concatenate([wf, wf, wf, jnp.zeros((D_MODEL, LANES - 3 * FOX_HEADS), F32)], axis=1).astype(BF)
    bias3 = jnp.concatenate([b_forget, b_forget, b_forget,
                             jnp.zeros((LANES - 3 * FOX_HEADS,), F32)]).reshape(1, LANES)

    mk, mvT = _mem_kv(mem, w_mem_kv.astype(BF))
    x1 = _layer0(x, w_inT[0], pool_wT, ps, mk, mvT, w_outT[0], lng[0], lnb[0], tm)
    k, dk, qT, vT, cumT, rb, kst, qst = _shared_kv(x1, wf3, bias3, wk, w_inT[1, :D_MAIN], wvT, tm)
    yT = _attention(qT, cumT, rb, kst, qst, k, dk, vT, tm)
    return _layer1(x1, yT.reshape(B, D_MAIN, S), w_inT[1, D_MAIN:], mk, mvT, w_outT[1],
                   lng[1], lnb[1], tm)
```

```python
import numpy as np
import jax
import jax.numpy as jnp
from jax import lax
from jax.experimental import pallas as pl
from jax.experimental.pallas import tpu as pltpu

D_MODEL = 1024
N_MEM = 256
D_MAIN = 1024
POOL_WINDOWS = (2, 4, 8, 16)
POOL_GROUP = 256
FOX_HEADS = 16
FOX_DH = 64
MEM_HEADS = 4
MEM_DH = 128
D_MEM = MEM_HEADS * MEM_DH
D_MIX = D_MAIN + D_MEM
DEPTH = 2
ALPHA = (2 * DEPTH) ** 0.25
LN_EPS = 1e-5

LANES = 128
BF16_ROWS = 16
HEAD_PAIRS = FOX_HEADS // 2
QA_ROWS = 2 * LANES
VROWS = FOX_DH + BF16_ROWS
NEG = -1e30
LOG2E = 1.4426950408889634
UNDERFLOW_MARGIN = 160.0
NORM_SLACK = 1.02
TOKEN_SPLIT = 2
VMEM_LIMIT = 56 * 1024 * 1024

BF = jnp.bfloat16
F32 = jnp.float32


def _mm(a, b):
    return jnp.dot(a, b, preferred_element_type=F32)


def _nt(a, b):
    return lax.dot_general(a, b, (((1,), (1,)), ((), ())), preferred_element_type=F32)


def _split3(v):
    hi = v.astype(BF)
    r1 = v - hi.astype(F32)
    mid = r1.astype(BF)
    lo = (r1 - mid.astype(F32)).astype(BF)
    return hi, mid, lo


def _silu(g):
    hg = 0.5 * g
    return hg + hg * jnp.tanh(hg)


def _const_spec(shape):
    nd = len(shape)
    return pl.BlockSpec(shape, lambda *_: (0,) * nd)


def _mem_kv_kernel(mem_ref, w_ref, mk_ref, mvT_ref):
    memb = mem_ref[0].astype(BF)
    mkv = _mm(memb, w_ref[0])
    for h in range(MEM_HEADS):
        mk_ref[0, 0, h] = mkv[:, h * MEM_DH:(h + 1) * MEM_DH].astype(BF)
        mv = mkv[:, D_MEM + h * MEM_DH:D_MEM + (h + 1) * MEM_DH]
        mvT_ref[0, 0, h] = mv.T.astype(BF)


def _mem_kv(mem, w_mem_kv_bf):
    B = mem.shape[0]
    L = w_mem_kv_bf.shape[0]
    return pl.pallas_call(
        _mem_kv_kernel,
        grid=(L, B),
        in_specs=[
            pl.BlockSpec((1, N_MEM, D_MODEL), lambda l, b: (b, 0, 0)),
            pl.BlockSpec((1, D_MODEL, 2 * D_MEM), lambda l, b: (l, 0, 0)),
        ],
        out_specs=[
            pl.BlockSpec((1, 1, MEM_HEADS, N_MEM, MEM_DH), lambda l, b: (l, b, 0, 0, 0)),
            pl.BlockSpec((1, 1, MEM_HEADS, MEM_DH, N_MEM), lambda l, b: (l, b, 0, 0, 0)),
        ],
        out_shape=[
            jax.ShapeDtypeStruct((L, B, MEM_HEADS, N_MEM, MEM_DH), BF),
            jax.ShapeDtypeStruct((L, B, MEM_HEADS, MEM_DH, N_MEM), BF),
        ],
        compiler_params=pltpu.CompilerParams(
            dimension_semantics=("arbitrary", "arbitrary"), vmem_limit_bytes=VMEM_LIMIT),
        name="mem_kv",
    )(mem, w_mem_kv_bf)


def _finish_layer(x, mainT, qmT, gT, mk_ref, mvT_ref, w_outT_ref, lng_ref, lnb_ref, o_ref, yg_ref, cols):
    yg_ref = yg_ref.at[:, cols]
    yg_ref[0:D_MAIN, :] = (mainT * _silu(gT[0:D_MAIN])).astype(BF)
    for h in range(MEM_HEADS):
        rows = slice(h * MEM_DH, (h + 1) * MEM_DH)
        qh = qmT[rows].astype(BF)
        lg = _mm(mk_ref[0, 0, h], qh) * (MEM_DH ** -0.5 * LOG2E)
        mx = jnp.max(lg, axis=0, keepdims=True)
        e = jnp.exp2(lg - mx)
        den = jnp.sum(e, axis=0, keepdims=True)
        ym = _mm(mvT_ref[0, 0, h], e.astype(BF)) * (1.0 / den)
        gm = gT[D_MAIN + h * MEM_DH:D_MAIN + (h + 1) * MEM_DH]
        yg_ref[D_MAIN + h * MEM_DH:D_MAIN + (h + 1) * MEM_DH, :] = (ym * _silu(gm)).astype(BF)
    outT = _mm(w_outT_ref[...], yg_ref[...])
    z = ALPHA * x + outT.T
    mu = jnp.mean(z, axis=-1, keepdims=True)
    zc = z - mu
    var = jnp.mean(zc * zc, axis=-1, keepdims=True)
    o_ref[0, cols, :] = zc * lax.rsqrt(var + LN_EPS) * lng_ref[...] + lnb_ref[...]


def _layer0_kernel(x_ref, w_inT_ref, band_ref, pool_wT_ref, ps_ref, mk_ref, mvT_ref, w_outT_ref,
                   lng_ref, lnb_ref, o_ref, tail_ref, main_ref, yg_ref):
    t = pl.program_id(1)
    x = x_ref[0]
    tm = x.shape[0]
    xb = x.astype(BF)
    uT = _nt(w_inT_ref[0:D_MAIN, :], xb)
    qmT = _nt(w_inT_ref[D_MAIN:D_MIX, :], xb)
    gT = _nt(w_inT_ref[D_MIX:, :], xb)

    @pl.when(t == 0)
    def _():
        tail_ref[...] = jnp.zeros_like(tail_ref)

    tpos = t * tm + lax.broadcasted_iota(jnp.int32, (1, tm), 1)
    ub = uT.astype(BF)
    for g, w in enumerate(POOL_WINDOWS):
        rows = slice(g * POOL_GROUP, (g + 1) * POOL_GROUP)
        ext = jnp.concatenate([tail_ref[rows, :], ub[rows]], axis=1)
        wsum = jnp.concatenate(
            [_mm(ext[:, n * LANES:(n + 2) * LANES], band_ref[g]) for n in range(tm // LANES)], axis=1)
        inv_cnt = 1.0 / jnp.minimum(tpos + 1, w).astype(F32)
        pm = wsum * inv_cnt - uT[rows]
        main_ref[rows, :] = _mm(pool_wT_ref[g], pm.astype(BF)) * ps_ref[rows, :]
    tail_ref[...] = ub[:, tm - LANES:]

    _finish_layer(x, main_ref[...], qmT, gT, mk_ref, mvT_ref, w_outT_ref, lng_ref, lnb_ref,
                  o_ref, yg_ref, slice(0, tm))


def _pool_bands():
    s = np.arange(2 * LANES)[:, None]
    c = np.arange(LANES)[None, :] + LANES
    return np.stack([((s > c - w) & (s <= c)).astype(np.float32) for w in POOL_WINDOWS])


def _layer0(x, w_inT, pool_wT, ps, mk, mvT, w_outT, lng, lnb, tm):
    B, S, _ = x.shape
    band = jnp.asarray(_pool_bands(), BF)
    return pl.pallas_call(
        _layer0_kernel,
        grid=(B, S // tm),
        in_specs=[
            pl.BlockSpec((1, tm, D_MODEL), lambda b, t: (b, t, 0)),
            _const_spec(w_inT.shape),
            _const_spec(band.shape),
            _const_spec(pool_wT.shape),
            _const_spec(ps.shape),
            pl.BlockSpec((1, 1, MEM_HEADS, N_MEM, MEM_DH), lambda b, t: (0, b, 0, 0, 0)),
            pl.BlockSpec((1, 1, MEM_HEADS, MEM_DH, N_MEM), lambda b, t: (0, b, 0, 0, 0)),
            _const_spec(w_outT.shape),
            _const_spec(lng.shape),
            _const_spec(lnb.shape),
        ],
        out_specs=pl.BlockSpec((1, tm, D_MODEL), lambda b, t: (b, t, 0)),
        out_shape=jax.ShapeDtypeStruct((B, S, D_MODEL), F32),
        scratch_shapes=[
            pltpu.VMEM((D_MAIN, LANES), BF),
            pltpu.VMEM((D_MAIN, tm), F32),
            pltpu.VMEM((D_MIX, tm), BF),
        ],
        compiler_params=pltpu.CompilerParams(
            dimension_semantics=("arbitrary", "arbitrary"), vmem_limit_bytes=VMEM_LIMIT),
        name="layer0",
    )(x, w_inT, band, pool_wT, ps, mk, mvT, w_outT, lng, lnb)


def _shared_kv_kernel(x_ref, wf3_ref, bias3_ref, wk_ref, wqT_ref, wvT_ref, hsum_ref,
                      k_ref, dk_ref, qT_ref, vT_ref, cumT_ref, rb_ref, kst_ref, qst_ref, carry_ref):
    t = pl.program_id(1)

    @pl.when(t == 0)
    def _():
        carry_ref[...] = jnp.zeros_like(carry_ref)

    xb = x_ref[0].astype(BF)
    tm = xb.shape[0]
    lane = lax.broadcasted_iota(jnp.int32, (tm, LANES), 1)
    f3 = _mm(xb, wf3_ref[...]) + bias3_ref[...]
    lf = jnp.where(lane < 3 * FOX_HEADS, jax.nn.log_sigmoid(f3), 0.0)

    cb = min(256, tm)
    ri = lax.broadcasted_iota(jnp.int32, (cb, cb), 0)
    ci = lax.broadcasted_iota(jnp.int32, (cb, cb), 1)
    tri = jnp.where(ci <= ri, 1.0, 0.0).astype(BF)
    off = jnp.zeros((1, LANES), F32)
    blocks = []
    for i in range(tm // cb):
        hi, mid, lo = _split3(lf[i * cb:(i + 1) * cb])
        c = _mm(tri, hi) + _mm(tri, mid) + _mm(tri, lo) + off
        off = c[cb - 1:cb, :]
        blocks.append(c)
    c_loc = jnp.concatenate(blocks, axis=0) * LOG2E
    cum = c_loc + carry_ref[...]
    carry_ref[...] = cum[tm - 1:tm, :]

    hi, mid, lo = (v.astype(F32) for v in _split3(c_loc[0:1, :] - c_loc))
    dk_ref[0, 0] = jnp.where(lane < FOX_HEADS, hi,
                             jnp.where(lane < 2 * FOX_HEADS, mid,
                                       jnp.where(lane < 3 * FOX_HEADS, lo, 0.0))).astype(BF)
    kb = _mm(xb, wk_ref[...]).astype(BF)
    for g in range(HEAD_PAIRS):
        k_ref[0, g, 0] = kb[:, g * LANES:(g + 1) * LANES]

    qT = (_nt(wqT_ref[...], xb) * (FOX_DH ** -0.5 * LOG2E)).astype(BF)
    vT = _nt(wvT_ref[...], xb).astype(BF)

    kf32 = kb.astype(F32)
    kn2 = jnp.max(_mm((kf32 * kf32).astype(BF), hsum_ref[...]), axis=0, keepdims=True)
    cmax = jnp.max(cum, axis=0, keepdims=True)
    cmin = jnp.min(cum, axis=0, keepdims=True)
    srow = lax.broadcasted_iota(jnp.int32, (8, LANES), 0)
    kst_ref[0, 0] = jnp.where(srow == 0, cmax, jnp.where(srow == 1, cmin, jnp.where(srow == 2, kn2, 0.0)))
    qf32 = qT.astype(F32)
    qn2 = jnp.sum((qf32 * qf32).reshape(FOX_HEADS, FOX_DH, tm), axis=1)
    qst_ref[0, 0] = jnp.broadcast_to(jnp.max(qn2, axis=1, keepdims=True), (FOX_HEADS, LANES))
    ones_rows = jnp.where(lax.broadcasted_iota(jnp.int32, (BF16_ROWS, tm), 0) == 0, 1.0, 0.0).astype(BF)
    for h in range(FOX_HEADS):
        qT_ref[0, h] = qT[h * FOX_DH:(h + 1) * FOX_DH]
        vT_ref[0, h, 0, 0:FOX_DH, :] = vT[h * FOX_DH:(h + 1) * FOX_DH]
        vT_ref[0, h, 0, FOX_DH:, :] = ones_rows

    cT = cum.T
    for h in range(FOX_HEADS):
        cumT_ref[0, h] = cT[h:h + 1]
        rb_ref[0, h] = jnp.broadcast_to(cT[h:h + 1, 0:1], (1, LANES))


def _head_sum_matrix():
    g = np.zeros((D_MAIN, LANES), np.float32)
    for h in range(FOX_HEADS):
        g[h * FOX_DH:(h + 1) * FOX_DH, h] = 1.0
    return g


def _shared_kv(x1, wf3, bias3, wk, wqT, wvT, tm):
    B, S, _ = x1.shape
    nt = S // tm
    hsum = jnp.asarray(_head_sum_matrix(), BF)
    return pl.pallas_call(
        _shared_kv_kernel,
        grid=(B, nt),
        in_specs=[
            pl.BlockSpec((1, tm, D_MODEL), lambda b, t: (b, t, 0)),
            _const_spec(wf3.shape),
            _const_spec(bias3.shape),
            _const_spec(wk.shape),
            _const_spec(wqT.shape),
            _const_spec(wvT.shape),
            _const_spec(hsum.shape),
        ],
        out_specs=[
            pl.BlockSpec((1, HEAD_PAIRS, 1, tm, LANES), lambda b, t: (b, 0, t, 0, 0)),
            pl.BlockSpec((1, 1, tm, LANES), lambda b, t: (b, t, 0, 0)),
            pl.BlockSpec((1, FOX_HEADS, FOX_DH, tm), lambda b, t: (b, 0, 0, t)),
            pl.BlockSpec((1, FOX_HEADS, 1, VROWS, tm), lambda b, t: (b, 0, t, 0, 0)),
            pl.BlockSpec((1, FOX_HEADS, 1, tm), lambda b, t: (b, 0, 0, t)),
            pl.BlockSpec((1, FOX_HEADS, 1, LANES), lambda b, t: (b, t, 0, 0)),
            pl.BlockSpec((1, 1, 8, LANES), lambda b, t: (b, t, 0, 0)),
            pl.BlockSpec((1, 1, FOX_HEADS, LANES), lambda b, t: (b, t, 0, 0)),
        ],
        out_shape=[
            jax.ShapeDtypeStruct((B, HEAD_PAIRS, nt, tm, LANES), BF),
            jax.ShapeDtypeStruct((B, nt, tm, LANES), BF),
            jax.ShapeDtypeStruct((B, FOX_HEADS, FOX_DH, S), BF),
            jax.ShapeDtypeStruct((B, FOX_HEADS, nt, VROWS, tm), BF),
            jax.ShapeDtypeStruct((B, FOX_HEADS, 1, S), F32),
            jax.ShapeDtypeStruct((B, nt * FOX_HEADS, 1, LANES), F32),
            jax.ShapeDtypeStruct((B, nt, 8, LANES), F32),
            jax.ShapeDtypeStruct((B, nt, FOX_HEADS, LANES), F32),
        ],
        scratch_shapes=[pltpu.VMEM((1, LANES), F32)],
        compiler_params=pltpu.CompilerParams(
            dimension_semantics=("arbitrary", "arbitrary"), vmem_limit_bytes=VMEM_LIMIT),
        name="shared_kv",
    )(x1, wf3, bias3, wk, wqT, wvT, hsum)


def _attn_kernel(n1_ref, ul_ref, qT_ref, cumT_ref, rb_ref, k_ref, dk_ref, vT_ref, y_ref,
                 qa_ref, o_ref, m_ref, s_ref, p_ref):
    b = pl.program_id(0)
    i = pl.program_id(1)
    n1 = n1_ref[b * pl.num_programs(1) + i]
    tq = qa_ref.shape[2]
    tk = k_ref.shape[3]

    rowi = lax.broadcasted_iota(jnp.int32, (LANES, tq), 0)
    zeros = jnp.zeros((FOX_DH, tq), BF)
    for h in range(FOX_HEADS):
        lo = (h % 2) * FOX_DH
        qa_ref[h, lo:lo + FOX_DH, :] = qT_ref[0, h]
        qa_ref[h, FOX_DH - lo:2 * FOX_DH - lo, :] = zeros
        piece_rows = (rowi == h) | (rowi == FOX_HEADS + h) | (rowi == 2 * FOX_HEADS + h)
        qa_ref[h, LANES:, :] = jnp.where(piece_rows, 1.0, 0.0).astype(BF)
    m_ref[...] = jnp.full(m_ref.shape, NEG, F32)
    o_ref[...] = jnp.zeros_like(o_ref)

    def logits(h, j, m_prev, slot, masked):
        pair_idx = lax.shift_right_logical(jnp.int32(h), 1)
        lhs = jnp.concatenate([k_ref[0, pair_idx, j], dk_ref[0, j]], axis=1)
        s = _mm(lhs, qa_ref[h])
        if masked:
            r = lax.broadcasted_iota(jnp.int32, (tk, tq), 0)
            c = lax.broadcasted_iota(jnp.int32, (tk, tq), 1)
            s = jnp.where(r <= c, s, NEG)
        a = cumT_ref[0, h] - jnp.tile(rb_ref[0, j * FOX_HEADS + h], (1, tq // LANES))
        m_cur = jnp.maximum(m_prev, jnp.max(s, axis=0, keepdims=True) + a)
        s_ref[slot] = s
        return m_cur, m_cur - a, jnp.exp2(m_prev - m_cur)

    def probs(slot, shift):
        p_ref[slot] = jnp.exp2(s_ref[slot] - shift).astype(BF)

    def values(h, j, slot, al):
        o_ref[h] = al * o_ref[h] + _mm(vT_ref[0, h, j], p_ref[slot])

    def sweep(unit, n, masked):
        (h0, j0), (h1, j1), (h2, j2), (h3, j3) = unit(0), unit(1), unit(2), unit(3)
        m, sh_e, a_e = logits(h0, j0, m_ref[h0], 0, masked)
        m_ref[h0] = m
        m, sh_o, a_o = logits(h1, j1, m_ref[h1], 1, masked)
        m_ref[h1] = m
        probs(0, sh_e)
        a_c = a_e
        m, sh_e, a_e = logits(h2, j2, m_ref[h2], 0, masked)
        m_ref[h2] = m
        probs(1, sh_o)
        values(h0, j0, 0, a_c)
        a_c = a_o
        m, sh_o, a_o = logits(h3, j3, m_ref[h3], 1, masked)
        m_ref[h3] = m

        def pair(g, carry):
            sh_e, a_e, sh_o, a_o, a_c = carry
            t = 2 * g
            (hp, jp), (he, je) = unit(t - 1), unit(t)
            (he2, je2), (ho2, jo2) = unit(t + 2), unit(t + 3)
            mp_e, mp_o = m_ref[he2], m_ref[ho2]
            probs(0, sh_e)
            m_e2, sh_e2, a_e2 = logits(he2, je2, mp_e, 0, masked)
            values(hp, jp, 1, a_c)
            probs(1, sh_o)
            mp_o = jnp.where(ho2 == he2, m_e2, mp_o)
            m_o2, sh_o2, a_o2 = logits(ho2, jo2, mp_o, 1, masked)
            values(he, je, 0, a_e)
            m_ref[he2] = m_e2
            m_ref[ho2] = m_o2
            return sh_e2, a_e2, sh_o2, a_o2, a_o

        sh_e, a_e, sh_o, a_o, a_c = lax.fori_loop(1, n // 2 - 1, pair, (sh_e, a_e, sh_o, a_o, a_c))
        (hp, jp), (he, je), (ho, jo) = unit(n - 3), unit(n - 2), unit(n - 1)
        probs(0, sh_e)
        values(hp, jp, 1, a_c)
        probs(1, sh_o)
        values(he, je, 0, a_e)
        values(ho, jo, 1, a_o)

    def listed_unit(t):
        code = ul_ref[0, 0, t]
        return code & (FOX_HEADS - 1), lax.shift_right_logical(code, 4)

    @pl.when(n1 > 0)
    def _():
        sweep(listed_unit, n1, False)

    sweep(lambda t: (t, i), FOX_HEADS, True)
    for h in range(FOX_HEADS):
        y_ref[0, h] = (o_ref[h, 0:FOX_DH, :] / o_ref[h, FOX_DH:FOX_DH + 1, :]).astype(BF)


def _attn_schedule(kst, qst):
    B, nt = kst.shape[0], kst.shape[1]
    cmax = kst[:, :, 0, :FOX_HEADS]
    cmin = kst[:, :, 1, :FOX_HEADS]
    kn = jnp.sqrt(kst[:, :, 2, :FOX_HEADS]) * NORM_SLACK
    qn = jnp.sqrt(qst[:, :, :, 0]) * NORM_SLACK
    upper = qn[:, :, None] * kn[:, None] + cmax[:, :, None] - cmin[:, None]
    lower = -qn * kn
    need = upper > lower[:, :, None] - UNDERFLOW_MARGIN
    ti = jnp.arange(nt)
    valid = (ti[None, :] < ti[:, None])[None, :, :, None]
    rank = jnp.where(valid & need, 0, jnp.where(valid, 1, 2)).reshape(B, nt, nt * FOX_HEADS)
    units = jnp.argsort(rank, axis=-1, stable=True).astype(jnp.int32)
    cnt = jnp.sum(rank == 0, axis=-1).astype(jnp.int32)
    n1 = jnp.where(cnt > 0, jnp.maximum(cnt + (cnt & 1), 4), 0)
    return n1.reshape(-1), units.reshape(B * nt, 1, nt * FOX_HEADS)


def _attention(qT, cumT, rb, kst, qst, k, dk, vT, tq):
    B, H, dh, S = qT.shape
    nt = S // tq
    n1, units = _attn_schedule(kst, qst)
    resident = dict(pipeline_mode=pl.Buffered(1))
    grid_spec = pltpu.PrefetchScalarGridSpec(
        num_scalar_prefetch=1,
        grid=(B, nt),
        in_specs=[
            pl.BlockSpec((1, 1, nt * H), lambda b, i, n1: (b * nt + i, 0, 0), memory_space=pltpu.SMEM),
            pl.BlockSpec((1, H, dh, tq), lambda b, i, n1: (b, 0, 0, i)),
            pl.BlockSpec((1, H, 1, tq), lambda b, i, n1: (b, 0, 0, i)),
            pl.BlockSpec((1, nt * H, 1, LANES), lambda b, i, n1: (b, 0, 0, 0)),
            pl.BlockSpec((1, HEAD_PAIRS, nt, tq, LANES), lambda b, i, n1: (b, 0, 0, 0, 0), **resident),
            pl.BlockSpec((1, nt, tq, LANES), lambda b, i, n1: (b, 0, 0, 0), **resident),
            pl.BlockSpec((1, H, nt, VROWS, tq), lambda b, i, n1: (b, 0, 0, 0, 0), **resident),
        ],
        out_specs=pl.BlockSpec((1, H, dh, tq), lambda b, i, n1: (b, 0, 0, i)),
        scratch_shapes=[
            pltpu.VMEM((H, QA_ROWS, tq), BF),
            pltpu.VMEM((H, VROWS, tq), F32),
            pltpu.VMEM((H, 1, tq), F32),
            pltpu.VMEM((2, tq, tq), F32),
            pltpu.VMEM((2, tq, tq), BF),
        ],
    )
    return pl.pallas_call(
        _attn_kernel,
        grid_spec=grid_spec,
        out_shape=jax.ShapeDtypeStruct((B, H, dh, S), BF),
        compiler_params=pltpu.CompilerParams(
            dimension_semantics=("arbitrary", "arbitrary"), vmem_limit_bytes=VMEM_LIMIT),
        name="fox_attn",
    )(n1, units, qT, cumT, rb, k, dk, vT)


def _layer1_kernel(x_ref, yT_ref, w_inT_ref, mk_ref, mvT_ref, w_outT_ref, lng_ref, lnb_ref,
                   o_ref, yg_ref, h_ref, xs_ref, ys_ref):
    t = pl.program_id(1)
    tm = xs_ref.shape[0]

    @pl.when(t == 0)
    def _():
        h_ref[...] = jnp.zeros_like(h_ref)
        xs_ref[...] = jnp.zeros_like(xs_ref)
        ys_ref[...] = jnp.zeros_like(ys_ref)

    _finish_layer(xs_ref[...], ys_ref[...].astype(F32), h_ref[0:D_MEM, :], h_ref[D_MEM:, :],
                  mk_ref, mvT_ref, w_outT_ref, lng_ref, lnb_ref, o_ref, yg_ref, slice(0, tm))

    x = x_ref[0]
    h_ref[...] = _nt(w_inT_ref[...], x.astype(BF))
    xs_ref[...] = x
    ys_ref[...] = yT_ref[0]


def _layer1(x1, yT, w_inT, mk, mvT, w_outT, lng, lnb, tm):
    B, S, _ = x1.shape
    nt = S // tm
    cur = lambda t: jnp.minimum(t, nt - 1)
    return pl.pallas_call(
        _layer1_kernel,
        grid=(B, nt + 1),
        in_specs=[
            pl.BlockSpec((1, tm, D_MODEL), lambda b, t: (b, cur(t), 0)),
            pl.BlockSpec((1, D_MAIN, tm), lambda b, t: (b, 0, cur(t))),
            _const_spec(w_inT.shape),
            pl.BlockSpec((1, 1, MEM_HEADS, N_MEM, MEM_DH), lambda b, t: (1, b, 0, 0, 0)),
            pl.BlockSpec((1, 1, MEM_HEADS, MEM_DH, N_MEM), lambda b, t: (1, b, 0, 0, 0)),
            _const_spec(w_outT.shape),
            _const_spec(lng.shape),
            _const_spec(lnb.shape),
        ],
        out_specs=pl.BlockSpec((1, tm, D_MODEL), lambda b, t: (b, jnp.maximum(t - 1, 0), 0)),
        out_shape=jax.ShapeDtypeStruct((B, S, D_MODEL), F32),
        scratch_shapes=[
            pltpu.VMEM((D_MIX, tm), BF),
            pltpu.VMEM((D_MEM + D_MIX, tm), F32),
            pltpu.VMEM((tm, D_MODEL), F32),
            pltpu.VMEM((D_MAIN, tm), BF),
        ],
        compiler_params=pltpu.CompilerParams(
            dimension_semantics=("arbitrary", "arbitrary"), vmem_limit_bytes=VMEM_LIMIT),
        name="layer1",
    )(x1, yT, w_inT, mk, mvT, w_outT, lng, lnb)


def kernel(x, mem, w_in, w_mem_kv, w_out, ln_g, ln_b, pool_w, pool_scale, w_kv_shared, b_forget):
    B, S, _ = x.shape
    tm = min(512, S)

    w_inT = jnp.swapaxes(w_in, 1, 2).astype(BF)
    w_outT = jnp.swapaxes(w_out, 1, 2).astype(BF)
    pool_wT = jnp.swapaxes(pool_w[0], 1, 2).astype(BF)
    ps = pool_scale[0].reshape(D_MAIN, 1)
    lng = ln_g.reshape(DEPTH, 1, D_MODEL)
    lnb = ln_b.reshape(DEPTH, 1, D_MODEL)
    wk = w_kv_shared[:, :D_MAIN].astype(BF)
    wvT = w_kv_shared[:, D_MAIN:2 * D_MAIN].T.astype(BF)
    wf = w_kv_shared[:, 2 * D_MAIN:]
    wf3 = jnp.concatenate([wf, wf, wf, jnp.zeros((D_MODEL, LANES - 3 * FOX_HEADS), F32)], axis=1).astype(BF)
    bias3 = jnp.concatenate([b_forget, b_forget, b_forget,
                             jnp.zeros((LANES - 3 * FOX_HEADS,), F32)]).reshape(1, LANES)

    mk, mvT = _mem_kv(mem, w_mem_kv.astype(BF))
    x1 = _layer0(x, w_inT[0], pool_wT, ps, mk, mvT, w_outT[0], lng[0], lnb[0], tm)
    k, dk, qT, vT, cumT, rb, kst, qst = _shared_kv(x1, wf3, bias3, wk, w_inT[1, :D_MAIN], wvT, tm)
    yT = _attention(qT, cumT, rb, kst, qst, k, dk, vT, tm)
    return _layer1(x1, yT.reshape(B, D_MAIN, S), w_inT[1, D_MAIN:], mk, mvT, w_outT[1],
                   lng[1], lnb[1], tm)
```

```python
import numpy as np
import jax
import jax.numpy as jnp
from jax import lax
from jax.experimental import pallas as pl
from jax.experimental.pallas import tpu as pltpu

D_MODEL = 1024
N_MEM = 256
D_MAIN = 1024
POOL_WINDOWS = (2, 4, 8, 16)
POOL_GROUP = 256
FOX_HEADS = 16
FOX_DH = 64
MEM_HEADS = 4
MEM_DH = 128
D_MEM = MEM_HEADS * MEM_DH
D_MIX = D_MAIN + D_MEM
DEPTH = 2
ALPHA = (2 * DEPTH) ** 0.25
LN_EPS = 1e-5

LANES = 128
BF16_ROWS = 16
HEAD_PAIRS = FOX_HEADS // 2
QA_ROWS = 2 * LANES
VROWS = FOX_DH + BF16_ROWS
NEG = -1e30
LOG2E = 1.4426950408889634
UNDERFLOW_MARGIN = 160.0
NORM_SLACK = 1.02
TOKEN_SPLIT = 2
VMEM_LIMIT = 56 * 1024 * 1024

BF = jnp.bfloat16
F32 = jnp.float32


def _mm(a, b):
    return jnp.dot(a, b, preferred_element_type=F32)


def _nt(a, b):
    return lax.dot_general(a, b, (((1,), (1,)), ((), ())), preferred_element_type=F32)


def _split3(v):
    hi = v.astype(BF)
    r1 = v - hi.astype(F32)
    mid = r1.astype(BF)
    lo = (r1 - mid.astype(F32)).astype(BF)
    return hi, mid, lo


def _silu(g):
    hg = 0.5 * g
    return hg + hg * jnp.tanh(hg)


def _const_spec(shape):
    nd = len(shape)
    return pl.BlockSpec(shape, lambda *_: (0,) * nd)


def _mem_kv_kernel(mem_ref, w_ref, mk_ref, mvT_ref):
    memb = mem_ref[0].astype(BF)
    mkv = _mm(memb, w_ref[0])
    for h in range(MEM_HEADS):
        mk_ref[0, 0, h] = mkv[:, h * MEM_DH:(h + 1) * MEM_DH].astype(BF)
        mv = mkv[:, D_MEM + h * MEM_DH:D_MEM + (h + 1) * MEM_DH]
        mvT_ref[0, 0, h] = mv.T.astype(BF)


def _mem_kv(mem, w_mem_kv_bf):
    B = mem.shape[0]
    L = w_mem_kv_bf.shape[0]
    return pl.pallas_call(
        _mem_kv_kernel,
        grid=(L, B),
        in_specs=[
            pl.BlockSpec((1, N_MEM, D_MODEL), lambda l, b: (b, 0, 0)),
            pl.BlockSpec((1, D_MODEL, 2 * D_MEM), lambda l, b: (l, 0, 0)),
        ],
        out_specs=[
            pl.BlockSpec((1, 1, MEM_HEADS, N_MEM, MEM_DH), lambda l, b: (l, b, 0, 0, 0)),
            pl.BlockSpec((1, 1, MEM_HEADS, MEM_DH, N_MEM), lambda l, b: (l, b, 0, 0, 0)),
        ],
        out_shape=[
            jax.ShapeDtypeStruct((L, B, MEM_HEADS, N_MEM, MEM_DH), BF),
            jax.ShapeDtypeStruct((L, B, MEM_HEADS, MEM_DH, N_MEM), BF),
        ],
        compiler_params=pltpu.CompilerParams(
            dimension_semantics=("arbitrary", "arbitrary"), vmem_limit_bytes=VMEM_LIMIT),
        name="mem_kv",
    )(mem, w_mem_kv_bf)


def _finish_layer(x, mainT, qmT, gT, mk_ref, mvT_ref, w_outT_ref, lng_ref, lnb_ref, o_ref, yg_ref, cols):
    yg_ref = yg_ref.at[:, cols]
    yg_ref[0:D_MAIN, :] = (mainT * _silu(gT[0:D_MAIN])).astype(BF)
    for h in range(MEM_HEADS):
        rows = slice(h * MEM_DH, (h + 1) * MEM_DH)
        qh = qmT[rows].astype(BF)
        lg = _mm(mk_ref[0, 0, h], qh) * (MEM_DH ** -0.5 * LOG2E)
        mx = jnp.max(lg, axis=0, keepdims=True)
        e = jnp.exp2(lg - mx)
        den = jnp.sum(e, axis=0, keepdims=True)
        ym = _mm(mvT_ref[0, 0, h], e.astype(BF)) * (1.0 / den)
        gm = gT[D_MAIN + h * MEM_DH:D_MAIN + (h + 1) * MEM_DH]
        yg_ref[D_MAIN + h * MEM_DH:D_MAIN + (h + 1) * MEM_DH, :] = (ym * _silu(gm)).astype(BF)
    outT = _mm(w_outT_ref[...], yg_ref[...])
    z = ALPHA * x + outT.T
    mu = jnp.mean(z, axis=-1, keepdims=True)
    zc = z - mu
    var = jnp.mean(zc * zc, axis=-1, keepdims=True)
    o_ref[0, cols, :] = zc * lax.rsqrt(var + LN_EPS) * lng_ref[...] + lnb_ref[...]


def _layer0_kernel(x_ref, w_inT_ref, band_ref, pool_wT_ref, ps_ref, mk_ref, mvT_ref, w_outT_ref,
                   lng_ref, lnb_ref, o_ref, tail_ref, main_ref, yg_ref):
    t = pl.program_id(1)
    x = x_ref[0]
    tm = x.shape[0]
    xb = x.astype(BF)
    uT = _nt(w_inT_ref[0:D_MAIN, :], xb)
    qmT = _nt(w_inT_ref[D_MAIN:D_MIX, :], xb)
    gT = _nt(w_inT_ref[D_MIX:, :], xb)

    @pl.when(t == 0)
    def _():
        tail_ref[...] = jnp.zeros_like(tail_ref)

    tpos = t * tm + lax.broadcasted_iota(jnp.int32, (1, tm), 1)
    ub = uT.astype(BF)
    for g, w in enumerate(POOL_WINDOWS):
        rows = slice(g * POOL_GROUP, (g + 1) * POOL_GROUP)
        ext = jnp.concatenate([tail_ref[rows, :], ub[rows]], axis=1)
        wsum = jnp.concatenate(
            [_mm(ext[:, n * LANES:(n + 2) * LANES], band_ref[g]) for n in range(tm // LANES)], axis=1)
        inv_cnt = 1.0 / jnp.minimum(tpos + 1, w).astype(F32)
        pm = wsum * inv_cnt - uT[rows]
        main_ref[rows, :] = _mm(pool_wT_ref[g], pm.astype(BF)) * ps_ref[rows, :]
    tail_ref[...] = ub[:, tm - LANES:]

    _finish_layer(x, main_ref[...], qmT, gT, mk_ref, mvT_ref, w_outT_ref, lng_ref, lnb_ref,
                  o_ref, yg_ref, slice(0, tm))


def _pool_bands():
    s = np.arange(2 * LANES)[:, None]
    c = np.arange(LANES)[None, :] + LANES
    return np.stack([((s > c - w) & (s <= c)).astype(np.float32) for w in POOL_WINDOWS])


def _layer0(x, w_inT, pool_wT, ps, mk, mvT, w_outT, lng, lnb, tm):
    B, S, _ = x.shape
    band = jnp.asarray(_pool_bands(), BF)
    return pl.pallas_call(
        _layer0_kernel,
        grid=(B, S // tm),
        in_specs=[
            pl.BlockSpec((1, tm, D_MODEL), lambda b, t: (b, t, 0)),
            _const_spec(w_inT.shape),
            _const_spec(band.shape),
            _const_spec(pool_wT.shape),
            _const_spec(ps.shape),
            pl.BlockSpec((1, 1, MEM_HEADS, N_MEM, MEM_DH), lambda b, t: (0, b, 0, 0, 0)),
            pl.BlockSpec((1, 1, MEM_HEADS, MEM_DH, N_MEM), lambda b, t: (0, b, 0, 0, 0)),
            _const_spec(w_outT.shape),
            _const_spec(lng.shape),
            _const_spec(lnb.shape),
        ],
        out_specs=pl.BlockSpec((1, tm, D_MODEL), lambda b, t: (b, t, 0)),
        out_shape=jax.ShapeDtypeStruct((B, S, D_MODEL), F32),
        scratch_shapes=[
            pltpu.VMEM((D_MAIN, LANES), BF),
            pltpu.VMEM((D_MAIN, tm), F32),
            pltpu.VMEM((D_MIX, tm), BF),
        ],
        compiler_params=pltpu.CompilerParams(
            dimension_semantics=("arbitrary", "arbitrary"), vmem_limit_bytes=VMEM_LIMIT),
        name="layer0",
    )(x, w_inT, band, pool_wT, ps, mk, mvT, w_outT, lng, lnb)


def _shared_kv_kernel(x_ref, wf3_ref, bias3_ref, wk_ref, wqT_ref, wvT_ref, hsum_ref,
                      k_ref, dk_ref, qT_ref, vT_ref, cumT_ref, rb_ref, kst_ref, qst_ref, carry_ref):
    t = pl.program_id(1)

    @pl.when(t == 0)
    def _():
        carry_ref[...] = jnp.zeros_like(carry_ref)

    xb = x_ref[0].astype(BF)
    tm = xb.shape[0]
    lane = lax.broadcasted_iota(jnp.int32, (tm, LANES), 1)
    f3 = _mm(xb, wf3_ref[...]) + bias3_ref[...]
    lf = jnp.where(lane < 3 * FOX_HEADS, jax.nn.log_sigmoid(f3), 0.0)

    cb = min(256, tm)
    ri = lax.broadcasted_iota(jnp.int32, (cb, cb), 0)
    ci = lax.broadcasted_iota(jnp.int32, (cb, cb), 1)
    tri = jnp.where(ci <= ri, 1.0, 0.0).astype(BF)
    off = jnp.zeros((1, LANES), F32)
    blocks = []
    for i in range(tm // cb):
        hi, mid, lo = _split3(lf[i * cb:(i + 1) * cb])
        c = _mm(tri, hi) + _mm(tri, mid) + _mm(tri, lo) + off
        off = c[cb - 1:cb, :]
        blocks.append(c)
    c_loc = jnp.concatenate(blocks, axis=0) * LOG2E
    cum = c_loc + carry_ref[...]
    carry_ref[...] = cum[tm - 1:tm, :]

    hi, mid, lo = (v.astype(F32) for v in _split3(c_loc[0:1, :] - c_loc))
    dk_ref[0, 0] = jnp.where(lane < FOX_HEADS, hi,
                             jnp.where(lane < 2 * FOX_HEADS, mid,
                                       jnp.where(lane < 3 * FOX_HEADS, lo, 0.0))).astype(BF)
    kb = _mm(xb, wk_ref[...]).astype(BF)
    for g in range(HEAD_PAIRS):
        k_ref[0, g, 0] = kb[:, g * LANES:(g + 1) * LANES]

    qT = (_nt(wqT_ref[...], xb) * (FOX_DH ** -0.5 * LOG2E)).astype(BF)
    vT = _nt(wvT_ref[...], xb).astype(BF)

    kf32 = kb.astype(F32)
    kn2 = jnp.max(_mm((kf32 * kf32).astype(BF), hsum_ref[...]), axis=0, keepdims=True)
    cmax = jnp.max(cum, axis=0, keepdims=True)
    cmin = jnp.min(cum, axis=0, keepdims=True)
    srow = lax.broadcasted_iota(jnp.int32, (8, LANES), 0)
    kst_ref[0, 0] = jnp.where(srow == 0, cmax, jnp.where(srow == 1, cmin, jnp.where(srow == 2, kn2, 0.0)))
    qf32 = qT.astype(F32)
    qn2 = jnp.sum((qf32 * qf32).reshape(FOX_HEADS, FOX_DH, tm), axis=1)
    qst_ref[0, 0] = jnp.broadcast_to(jnp.max(qn2, axis=1, keepdims=True), (FOX_HEADS, LANES))
    ones_rows = jnp.where(lax.broadcasted_iota(jnp.int32, (BF16_ROWS, tm), 0) == 0, 1.0, 0.0).astype(BF)
    for h in range(FOX_HEADS):
        qT_ref[0, h] = qT[h * FOX_DH:(h + 1) * FOX_DH]
        vT_ref[0, h, 0, 0:FOX_DH, :] = vT[h * FOX_DH:(h + 1) * FOX_DH]
        vT_ref[0, h, 0, FOX_DH:, :] = ones_rows

    cT = cum.T
    for h in range(FOX_HEADS):
        cumT_ref[0, h] = cT[h:h + 1]
        rb_ref[0, h] = jnp.broadcast_to(cT[h:h + 1, 0:1], (1, LANES))


def _head_sum_matrix():
    g = np.zeros((D_MAIN, LANES), np.float32)
    for h in range(FOX_HEADS):
        g[h * FOX_DH:(h + 1) * FOX_DH, h] = 1.0
    return g


def _shared_kv(x1, wf3, bias3, wk, wqT, wvT, tm):
    B, S, _ = x1.shape
    nt = S // tm
    hsum = jnp.asarray(_head_sum_matrix(), BF)
    return pl.pallas_call(
        _shared_kv_kernel,
        grid=(B, nt),
        in_specs=[
            pl.BlockSpec((1, tm, D_MODEL), lambda b, t: (b, t, 0)),
            _const_spec(wf3.shape),
            _const_spec(bias3.shape),
            _const_spec(wk.shape),
            _const_spec(wqT.shape),
            _const_spec(wvT.shape),
            _const_spec(hsum.shape),
        ],
        out_specs=[
            pl.BlockSpec((1, HEAD_PAIRS, 1, tm, LANES), lambda b, t: (b, 0, t, 0, 0)),
            pl.BlockSpec((1, 1, tm, LANES), lambda b, t: (b, t, 0, 0)),
            pl.BlockSpec((1, FOX_HEADS, FOX_DH, tm), lambda b, t: (b, 0, 0, t)),
            pl.BlockSpec((1, FOX_HEADS, 1, VROWS, tm), lambda b, t: (b, 0, t, 0, 0)),
            pl.BlockSpec((1, FOX_HEADS, 1, tm), lambda b, t: (b, 0, 0, t)),
            pl.BlockSpec((1, FOX_HEADS, 1, LANES), lambda b, t: (b, t, 0, 0)),
            pl.BlockSpec((1, 1, 8, LANES), lambda b, t: (b, t, 0, 0)),
            pl.BlockSpec((1, 1, FOX_HEADS, LANES), lambda b, t: (b, t, 0, 0)),
        ],
        out_shape=[
            jax.ShapeDtypeStruct((B, HEAD_PAIRS, nt, tm, LANES), BF),
            jax.ShapeDtypeStruct((B, nt, tm, LANES), BF),
            jax.ShapeDtypeStruct((B, FOX_HEADS, FOX_DH, S), BF),
            jax.ShapeDtypeStruct((B, FOX_HEADS, nt, VROWS, tm), BF),
            jax.ShapeDtypeStruct((B, FOX_HEADS, 1, S), F32),
            jax.ShapeDtypeStruct((B, nt * FOX_HEADS, 1, LANES), F32),
            jax.ShapeDtypeStruct((B, nt, 8, LANES), F32),
            jax.ShapeDtypeStruct((B, nt, FOX_HEADS, LANES), F32),
        ],
        scratch_shapes=[pltpu.VMEM((1, LANES), F32)],
        compiler_params=pltpu.CompilerParams(
            dimension_semantics=("arbitrary", "arbitrary"), vmem_limit_bytes=VMEM_LIMIT),
        name="shared_kv",
    )(x1, wf3, bias3, wk, wqT, wvT, hsum)


def _attn_kernel(n1_ref, ul_ref, qT_ref, cumT_ref, rb_ref, k_ref, dk_ref, vT_ref, y_ref,
                 qa_ref, o_ref, m_ref, s_ref, p_ref):
    b = pl.program_id(0)
    i = pl.program_id(1)
    n1 = n1_ref[b * pl.num_programs(1) + i]
    tq = qa_ref.shape[2]
    tk = k_ref.shape[3]

    rowi = lax.broadcasted_iota(jnp.int32, (LANES, tq), 0)
    zeros = jnp.zeros((FOX_DH, tq), BF)
    for h in range(FOX_HEADS):
        lo = (h % 2) * FOX_DH
        qa_ref[h, lo:lo + FOX_DH, :] = qT_ref[0, h]
        qa_ref[h, FOX_DH - lo:2 * FOX_DH - lo, :] = zeros
        piece_rows = (rowi == h) | (rowi == FOX_HEADS + h) | (rowi == 2 * FOX_HEADS + h)
        qa_ref[h, LANES:, :] = jnp.where(piece_rows, 1.0, 0.0).astype(BF)
    m_ref[...] = jnp.full(m_ref.shape, NEG, F32)
    o_ref[...] = jnp.zeros_like(o_ref)

    def logits(h, j, m_prev, slot, masked):
        pair_idx = lax.shift_right_logical(jnp.int32(h), 1)
        lhs = jnp.concatenate([k_ref[0, pair_idx, j], dk_ref[0, j]], axis=1)
        s = _mm(lhs, qa_ref[h])
        if masked:
            r = lax.broadcasted_iota(jnp.int32, (tk, tq), 0)
            c = lax.broadcasted_iota(jnp.int32, (tk, tq), 1)
            s = jnp.where(r <= c, s, NEG)
        a = cumT_ref[0, h] - jnp.tile(rb_ref[0, j * FOX_HEADS + h], (1, tq // LANES))
        m_cur = jnp.maximum(m_prev, jnp.max(s, axis=0, keepdims=True) + a)
        s_ref[slot] = s
        return m_cur, m_cur - a, jnp.exp2(m_prev - m_cur)

    def probs(slot, shift):
        p_ref[slot] = jnp.exp2(s_ref[slot] - shift).astype(BF)

    def values(h, j, slot, al):
        o_ref[h] = al * o_ref[h] + _mm(vT_ref[0, h, j], p_ref[slot])

    def sweep(unit, n, masked):
        (h0, j0), (h1, j1), (h2, j2), (h3, j3) = unit(0), unit(1), unit(2), unit(3)
        m, sh_e, a_e = logits(h0, j0, m_ref[h0], 0, masked)
        m_ref[h0] = m
        m, sh_o, a_o = logits(h1, j1, m_ref[h1], 1, masked)
        m_ref[h1] = m
        probs(0, sh_e)
        a_c = a_e
        m, sh_e, a_e = logits(h2, j2, m_ref[h2], 0, masked)
        m_ref[h2] = m
        probs(1, sh_o)
        values(h0, j0, 0, a_c)
        a_c = a_o
        m, sh_o, a_o = logits(h3, j3, m_ref[h3], 1, masked)
        m_ref[h3] = m

        def pair(g, carry):
            sh_e, a_e, sh_o, a_o, a_c = carry
            t = 2 * g
            (hp, jp), (he, je) = unit(t - 1), unit(t)
            (he2, je2), (ho2, jo2) = unit(t + 2), unit(t + 3)
            mp_e, mp_o = m_ref[he2], m_ref[ho2]
            probs(0, sh_e)
            m_e2, sh_e2, a_e2 = logits(he2, je2, mp_e, 0, masked)
            values(hp, jp, 1, a_c)
            probs(1, sh_o)
            mp_o = jnp.where(ho2 == he2, m_e2, mp_o)
            m_o2, sh_o2, a_o2 = logits(ho2, jo2, mp_o, 1, masked)
            values(he, je, 0, a_e)
            m_ref[he2] = m_e2
            m_ref[ho2] = m_o2
            return sh_e2, a_e2, sh_o2, a_o2, a_o

        sh_e, a_e, sh_o, a_o, a_c = lax.fori_loop(1, n // 2 - 1, pair, (sh_e, a_e, sh_o, a_o, a_c))
        (hp, jp), (he, je), (ho, jo) = unit(n - 3), unit(n - 2), unit(n - 1)
        probs(0, sh_e)
        values(hp, jp, 1, a_c)
        probs(1, sh_o)
        values(he, je, 0, a_e)
        values(ho, jo, 1, a_o)

    def listed_unit(t):
        code = ul_ref[0, 0, t]
        return code & (FOX_HEADS - 1), lax.shift_right_logical(code, 4)

    @pl.when(n1 > 0)
    def _():
        sweep(listed_unit, n1, False)

    sweep(lambda t: (t, i), FOX_HEADS, True)
    for h in range(FOX_HEADS):
        y_ref[0, h] = (o_ref[h, 0:FOX_DH, :] / o_ref[h, FOX_DH:FOX_DH + 1, :]).astype(BF)


def _attn_schedule(kst, qst):
    B, nt = kst.shape[0], kst.shape[1]
    cmax = kst[:, :, 0, :FOX_HEADS]
    cmin = kst[:, :, 1, :FOX_HEADS]
    kn = jnp.sqrt(kst[:, :, 2, :FOX_HEADS]) * NORM_SLACK
    qn = jnp.sqrt(qst[:, :, :, 0]) * NORM_SLACK
    upper = qn[:, :, None] * kn[:, None] + cmax[:, :, None] - cmin[:, None]
    lower = -qn * kn
    need = upper > lower[:, :, None] - UNDERFLOW_MARGIN
    ti = jnp.arange(nt)
    valid = (ti[None, :] < ti[:, None])[None, :, :, None]
    rank = jnp.where(valid & need, 0, jnp.where(valid, 1, 2)).reshape(B, nt, nt * FOX_HEADS)
    units = jnp.argsort(rank, axis=-1, stable=True).astype(jnp.int32)
    cnt = jnp.sum(rank == 0, axis=-1).astype(jnp.int32)
    n1 = jnp.where(cnt > 0, jnp.maximum(cnt + (cnt & 1), 4), 0)
    return n1.reshape(-1), units.reshape(B * nt, 1, nt * FOX_HEADS)


def _attention(qT, cumT, rb, kst, qst, k, dk, vT, tq):
    B, H, dh, S = qT.shape
    nt = S // tq
    n1, units = _attn_schedule(kst, qst)
    resident = dict(pipeline_mode=pl.Buffered(1))
    grid_spec = pltpu.PrefetchScalarGridSpec(
        num_scalar_prefetch=1,
        grid=(B, nt),
        in_specs=[
            pl.BlockSpec((1, 1, nt * H), lambda b, i, n1: (b * nt + i, 0, 0), memory_space=pltpu.SMEM),
            pl.BlockSpec((1, H, dh, tq), lambda b, i, n1: (b, 0, 0, i)),
            pl.BlockSpec((1, H, 1, tq), lambda b, i, n1: (b, 0, 0, i)),
            pl.BlockSpec((1, nt * H, 1, LANES), lambda b, i, n1: (b, 0, 0, 0)),
            pl.BlockSpec((1, HEAD_PAIRS, nt, tq, LANES), lambda b, i, n1: (b, 0, 0, 0, 0), **resident),
            pl.BlockSpec((1, nt, tq, LANES), lambda b, i, n1: (b, 0, 0, 0), **resident),
            pl.BlockSpec((1, H, nt, VROWS, tq), lambda b, i, n1: (b, 0, 0, 0, 0), **resident),
        ],
        out_specs=pl.BlockSpec((1, H, dh, tq), lambda b, i, n1: (b, 0, 0, i)),
        scratch_shapes=[
            pltpu.VMEM((H, QA_ROWS, tq), BF),
            pltpu.VMEM((H, VROWS, tq), F32),
            pltpu.VMEM((H, 1, tq), F32),
            pltpu.VMEM((2, tq, tq), F32),
            pltpu.VMEM((2, tq, tq), BF),
        ],
    )
    return pl.pallas_call(
        _attn_kernel,
        grid_spec=grid_spec,
        out_shape=jax.ShapeDtypeStruct((B, H, dh, S), BF),
        compiler_params=pltpu.CompilerParams(
            dimension_semantics=("arbitrary", "arbitrary"), vmem_limit_bytes=VMEM_LIMIT),
        name="fox_attn",
    )(n1, units, qT, cumT, rb, k, dk, vT)


def _layer1_kernel(x_ref, yT_ref, w_inT_ref, mk_ref, mvT_ref, w_outT_ref, lng_ref, lnb_ref,
                   o_ref, yg_ref):
    tc = x_ref.shape[1] // TOKEN_SPLIT
    for c in range(TOKEN_SPLIT):
        cols = slice(c * tc, (c + 1) * tc)
        x = x_ref[0, cols, :]
        xb = x.astype(BF)
        qmT = _nt(w_inT_ref[0:D_MEM, :], xb)
        gT = _nt(w_inT_ref[D_MEM:, :], xb)
        _finish_layer(x, yT_ref[0, :, cols].astype(F32), qmT, gT, mk_ref, mvT_ref, w_outT_ref,
                      lng_ref, lnb_ref, o_ref, yg_ref, cols)


def _layer1(x1, yT, w_inT, mk, mvT, w_outT, lng, lnb, tm):
    B, S, _ = x1.shape
    return pl.pallas_call(
        _layer1_kernel,
        grid=(B, S // tm),
        in_specs=[
            pl.BlockSpec((1, tm, D_MODEL), lambda b, t: (b, t, 0)),
            pl.BlockSpec((1, D_MAIN, tm), lambda b, t: (b, 0, t)),
            _const_spec(w_inT.shape),
            pl.BlockSpec((1, 1, MEM_HEADS, N_MEM, MEM_DH), lambda b, t: (1, b, 0, 0, 0)),
            pl.BlockSpec((1, 1, MEM_HEADS, MEM_DH, N_MEM), lambda b, t: (1, b, 0, 0, 0)),
            _const_spec(w_outT.shape),
            _const_spec(lng.shape),
            _const_spec(lnb.shape),
        ],
        out_specs=pl.BlockSpec((1, tm, D_MODEL), lambda b, t: (b, t, 0)),
        out_shape=jax.ShapeDtypeStruct((B, S, D_MODEL), F32),
        scratch_shapes=[pltpu.VMEM((D_MIX, tm), BF)],
        compiler_params=pltpu.CompilerParams(
            dimension_semantics=("arbitrary", "arbitrary"), vmem_limit_bytes=VMEM_LIMIT),
        name="layer1",
    )(x1, yT, w_inT, mk, mvT, w_outT, lng, lnb)


def kernel(x, mem, w_in, w_mem_kv, w_out, ln_g, ln_b, pool_w, pool_scale, w_kv_shared, b_forget):
    B, S, _ = x.shape
    tm = min(512, S)

    w_inT = jnp.swapaxes(w_in, 1, 2).astype(BF)
    w_outT = jnp.swapaxes(w_out, 1, 2).astype(BF)
    pool_wT = jnp.swapaxes(pool_w[0], 1, 2).astype(BF)
    ps = pool_scale[0].reshape(D_MAIN, 1)
    lng = ln_g.reshape(DEPTH, 1, D_MODEL)
    lnb = ln_b.reshape(DEPTH, 1, D_MODEL)
    wk = w_kv_shared[:, :D_MAIN].astype(BF)
    wvT = w_kv_shared[:, D_MAIN:2 * D_MAIN].T.astype(BF)
    wf = w_kv_shared[:, 2 * D_MAIN:]
    wf3 = jnp.concatenate([wf, wf, wf, jnp.zeros((D_MODEL, LANES - 3 * FOX_HEADS), F32)], axis=1).astype(BF)
    bias3 = jnp.concatenate([b_forget, b_forget, b_forget,
                             jnp.zeros((LANES - 3 * FOX_HEADS,), F32)]).reshape(1, LANES)

    mk, mvT = _mem_kv(mem, w_mem_kv.astype(BF))
    tl = 2 * tm if S % (2 * tm) == 0 else tm
    x1 = _layer0(x, w_inT[0], pool_wT, ps, mk, mvT, w_outT[0], lng[0], lnb[0], tl)
    k, dk, qT, vT, cumT, rb, kst, qst = _shared_kv(x1, wf3, bias3, wk, w_inT[1, :D_MAIN], wvT, tm)
    yT = _attention(qT, cumT, rb, kst, qst, k, dk, vT, tm)
    return _layer1(x1, yT.reshape(B, D_MAIN, S), w_inT[1, D_MAIN:], mk, mvT, w_outT[1],
                   lng[1], lnb[1], tl)
```

```python
import numpy as np
import jax
import jax.numpy as jnp
from jax import lax
from jax.experimental import pallas as pl
from jax.experimental.pallas import tpu as pltpu

D_MODEL = 1024
N_MEM = 256
D_MAIN = 1024
POOL_WINDOWS = (2, 4, 8, 16)
POOL_GROUP = 256
FOX_HEADS = 16
FOX_DH = 64
MEM_HEADS = 4
MEM_DH = 128
D_MEM = MEM_HEADS * MEM_DH
D_MIX = D_MAIN + D_MEM
DEPTH = 2
ALPHA = (2 * DEPTH) ** 0.25
LN_EPS = 1e-5

LANES = 128
BF16_ROWS = 16
HEAD_PAIRS = FOX_HEADS // 2
QA_ROWS = 2 * LANES
VROWS = FOX_DH + BF16_ROWS
NEG = -1e30
LOG2E = 1.4426950408889634
UNDERFLOW_MARGIN = 160.0
NORM_SLACK = 1.02
REF_HEADROOM = 40.0
REF_SPREAD_LIMIT = 100.0
TOKEN_SPLIT = 2
VMEM_LIMIT = 56 * 1024 * 1024

BF = jnp.bfloat16
F32 = jnp.float32


def _mm(a, b):
    return jnp.dot(a, b, preferred_element_type=F32)


def _nt(a, b):
    return lax.dot_general(a, b, (((1,), (1,)), ((), ())), preferred_element_type=F32)


def _split3(v):
    hi = v.astype(BF)
    r1 = v - hi.astype(F32)
    mid = r1.astype(BF)
    lo = (r1 - mid.astype(F32)).astype(BF)
    return hi, mid, lo


def _silu(g):
    hg = 0.5 * g
    return hg + hg * jnp.tanh(hg)


def _const_spec(shape):
    nd = len(shape)
    return pl.BlockSpec(shape, lambda *_: (0,) * nd)


def _mem_kv_kernel(mem_ref, w_ref, mk_ref, mvT_ref):
    memb = mem_ref[0].astype(BF)
    mkv = _mm(memb, w_ref[0])
    for h in range(MEM_HEADS):
        mk_ref[0, 0, h] = mkv[:, h * MEM_DH:(h + 1) * MEM_DH].astype(BF)
        mv = mkv[:, D_MEM + h * MEM_DH:D_MEM + (h + 1) * MEM_DH]
        mvT_ref[0, 0, h] = mv.T.astype(BF)


def _mem_kv(mem, w_mem_kv_bf):
    B = mem.shape[0]
    L = w_mem_kv_bf.shape[0]
    return pl.pallas_call(
        _mem_kv_kernel,
        grid=(L, B),
        in_specs=[
            pl.BlockSpec((1, N_MEM, D_MODEL), lambda l, b: (b, 0, 0)),
            pl.BlockSpec((1, D_MODEL, 2 * D_MEM), lambda l, b: (l, 0, 0)),
        ],
        out_specs=[
            pl.BlockSpec((1, 1, MEM_HEADS, N_MEM, MEM_DH), lambda l, b: (l, b, 0, 0, 0)),
            pl.BlockSpec((1, 1, MEM_HEADS, MEM_DH, N_MEM), lambda l, b: (l, b, 0, 0, 0)),
        ],
        out_shape=[
            jax.ShapeDtypeStruct((L, B, MEM_HEADS, N_MEM, MEM_DH), BF),
            jax.ShapeDtypeStruct((L, B, MEM_HEADS, MEM_DH, N_MEM), BF),
        ],
        compiler_params=pltpu.CompilerParams(
            dimension_semantics=("arbitrary", "arbitrary"), vmem_limit_bytes=VMEM_LIMIT),
        name="mem_kv",
    )(mem, w_mem_kv_bf)


def _finish_layer(x, mainT, qmT, gT, mk_ref, mvT_ref, w_outT_ref, lng_ref, lnb_ref, o_ref, yg_ref, cols):
    yg_ref = yg_ref.at[:, cols]
    yg_ref[0:D_MAIN, :] = (mainT * _silu(gT[0:D_MAIN])).astype(BF)
    for h in range(MEM_HEADS):
        rows = slice(h * MEM_DH, (h + 1) * MEM_DH)
        qh = qmT[rows].astype(BF)
        lg = _mm(mk_ref[0, 0, h], qh) * (MEM_DH ** -0.5 * LOG2E)
        mx = jnp.max(lg, axis=0, keepdims=True)
        e = jnp.exp2(lg - mx)
        den = jnp.sum(e, axis=0, keepdims=True)
        ym = _mm(mvT_ref[0, 0, h], e.astype(BF)) * (1.0 / den)
        gm = gT[D_MAIN + h * MEM_DH:D_MAIN + (h + 1) * MEM_DH]
        yg_ref[D_MAIN + h * MEM_DH:D_MAIN + (h + 1) * MEM_DH, :] = (ym * _silu(gm)).astype(BF)
    outT = _mm(w_outT_ref[...], yg_ref[...])
    z = ALPHA * x + outT.T
    mu = jnp.mean(z, axis=-1, keepdims=True)
    zc = z - mu
    var = jnp.mean(zc * zc, axis=-1, keepdims=True)
    o_ref[0, cols, :] = zc * lax.rsqrt(var + LN_EPS) * lng_ref[...] + lnb_ref[...]


def _layer0_kernel(x_ref, w_inT_ref, band_ref, pool_wT_ref, ps_ref, mk_ref, mvT_ref, w_outT_ref,
                   lng_ref, lnb_ref, o_ref, tail_ref, main_ref, yg_ref):
    t = pl.program_id(1)
    x = x_ref[0]
    tm = x.shape[0]
    xb = x.astype(BF)
    uT = _nt(w_inT_ref[0:D_MAIN, :], xb)
    qmT = _nt(w_inT_ref[D_MAIN:D_MIX, :], xb)
    gT = _nt(w_inT_ref[D_MIX:, :], xb)

    @pl.when(t == 0)
    def _():
        tail_ref[...] = jnp.zeros_like(tail_ref)

    tpos = t * tm + lax.broadcasted_iota(jnp.int32, (1, tm), 1)
    ub = uT.astype(BF)
    for g, w in enumerate(POOL_WINDOWS):
        rows = slice(g * POOL_GROUP, (g + 1) * POOL_GROUP)
        ext = jnp.concatenate([tail_ref[rows, :], ub[rows]], axis=1)
        wsum = jnp.concatenate(
            [_mm(ext[:, n * LANES:(n + 2) * LANES], band_ref[g]) for n in range(tm // LANES)], axis=1)
        inv_cnt = 1.0 / jnp.minimum(tpos + 1, w).astype(F32)
        pm = wsum * inv_cnt - uT[rows]
        main_ref[rows, :] = _mm(pool_wT_ref[g], pm.astype(BF)) * ps_ref[rows, :]
    tail_ref[...] = ub[:, tm - LANES:]

    _finish_layer(x, main_ref[...], qmT, gT, mk_ref, mvT_ref, w_outT_ref, lng_ref, lnb_ref,
                  o_ref, yg_ref, slice(0, tm))


def _pool_bands():
    s = np.arange(2 * LANES)[:, None]
    c = np.arange(LANES)[None, :] + LANES
    return np.stack([((s > c - w) & (s <= c)).astype(np.float32) for w in POOL_WINDOWS])


def _layer0(x, w_inT, pool_wT, ps, mk, mvT, w_outT, lng, lnb, tm):
    B, S, _ = x.shape
    band = jnp.asarray(_pool_bands(), BF)
    return pl.pallas_call(
        _layer0_kernel,
        grid=(B, S // tm),
        in_specs=[
            pl.BlockSpec((1, tm, D_MODEL), lambda b, t: (b, t, 0)),
            _const_spec(w_inT.shape),
            _const_spec(band.shape),
            _const_spec(pool_wT.shape),
            _const_spec(ps.shape),
            pl.BlockSpec((1, 1, MEM_HEADS, N_MEM, MEM_DH), lambda b, t: (0, b, 0, 0, 0)),
            pl.BlockSpec((1, 1, MEM_HEADS, MEM_DH, N_MEM), lambda b, t: (0, b, 0, 0, 0)),
            _const_spec(w_outT.shape),
            _const_spec(lng.shape),
            _const_spec(lnb.shape),
        ],
        out_specs=pl.BlockSpec((1, tm, D_MODEL), lambda b, t: (b, t, 0)),
        out_shape=jax.ShapeDtypeStruct((B, S, D_MODEL), F32),
        scratch_shapes=[
            pltpu.VMEM((D_MAIN, LANES), BF),
            pltpu.VMEM((D_MAIN, tm), F32),
            pltpu.VMEM((D_MIX, tm), BF),
        ],
        compiler_params=pltpu.CompilerParams(
            dimension_semantics=("arbitrary", "arbitrary"), vmem_limit_bytes=VMEM_LIMIT),
        name="layer0",
    )(x, w_inT, band, pool_wT, ps, mk, mvT, w_outT, lng, lnb)


def _shared_kv_kernel(x_ref, wf3_ref, bias3_ref, wk_ref, wqT_ref, wvT_ref, hsum_ref,
                      k_ref, dk_ref, qT_ref, vT_ref, cumT_ref, rb_ref, kst_ref, qst_ref, carry_ref):
    t = pl.program_id(1)

    @pl.when(t == 0)
    def _():
        carry_ref[...] = jnp.zeros_like(carry_ref)

    xb = x_ref[0].astype(BF)
    tm = xb.shape[0]
    lane = lax.broadcasted_iota(jnp.int32, (tm, LANES), 1)
    f3 = _mm(xb, wf3_ref[...]) + bias3_ref[...]
    lf = jnp.where(lane < 3 * FOX_HEADS, jax.nn.log_sigmoid(f3), 0.0)

    cb = min(256, tm)
    ri = lax.broadcasted_iota(jnp.int32, (cb, cb), 0)
    ci = lax.broadcasted_iota(jnp.int32, (cb, cb), 1)
    tri = jnp.where(ci <= ri, 1.0, 0.0).astype(BF)
    off = jnp.zeros((1, LANES), F32)
    blocks = []
    for i in range(tm // cb):
        hi, mid, lo = _split3(lf[i * cb:(i + 1) * cb])
        c = _mm(tri, hi) + _mm(tri, mid) + _mm(tri, lo) + off
        off = c[cb - 1:cb, :]
        blocks.append(c)
    c_loc = jnp.concatenate(blocks, axis=0) * LOG2E
    cum = c_loc + carry_ref[...]
    carry_ref[...] = cum[tm - 1:tm, :]

    hi, mid, lo = (v.astype(F32) for v in _split3(c_loc[0:1, :] - c_loc))
    dk_ref[0, 0] = jnp.where(lane < FOX_HEADS, hi,
                             jnp.where(lane < 2 * FOX_HEADS, mid,
                                       jnp.where(lane < 3 * FOX_HEADS, lo, 0.0))).astype(BF)
    kb = _mm(xb, wk_ref[...]).astype(BF)
    for g in range(HEAD_PAIRS):
        k_ref[0, g, 0] = kb[:, g * LANES:(g + 1) * LANES]

    qT = (_nt(wqT_ref[...], xb) * (FOX_DH ** -0.5 * LOG2E)).astype(BF)
    vT = _nt(wvT_ref[...], xb).astype(BF)

    kf32 = kb.astype(F32)
    kn2 = jnp.max(_mm((kf32 * kf32).astype(BF), hsum_ref[...]), axis=0, keepdims=True)
    cmax = jnp.max(cum, axis=0, keepdims=True)
    cmin = jnp.min(cum, axis=0, keepdims=True)
    srow = lax.broadcasted_iota(jnp.int32, (8, LANES), 0)
    kst_ref[0, 0] = jnp.where(srow == 0, cmax, jnp.where(srow == 1, cmin, jnp.where(srow == 2, kn2, 0.0)))
    qf32 = qT.astype(F32)
    qn2 = jnp.sum((qf32 * qf32).reshape(FOX_HEADS, FOX_DH, tm), axis=1)
    qst_ref[0, 0] = jnp.broadcast_to(jnp.max(qn2, axis=1, keepdims=True), (FOX_HEADS, LANES))
    ones_rows = jnp.where(lax.broadcasted_iota(jnp.int32, (BF16_ROWS, tm), 0) == 0, 1.0, 0.0).astype(BF)
    for h in range(FOX_HEADS):
        qT_ref[0, h] = qT[h * FOX_DH:(h + 1) * FOX_DH]
        vT_ref[0, h, 0, 0:FOX_DH, :] = vT[h * FOX_DH:(h + 1) * FOX_DH]
        vT_ref[0, h, 0, FOX_DH:, :] = ones_rows

    cT = cum.T
    for h in range(FOX_HEADS):
        cumT_ref[0, h] = cT[h:h + 1]
        rb_ref[0, h] = jnp.broadcast_to(cT[h:h + 1, 0:1], (1, LANES))


def _head_sum_matrix():
    g = np.zeros((D_MAIN, LANES), np.float32)
    for h in range(FOX_HEADS):
        g[h * FOX_DH:(h + 1) * FOX_DH, h] = 1.0
    return g


def _shared_kv(x1, wf3, bias3, wk, wqT, wvT, tm):
    B, S, _ = x1.shape
    nt = S // tm
    hsum = jnp.asarray(_head_sum_matrix(), BF)
    return pl.pallas_call(
        _shared_kv_kernel,
        grid=(B, nt),
        in_specs=[
            pl.BlockSpec((1, tm, D_MODEL), lambda b, t: (b, t, 0)),
            _const_spec(wf3.shape),
            _const_spec(bias3.shape),
            _const_spec(wk.shape),
            _const_spec(wqT.shape),
            _const_spec(wvT.shape),
            _const_spec(hsum.shape),
        ],
        out_specs=[
            pl.BlockSpec((1, HEAD_PAIRS, 1, tm, LANES), lambda b, t: (b, 0, t, 0, 0)),
            pl.BlockSpec((1, 1, tm, LANES), lambda b, t: (b, t, 0, 0)),
            pl.BlockSpec((1, FOX_HEADS, FOX_DH, tm), lambda b, t: (b, 0, 0, t)),
            pl.BlockSpec((1, FOX_HEADS, 1, VROWS, tm), lambda b, t: (b, 0, t, 0, 0)),
            pl.BlockSpec((1, FOX_HEADS, 1, tm), lambda b, t: (b, 0, 0, t)),
            pl.BlockSpec((1, FOX_HEADS, 1, LANES), lambda b, t: (b, t, 0, 0)),
            pl.BlockSpec((1, 1, 8, LANES), lambda b, t: (b, t, 0, 0)),
            pl.BlockSpec((1, 1, FOX_HEADS, LANES), lambda b, t: (b, t, 0, 0)),
        ],
        out_shape=[
            jax.ShapeDtypeStruct((B, HEAD_PAIRS, nt, tm, LANES), BF),
            jax.ShapeDtypeStruct((B, nt, tm, LANES), BF),
            jax.ShapeDtypeStruct((B, FOX_HEADS, FOX_DH, S), BF),
            jax.ShapeDtypeStruct((B, FOX_HEADS, nt, VROWS, tm), BF),
            jax.ShapeDtypeStruct((B, FOX_HEADS, 1, S), F32),
            jax.ShapeDtypeStruct((B, nt * FOX_HEADS, 1, LANES), F32),
            jax.ShapeDtypeStruct((B, nt, 8, LANES), F32),
            jax.ShapeDtypeStruct((B, nt, FOX_HEADS, LANES), F32),
        ],
        scratch_shapes=[pltpu.VMEM((1, LANES), F32)],
        compiler_params=pltpu.CompilerParams(
            dimension_semantics=("arbitrary", "arbitrary"), vmem_limit_bytes=VMEM_LIMIT),
        name="shared_kv",
    )(x1, wf3, bias3, wk, wqT, wvT, hsum)


def _attn_kernel(n1_ref, fx_ref, ul_ref, qT_ref, cumT_ref, rb_ref, ref_ref, k_ref, dk_ref, vT_ref, y_ref,
                 qa_ref, o_ref, m_ref, s_ref, p_ref):
    b = pl.program_id(0)
    i = pl.program_id(1)
    step = b * pl.num_programs(1) + i
    n1 = n1_ref[step]
    fixed_ok = fx_ref[step]
    tq = qa_ref.shape[2]
    tk = k_ref.shape[3]

    rowi = lax.broadcasted_iota(jnp.int32, (LANES, tq), 0)
    zeros = jnp.zeros((FOX_DH, tq), BF)
    for h in range(FOX_HEADS):
        lo = (h % 2) * FOX_DH
        qa_ref[h, lo:lo + FOX_DH, :] = qT_ref[0, h]
        qa_ref[h, FOX_DH - lo:2 * FOX_DH - lo, :] = zeros
        piece_rows = (rowi == h) | (rowi == FOX_HEADS + h) | (rowi == 2 * FOX_HEADS + h)
        qa_ref[h, LANES:, :] = jnp.where(piece_rows, 1.0, 0.0).astype(BF)
    m_ref[...] = jnp.full(m_ref.shape, NEG, F32)
    o_ref[...] = jnp.zeros_like(o_ref)

    def logits(h, j, m_prev, slot, masked):
        pair_idx = lax.shift_right_logical(jnp.int32(h), 1)
        lhs = jnp.concatenate([k_ref[0, pair_idx, j], dk_ref[0, j]], axis=1)
        s = _mm(lhs, qa_ref[h])
        if masked:
            r = lax.broadcasted_iota(jnp.int32, (tk, tq), 0)
            c = lax.broadcasted_iota(jnp.int32, (tk, tq), 1)
            s = jnp.where(r <= c, s, NEG)
        a = cumT_ref[0, h] - jnp.tile(rb_ref[0, j * FOX_HEADS + h], (1, tq // LANES))
        m_cur = jnp.maximum(m_prev, jnp.max(s, axis=0, keepdims=True) + a)
        s_ref[slot] = s
        return m_cur, m_cur - a, jnp.exp2(m_prev - m_cur)

    def probs(slot, shift):
        p_ref[slot] = jnp.exp2(s_ref[slot] - shift).astype(BF)

    def values(h, j, slot, al):
        o_ref[h] = al * o_ref[h] + _mm(vT_ref[0, h, j], p_ref[slot])

    def sweep(unit, n, masked):
        (h0, j0), (h1, j1), (h2, j2), (h3, j3) = unit(0), unit(1), unit(2), unit(3)
        m, sh_e, a_e = logits(h0, j0, m_ref[h0], 0, masked)
        m_ref[h0] = m
        m, sh_o, a_o = logits(h1, j1, m_ref[h1], 1, masked)
        m_ref[h1] = m
        probs(0, sh_e)
        a_c = a_e
        m, sh_e, a_e = logits(h2, j2, m_ref[h2], 0, masked)
        m_ref[h2] = m
        probs(1, sh_o)
        values(h0, j0, 0, a_c)
        a_c = a_o
        m, sh_o, a_o = logits(h3, j3, m_ref[h3], 1, masked)
        m_ref[h3] = m

        def pair(g, carry):
            sh_e, a_e, sh_o, a_o, a_c = carry
            t = 2 * g
            (hp, jp), (he, je) = unit(t - 1), unit(t)
            (he2, je2), (ho2, jo2) = unit(t + 2), unit(t + 3)
            mp_e, mp_o = m_ref[he2], m_ref[ho2]
            probs(0, sh_e)
            m_e2, sh_e2, a_e2 = logits(he2, je2, mp_e, 0, masked)
            values(hp, jp, 1, a_c)
            probs(1, sh_o)
            mp_o = jnp.where(ho2 == he2, m_e2, mp_o)
            m_o2, sh_o2, a_o2 = logits(ho2, jo2, mp_o, 1, masked)
            values(he, je, 0, a_e)
            m_ref[he2] = m_e2
            m_ref[ho2] = m_o2
            return sh_e2, a_e2, sh_o2, a_o2, a_o

        sh_e, a_e, sh_o, a_o, a_c = lax.fori_loop(1, n // 2 - 1, pair, (sh_e, a_e, sh_o, a_o, a_c))
        (hp, jp), (he, je), (ho, jo) = unit(n - 3), unit(n - 2), unit(n - 1)
        probs(0, sh_e)
        values(hp, jp, 1, a_c)
        probs(1, sh_o)
        values(he, je, 0, a_e)
        values(ho, jo, 1, a_o)

    def fixed_probs(h, j, slot, masked):
        pair_idx = lax.shift_right_logical(jnp.int32(h), 1)
        lhs = jnp.concatenate([k_ref[0, pair_idx, j], dk_ref[0, j]], axis=1)
        s = _mm(lhs, qa_ref[h])
        if masked:
            r = lax.broadcasted_iota(jnp.int32, (tk, tq), 0)
            c = lax.broadcasted_iota(jnp.int32, (tk, tq), 1)
            s = jnp.where(r <= c, s, NEG)
        shift = jnp.tile(rb_ref[0, j * FOX_HEADS + h] + ref_ref[0, h], (1, tq // LANES)) - cumT_ref[0, h]
        p_ref[slot] = jnp.exp2(s - shift).astype(BF)

    def fixed_values(h, j, slot):
        o_ref[h] += _mm(vT_ref[0, h, j], p_ref[slot])

    def fixed_sweep(unit, n, masked):
        fixed_probs(*unit(0), 0, masked)
        fixed_probs(*unit(1), 1, masked)

        def pair(g, carry):
            t = 2 * g
            (he, je), (ho, jo) = unit(t), unit(t + 1)
            fixed_values(he, je, 0)
            fixed_probs(*unit(t + 2), 0, masked)
            fixed_values(ho, jo, 1)
            fixed_probs(*unit(t + 3), 1, masked)
            return carry

        lax.fori_loop(0, n // 2 - 1, pair, 0)
        fixed_values(*unit(n - 2), 0)
        fixed_values(*unit(n - 1), 1)

    def listed_unit(t):
        code = ul_ref[0, 0, t]
        return code & (FOX_HEADS - 1), lax.shift_right_logical(code, 4)

    diag_unit = lambda t: (t, i)

    @pl.when(jnp.logical_and(fixed_ok == 1, n1 > 0))
    def _():
        fixed_sweep(listed_unit, n1, False)

    @pl.when(fixed_ok == 1)
    def _():
        fixed_sweep(diag_unit, FOX_HEADS, True)

    @pl.when(jnp.logical_and(fixed_ok == 0, n1 > 0))
    def _():
        sweep(listed_unit, n1, False)

    @pl.when(fixed_ok == 0)
    def _():
        sweep(diag_unit, FOX_HEADS, True)

    for h in range(FOX_HEADS):
        y_ref[0, h] = (o_ref[h, 0:FOX_DH, :] / o_ref[h, FOX_DH:FOX_DH + 1, :]).astype(BF)


def _attn_schedule(kst, qst):
    B, nt = kst.shape[0], kst.shape[1]
    cmax = kst[:, :, 0, :FOX_HEADS]
    cmin = kst[:, :, 1, :FOX_HEADS]
    kn = jnp.sqrt(kst[:, :, 2, :FOX_HEADS]) * NORM_SLACK
    qn = jnp.sqrt(qst[:, :, :, 0]) * NORM_SLACK
    upper = qn[:, :, None] * kn[:, None] + cmax[:, :, None] - cmin[:, None]
    lower = -qn * kn
    need = upper > lower[:, :, None] - UNDERFLOW_MARGIN
    ti = jnp.arange(nt)
    valid = (ti[None, :] < ti[:, None])[None, :, :, None]
    rank = jnp.where(valid & need, 0, jnp.where(valid, 1, 2)).reshape(B, nt, nt * FOX_HEADS)
    units = jnp.argsort(rank, axis=-1, stable=True).astype(jnp.int32)
    cnt = jnp.sum(rank == 0, axis=-1).astype(jnp.int32)
    n1 = jnp.where(cnt > 0, jnp.maximum(cnt + (cnt & 1), 4), 0)

    top = qn * lax.cummax(kn, axis=1)
    ref = 0.5 * (top + lower) - REF_HEADROOM
    fixed_ok = jnp.all(top - lower <= REF_SPREAD_LIMIT, axis=-1).astype(jnp.int32)
    ref = jnp.broadcast_to(ref.reshape(B * nt, FOX_HEADS, 1, 1), (B * nt, FOX_HEADS, 1, LANES))
    return n1.reshape(-1), fixed_ok.reshape(-1), units.reshape(B * nt, 1, nt * FOX_HEADS), ref


def _attention(qT, cumT, rb, kst, qst, k, dk, vT, tq):
    B, H, dh, S = qT.shape
    nt = S // tq
    n1, fixed_ok, units, ref = _attn_schedule(kst, qst)
    resident = dict(pipeline_mode=pl.Buffered(1))
    grid_spec = pltpu.PrefetchScalarGridSpec(
        num_scalar_prefetch=2,
        grid=(B, nt),
        in_specs=[
            pl.BlockSpec((1, 1, nt * H), lambda b, i, *_: (b * nt + i, 0, 0), memory_space=pltpu.SMEM),
            pl.BlockSpec((1, H, dh, tq), lambda b, i, *_: (b, 0, 0, i)),
            pl.BlockSpec((1, H, 1, tq), lambda b, i, *_: (b, 0, 0, i)),
            pl.BlockSpec((1, nt * H, 1, LANES), lambda b, i, *_: (b, 0, 0, 0)),
            pl.BlockSpec((1, H, 1, LANES), lambda b, i, *_: (b * nt + i, 0, 0, 0)),
            pl.BlockSpec((1, HEAD_PAIRS, nt, tq, LANES), lambda b, i, *_: (b, 0, 0, 0, 0), **resident),
            pl.BlockSpec((1, nt, tq, LANES), lambda b, i, *_: (b, 0, 0, 0), **resident),
            pl.BlockSpec((1, H, nt, VROWS, tq), lambda b, i, *_: (b, 0, 0, 0, 0), **resident),
        ],
        out_specs=pl.BlockSpec((1, H, dh, tq), lambda b, i, *_: (b, 0, 0, i)),
        scratch_shapes=[
            pltpu.VMEM((H, QA_ROWS, tq), BF),
            pltpu.VMEM((H, VROWS, tq), F32),
            pltpu.VMEM((H, 1, tq), F32),
            pltpu.VMEM((2, tq, tq), F32),
            pltpu.VMEM((2, tq, tq), BF),
        ],
    )
    return pl.pallas_call(
        _attn_kernel,
        grid_spec=grid_spec,
        out_shape=jax.ShapeDtypeStruct((B, H, dh, S), BF),
        compiler_params=pltpu.CompilerParams(
            dimension_semantics=("arbitrary", "arbitrary"), vmem_limit_bytes=VMEM_LIMIT),
        name="fox_attn",
    )(n1, fixed_ok, units, qT, cumT, rb, ref, k, dk, vT)


def _layer1_kernel(x_ref, yT_ref, w_inT_ref, mk_ref, mvT_ref, w_outT_ref, lng_ref, lnb_ref,
                   o_ref, yg_ref):
    tc = x_ref.shape[1] // TOKEN_SPLIT
    for c in range(TOKEN_SPLIT):
        cols = slice(c * tc, (c + 1) * tc)
        x = x_ref[0, cols, :]
        xb = x.astype(BF)
        qmT = _nt(w_inT_ref[0:D_MEM, :], xb)
        gT = _nt(w_inT_ref[D_MEM:, :], xb)
        _finish_layer(x, yT_ref[0, :, cols].astype(F32), qmT, gT, mk_ref, mvT_ref, w_outT_ref,
                      lng_ref, lnb_ref, o_ref, yg_ref, cols)


def _layer1(x1, yT, w_inT, mk, mvT, w_outT, lng, lnb, tm):
    B, S, _ = x1.shape
    return pl.pallas_call(
        _layer1_kernel,
        grid=(B, S // tm),
        in_specs=[
            pl.BlockSpec((1, tm, D_MODEL), lambda b, t: (b, t, 0)),
            pl.BlockSpec((1, D_MAIN, tm), lambda b, t: (b, 0, t)),
            _const_spec(w_inT.shape),
            pl.BlockSpec((1, 1, MEM_HEADS, N_MEM, MEM_DH), lambda b, t: (1, b, 0, 0, 0)),
            pl.BlockSpec((1, 1, MEM_HEADS, MEM_DH, N_MEM), lambda b, t: (1, b, 0, 0, 0)),
            _const_spec(w_outT.shape),
            _const_spec(lng.shape),
            _const_spec(lnb.shape),
        ],
        out_specs=pl.BlockSpec((1, tm, D_MODEL), lambda b, t: (b, t, 0)),
        out_shape=jax.ShapeDtypeStruct((B, S, D_MODEL), F32),
        scratch_shapes=[pltpu.VMEM((D_MIX, tm), BF)],
        compiler_params=pltpu.CompilerParams(
            dimension_semantics=("arbitrary", "arbitrary"), vmem_limit_bytes=VMEM_LIMIT),
        name="layer1",
    )(x1, yT, w_inT, mk, mvT, w_outT, lng, lnb)


def kernel(x, mem, w_in, w_mem_kv, w_out, ln_g, ln_b, pool_w, pool_scale, w_kv_shared, b_forget):
    B, S, _ = x.shape
    tm = min(512, S)

    w_inT = jnp.swapaxes(w_in, 1, 2).astype(BF)
    w_outT = jnp.swapaxes(w_out, 1, 2).astype(BF)
    pool_wT = jnp.swapaxes(pool_w[0], 1, 2).astype(BF)
    ps = pool_scale[0].reshape(D_MAIN, 1)
    lng = ln_g.reshape(DEPTH, 1, D_MODEL)
    lnb = ln_b.reshape(DEPTH, 1, D_MODEL)
    wk = w_kv_shared[:, :D_MAIN].astype(BF)
    wvT = w_kv_shared[:, D_MAIN:2 * D_MAIN].T.astype(BF)
    wf = w_kv_shared[:, 2 * D_MAIN:]
    wf3 = jnp.concatenate([wf, wf, wf, jnp.zeros((D_MODEL, LANES - 3 * FOX_HEADS), F32)], axis=1).astype(BF)
    bias3 = jnp.concatenate([b_forget, b_forget, b_forget,
                             jnp.zeros((LANES - 3 * FOX_HEADS,), F32)]).reshape(1, LANES)

    mk, mvT = _mem_kv(mem, w_mem_kv.astype(BF))
    tl = 2 * tm if S % (2 * tm) == 0 else tm
    x1 = _layer0(x, w_inT[0], pool_wT, ps, mk, mvT, w_outT[0], lng[0], lnb[0], tl)
    k, dk, qT, vT, cumT, rb, kst, qst = _shared_kv(x1, wf3, bias3, wk, w_inT[1, :D_MAIN], wvT, tm)
    yT = _attention(qT, cumT, rb, kst, qst, k, dk, vT, tm)
    return _layer1(x1, yT.reshape(B, D_MAIN, S), w_inT[1, D_MAIN:], mk, mvT, w_outT[1],
                   lng[1], lnb[1], tl)
```

```python
import numpy as np
import jax
import jax.numpy as jnp
from jax import lax
from jax.experimental import pallas as pl
from jax.experimental.pallas import tpu as pltpu

D_MODEL = 1024
N_MEM = 256
D_MAIN = 1024
POOL_WINDOWS = (2, 4, 8, 16)
POOL_GROUP = 256
FOX_HEADS = 16
FOX_DH = 64
MEM_HEADS = 4
MEM_DH = 128
D_MEM = MEM_HEADS * MEM_DH
D_MIX = D_MAIN + D_MEM
DEPTH = 2
ALPHA = (2 * DEPTH) ** 0.25
LN_EPS = 1e-5

LANES = 128
BF16_ROWS = 16
HEAD_PAIRS = FOX_HEADS // 2
QA_ROWS = 2 * LANES
VROWS = FOX_DH + BF16_ROWS
NEG = -1e30
LOG2E = 1.4426950408889634
UNDERFLOW_MARGIN = 160.0
NORM_SLACK = 1.02
REF_HEADROOM = 40.0
REF_SPREAD_LIMIT = 100.0
TOKEN_SPLIT = 2
VMEM_LIMIT = 56 * 1024 * 1024

BF = jnp.bfloat16
F32 = jnp.float32


def _mm(a, b):
    return jnp.dot(a, b, preferred_element_type=F32)


def _nt(a, b):
    return lax.dot_general(a, b, (((1,), (1,)), ((), ())), preferred_element_type=F32)


def _split3(v):
    hi = v.astype(BF)
    r1 = v - hi.astype(F32)
    mid = r1.astype(BF)
    lo = (r1 - mid.astype(F32)).astype(BF)
    return hi, mid, lo


def _silu(g):
    hg = 0.5 * g
    return hg + hg * jnp.tanh(hg)


def _const_spec(shape):
    nd = len(shape)
    return pl.BlockSpec(shape, lambda *_: (0,) * nd)


def _mem_kv_kernel(mem_ref, w_ref, mk_ref, mvT_ref):
    memb = mem_ref[0].astype(BF)
    mkv = _mm(memb, w_ref[0])
    for h in range(MEM_HEADS):
        mk_ref[0, 0, h] = mkv[:, h * MEM_DH:(h + 1) * MEM_DH].astype(BF)
        mv = mkv[:, D_MEM + h * MEM_DH:D_MEM + (h + 1) * MEM_DH]
        mvT_ref[0, 0, h] = mv.T.astype(BF)


def _mem_kv(mem, w_mem_kv_bf):
    B = mem.shape[0]
    L = w_mem_kv_bf.shape[0]
    return pl.pallas_call(
        _mem_kv_kernel,
        grid=(L, B),
        in_specs=[
            pl.BlockSpec((1, N_MEM, D_MODEL), lambda l, b: (b, 0, 0)),
            pl.BlockSpec((1, D_MODEL, 2 * D_MEM), lambda l, b: (l, 0, 0)),
        ],
        out_specs=[
            pl.BlockSpec((1, 1, MEM_HEADS, N_MEM, MEM_DH), lambda l, b: (l, b, 0, 0, 0)),
            pl.BlockSpec((1, 1, MEM_HEADS, MEM_DH, N_MEM), lambda l, b: (l, b, 0, 0, 0)),
        ],
        out_shape=[
            jax.ShapeDtypeStruct((L, B, MEM_HEADS, N_MEM, MEM_DH), BF),
            jax.ShapeDtypeStruct((L, B, MEM_HEADS, MEM_DH, N_MEM), BF),
        ],
        compiler_params=pltpu.CompilerParams(
            dimension_semantics=("arbitrary", "arbitrary"), vmem_limit_bytes=VMEM_LIMIT),
        name="mem_kv",
    )(mem, w_mem_kv_bf)


def _finish_layer(x, mainT, qmT, gT, mk_ref, mvT_ref, w_outT_ref, lng_ref, lnb_ref, o_ref, yg_ref, cols):
    yg_ref = yg_ref.at[:, cols]
    yg_ref[0:D_MAIN, :] = (mainT * _silu(gT[0:D_MAIN])).astype(BF)
    for h in range(MEM_HEADS):
        rows = slice(h * MEM_DH, (h + 1) * MEM_DH)
        qh = qmT[rows].astype(BF)
        lg = _mm(mk_ref[0, 0, h], qh) * (MEM_DH ** -0.5 * LOG2E)
        mx = jnp.max(lg, axis=0, keepdims=True)
        e = jnp.exp2(lg - mx)
        den = jnp.sum(e, axis=0, keepdims=True)
        ym = _mm(mvT_ref[0, 0, h], e.astype(BF)) * (1.0 / den)
        gm = gT[D_MAIN + h * MEM_DH:D_MAIN + (h + 1) * MEM_DH]
        yg_ref[D_MAIN + h * MEM_DH:D_MAIN + (h + 1) * MEM_DH, :] = (ym * _silu(gm)).astype(BF)
    outT = _mm(w_outT_ref[...], yg_ref[...])
    z = ALPHA * x + outT.T
    mu = jnp.mean(z, axis=-1, keepdims=True)
    zc = z - mu
    var = jnp.mean(zc * zc, axis=-1, keepdims=True)
    o_ref[0, cols, :] = zc * lax.rsqrt(var + LN_EPS) * lng_ref[...] + lnb_ref[...]


def _layer0_kernel(x_ref, w_inT_ref, band_ref, pool_wT_ref, ps_ref, mk_ref, mvT_ref, w_outT_ref,
                   lng_ref, lnb_ref, o_ref, tail_ref, main_ref, yg_ref):
    t = pl.program_id(1)
    x = x_ref[0]
    tm = x.shape[0]
    xb = x.astype(BF)
    uT = _nt(w_inT_ref[0:D_MAIN, :], xb)
    qmT = _nt(w_inT_ref[D_MAIN:D_MIX, :], xb)
    gT = _nt(w_inT_ref[D_MIX:, :], xb)

    @pl.when(t == 0)
    def _():
        tail_ref[...] = jnp.zeros_like(tail_ref)

    tpos = t * tm + lax.broadcasted_iota(jnp.int32, (1, tm), 1)
    ub = uT.astype(BF)
    for g, w in enumerate(POOL_WINDOWS):
        rows = slice(g * POOL_GROUP, (g + 1) * POOL_GROUP)
        ext = jnp.concatenate([tail_ref[rows, :], ub[rows]], axis=1)
        wsum = jnp.concatenate(
            [_mm(ext[:, n * LANES:(n + 2) * LANES], band_ref[g]) for n in range(tm // LANES)], axis=1)
        inv_cnt = 1.0 / jnp.minimum(tpos + 1, w).astype(F32)
        pm = wsum * inv_cnt - uT[rows]
        main_ref[rows, :] = _mm(pool_wT_ref[g], pm.astype(BF)) * ps_ref[rows, :]
    tail_ref[...] = ub[:, tm - LANES:]

    _finish_layer(x, main_ref[...], qmT, gT, mk_ref, mvT_ref, w_outT_ref, lng_ref, lnb_ref,
                  o_ref, yg_ref, slice(0, tm))


def _pool_bands():
    s = np.arange(2 * LANES)[:, None]
    c = np.arange(LANES)[None, :] + LANES
    return np.stack([((s > c - w) & (s <= c)).astype(np.float32) for w in POOL_WINDOWS])


def _layer0(x, w_inT, pool_wT, ps, mk, mvT, w_outT, lng, lnb, tm):
    B, S, _ = x.shape
    band = jnp.asarray(_pool_bands(), BF)
    return pl.pallas_call(
        _layer0_kernel,
        grid=(B, S // tm),
        in_specs=[
            pl.BlockSpec((1, tm, D_MODEL), lambda b, t: (b, t, 0)),
            _const_spec(w_inT.shape),
            _const_spec(band.shape),
            _const_spec(pool_wT.shape),
            _const_spec(ps.shape),
            pl.BlockSpec((1, 1, MEM_HEADS, N_MEM, MEM_DH), lambda b, t: (0, b, 0, 0, 0)),
            pl.BlockSpec((1, 1, MEM_HEADS, MEM_DH, N_MEM), lambda b, t: (0, b, 0, 0, 0)),
            _const_spec(w_outT.shape),
            _const_spec(lng.shape),
            _const_spec(lnb.shape),
        ],
        out_specs=pl.BlockSpec((1, tm, D_MODEL), lambda b, t: (b, t, 0)),
        out_shape=jax.ShapeDtypeStruct((B, S, D_MODEL), F32),
        scratch_shapes=[
            pltpu.VMEM((D_MAIN, LANES), BF),
            pltpu.VMEM((D_MAIN, tm), F32),
            pltpu.VMEM((D_MIX, tm), BF),
        ],
        compiler_params=pltpu.CompilerParams(
            dimension_semantics=("arbitrary", "arbitrary"), vmem_limit_bytes=VMEM_LIMIT),
        name="layer0",
    )(x, w_inT, band, pool_wT, ps, mk, mvT, w_outT, lng, lnb)


def _shared_kv_kernel(x_ref, wf3_ref, bias3_ref, wk_ref, wqT_ref, wvT_ref, hsum_ref,
                      k_ref, dk_ref, qT_ref, vT_ref, cumT_ref, rb_ref, kst_ref, qst_ref, carry_ref):
    t = pl.program_id(1)

    @pl.when(t == 0)
    def _():
        carry_ref[...] = jnp.zeros_like(carry_ref)

    xb = x_ref[0].astype(BF)
    tm = xb.shape[0]
    lane = lax.broadcasted_iota(jnp.int32, (tm, LANES), 1)
    f3 = _mm(xb, wf3_ref[...]) + bias3_ref[...]
    lf = jnp.where(lane < 3 * FOX_HEADS, jax.nn.log_sigmoid(f3), 0.0)

    cb = min(256, tm)
    ri = lax.broadcasted_iota(jnp.int32, (cb, cb), 0)
    ci = lax.broadcasted_iota(jnp.int32, (cb, cb), 1)
    tri = jnp.where(ci <= ri, 1.0, 0.0).astype(BF)
    off = jnp.zeros((1, LANES), F32)
    blocks = []
    for i in range(tm // cb):
        hi, mid, lo = _split3(lf[i * cb:(i + 1) * cb])
        c = _mm(tri, hi) + _mm(tri, mid) + _mm(tri, lo) + off
        off = c[cb - 1:cb, :]
        blocks.append(c)
    c_loc = jnp.concatenate(blocks, axis=0) * LOG2E
    cum = c_loc + carry_ref[...]
    carry_ref[...] = cum[tm - 1:tm, :]

    hi, mid, lo = (v.astype(F32) for v in _split3(c_loc[0:1, :] - c_loc))
    dk_ref[0, 0] = jnp.where(lane < FOX_HEADS, hi,
                             jnp.where(lane < 2 * FOX_HEADS, mid,
                                       jnp.where(lane < 3 * FOX_HEADS, lo, 0.0))).astype(BF)
    kb = _mm(xb, wk_ref[...]).astype(BF)
    for g in range(HEAD_PAIRS):
        k_ref[0, g, 0] = kb[:, g * LANES:(g + 1) * LANES]

    qT = (_nt(wqT_ref[...], xb) * (FOX_DH ** -0.5 * LOG2E)).astype(BF)
    vT = _nt(wvT_ref[...], xb).astype(BF)

    kf32 = kb.astype(F32)
    kn2 = jnp.max(_mm((kf32 * kf32).astype(BF), hsum_ref[...]), axis=0, keepdims=True)
    cmax = jnp.max(cum, axis=0, keepdims=True)
    cmin = jnp.min(cum, axis=0, keepdims=True)
    srow = lax.broadcasted_iota(jnp.int32, (8, LANES), 0)
    kst_ref[0, 0] = jnp.where(srow == 0, cmax, jnp.where(srow == 1, cmin, jnp.where(srow == 2, kn2, 0.0)))
    qf32 = qT.astype(F32)
    qn2 = jnp.sum((qf32 * qf32).reshape(FOX_HEADS, FOX_DH, tm), axis=1)
    qst_ref[0, 0] = jnp.broadcast_to(jnp.max(qn2, axis=1, keepdims=True), (FOX_HEADS, LANES))
    ones_rows = jnp.where(lax.broadcasted_iota(jnp.int32, (BF16_ROWS, tm), 0) == 0, 1.0, 0.0).astype(BF)
    for h in range(FOX_HEADS):
        qT_ref[0, h] = qT[h * FOX_DH:(h + 1) * FOX_DH]
        vT_ref[0, h, 0, 0:FOX_DH, :] = vT[h * FOX_DH:(h + 1) * FOX_DH]
        vT_ref[0, h, 0, FOX_DH:, :] = ones_rows

    cT = cum.T
    for h in range(FOX_HEADS):
        cumT_ref[0, h] = cT[h:h + 1]
        rb_ref[0, h] = jnp.broadcast_to(cT[h:h + 1, 0:1], (1, LANES))


def _head_sum_matrix():
    g = np.zeros((D_MAIN, LANES), np.float32)
    for h in range(FOX_HEADS):
        g[h * FOX_DH:(h + 1) * FOX_DH, h] = 1.0
    return g


def _shared_kv(x1, wf3, bias3, wk, wqT, wvT, tm):
    B, S, _ = x1.shape
    nt = S // tm
    hsum = jnp.asarray(_head_sum_matrix(), BF)
    return pl.pallas_call(
        _shared_kv_kernel,
        grid=(B, nt),
        in_specs=[
            pl.BlockSpec((1, tm, D_MODEL), lambda b, t: (b, t, 0)),
            _const_spec(wf3.shape),
            _const_spec(bias3.shape),
            _const_spec(wk.shape),
            _const_spec(wqT.shape),
            _const_spec(wvT.shape),
            _const_spec(hsum.shape),
        ],
        out_specs=[
            pl.BlockSpec((1, HEAD_PAIRS, 1, tm, LANES), lambda b, t: (b, 0, t, 0, 0)),
            pl.BlockSpec((1, 1, tm, LANES), lambda b, t: (b, t, 0, 0)),
            pl.BlockSpec((1, FOX_HEADS, FOX_DH, tm), lambda b, t: (b, 0, 0, t)),
            pl.BlockSpec((1, FOX_HEADS, 1, VROWS, tm), lambda b, t: (b, 0, t, 0, 0)),
            pl.BlockSpec((1, FOX_HEADS, 1, tm), lambda b, t: (b, 0, 0, t)),
            pl.BlockSpec((1, FOX_HEADS, 1, LANES), lambda b, t: (b, t, 0, 0)),
            pl.BlockSpec((1, 1, 8, LANES), lambda b, t: (b, t, 0, 0)),
            pl.BlockSpec((1, 1, FOX_HEADS, LANES), lambda b, t: (b, t, 0, 0)),
        ],
        out_shape=[
            jax.ShapeDtypeStruct((B, HEAD_PAIRS, nt, tm, LANES), BF),
            jax.ShapeDtypeStruct((B, nt, tm, LANES), BF),
            jax.ShapeDtypeStruct((B, FOX_HEADS, FOX_DH, S), BF),
            jax.ShapeDtypeStruct((B, FOX_HEADS, nt, VROWS, tm), BF),
            jax.ShapeDtypeStruct((B, FOX_HEADS, 1, S), F32),
            jax.ShapeDtypeStruct((B, nt * FOX_HEADS, 1, LANES), F32),
            jax.ShapeDtypeStruct((B, nt, 8, LANES), F32),
            jax.ShapeDtypeStruct((B, nt, FOX_HEADS, LANES), F32),
        ],
        scratch_shapes=[pltpu.VMEM((1, LANES), F32)],
        compiler_params=pltpu.CompilerParams(
            dimension_semantics=("arbitrary", "arbitrary"), vmem_limit_bytes=VMEM_LIMIT),
        name="shared_kv",
    )(x1, wf3, bias3, wk, wqT, wvT, hsum)


def _attn_kernel(n1_ref, cnt_ref, fx_ref, ul_ref, qT_ref, cumT_ref, rb_ref, ref_ref, k_ref, dk_ref, vT_ref,
                 y_ref, qa_ref, o_ref, m_ref, s_ref, pa_ref, pb_ref, pc_ref):
    b = pl.program_id(0)
    i = pl.program_id(1)
    step = b * pl.num_programs(1) + i
    n1 = n1_ref[step]
    fixed_ok = fx_ref[step]
    tq = qa_ref.shape[2]
    tk = k_ref.shape[3]

    rowi = lax.broadcasted_iota(jnp.int32, (LANES, tq), 0)
    zeros = jnp.zeros((FOX_DH, tq), BF)
    for h in range(FOX_HEADS):
        lo = (h % 2) * FOX_DH
        qa_ref[h, lo:lo + FOX_DH, :] = qT_ref[0, h]
        qa_ref[h, FOX_DH - lo:2 * FOX_DH - lo, :] = zeros
        piece_rows = (rowi == h) | (rowi == FOX_HEADS + h) | (rowi == 2 * FOX_HEADS + h)
        qa_ref[h, LANES:, :] = jnp.where(piece_rows, 1.0, 0.0).astype(BF)
    m_ref[...] = jnp.full(m_ref.shape, NEG, F32)
    o_ref[...] = jnp.zeros_like(o_ref)

    def logits(h, j, m_prev, slot, masked):
        pair_idx = lax.shift_right_logical(jnp.int32(h), 1)
        lhs = jnp.concatenate([k_ref[0, pair_idx, j], dk_ref[0, j]], axis=1)
        s = _mm(lhs, qa_ref[h])
        if masked:
            r = lax.broadcasted_iota(jnp.int32, (tk, tq), 0)
            c = lax.broadcasted_iota(jnp.int32, (tk, tq), 1)
            s = jnp.where(r <= c, s, NEG)
        a = cumT_ref[0, h] - jnp.tile(rb_ref[0, j * FOX_HEADS + h], (1, tq // LANES))
        m_cur = jnp.maximum(m_prev, jnp.max(s, axis=0, keepdims=True) + a)
        s_ref[slot] = s
        return m_cur, m_cur - a, jnp.exp2(m_prev - m_cur)

    def probs(slot, shift):
        (pa_ref, pb_ref)[slot][...] = jnp.exp2(s_ref[slot] - shift).astype(BF)

    def values(h, j, slot, al):
        o_ref[h] = al * o_ref[h] + _mm(vT_ref[0, h, j], (pa_ref, pb_ref)[slot][...])

    def sweep(unit, n, masked):
        (h0, j0), (h1, j1), (h2, j2), (h3, j3) = unit(0), unit(1), unit(2), unit(3)
        m, sh_e, a_e = logits(h0, j0, m_ref[h0], 0, masked)
        m_ref[h0] = m
        m, sh_o, a_o = logits(h1, j1, m_ref[h1], 1, masked)
        m_ref[h1] = m
        probs(0, sh_e)
        a_c = a_e
        m, sh_e, a_e = logits(h2, j2, m_ref[h2], 0, masked)
        m_ref[h2] = m
        probs(1, sh_o)
        values(h0, j0, 0, a_c)
        a_c = a_o
        m, sh_o, a_o = logits(h3, j3, m_ref[h3], 1, masked)
        m_ref[h3] = m

        def pair(g, carry):
            sh_e, a_e, sh_o, a_o, a_c = carry
            t = 2 * g
            (hp, jp), (he, je) = unit(t - 1), unit(t)
            (he2, je2), (ho2, jo2) = unit(t + 2), unit(t + 3)
            mp_e, mp_o = m_ref[he2], m_ref[ho2]
            probs(0, sh_e)
            m_e2, sh_e2, a_e2 = logits(he2, je2, mp_e, 0, masked)
            values(hp, jp, 1, a_c)
            probs(1, sh_o)
            mp_o = jnp.where(ho2 == he2, m_e2, mp_o)
            m_o2, sh_o2, a_o2 = logits(ho2, jo2, mp_o, 1, masked)
            values(he, je, 0, a_e)
            m_ref[he2] = m_e2
            m_ref[ho2] = m_o2
            return sh_e2, a_e2, sh_o2, a_o2, a_o

        sh_e, a_e, sh_o, a_o, a_c = lax.fori_loop(1, n // 2 - 1, pair, (sh_e, a_e, sh_o, a_o, a_c))
        (hp, jp), (he, je), (ho, jo) = unit(n - 3), unit(n - 2), unit(n - 1)
        probs(0, sh_e)
        values(hp, jp, 1, a_c)
        probs(1, sh_o)
        values(he, je, 0, a_e)
        values(ho, jo, 1, a_o)

    def fixed_probs(t, unit, cnt, masked):
        h, j = unit(t)
        pair_idx = lax.shift_right_logical(jnp.int32(h), 1)
        lhs = jnp.concatenate([k_ref[0, pair_idx, j], dk_ref[0, j]], axis=1)
        s = _mm(lhs, qa_ref[h])
        if masked:
            r = lax.broadcasted_iota(jnp.int32, (tk, tq), 0)
            c = lax.broadcasted_iota(jnp.int32, (tk, tq), 1)
            s = jnp.where(r <= c, s, NEG)
        shift = jnp.tile(rb_ref[0, j * FOX_HEADS + h] + ref_ref[0, h], (1, tq // LANES)) - cumT_ref[0, h]
        if cnt is not None:
            shift = shift + jnp.where(t >= cnt, -NEG, 0.0)
        return jnp.exp2(s - shift).astype(BF)

    def fixed_values(t, unit, p_buf):
        h, j = unit(t)
        o_ref[h] += _mm(vT_ref[0, h, j], p_buf[...])

    def fixed_sweep(unit, n, cnt, masked):
        bufs = (pa_ref, pb_ref, pc_ref)
        bufs[0][...] = fixed_probs(0, unit, cnt, masked)
        bufs[1][...] = fixed_probs(1, unit, cnt, masked)

        def tick(t, k, last_ab):
            fixed_values(t, unit, bufs[k])
            if last_ab is None or t + 2 <= last_ab:
                bufs[(k + 2) % 3][...] = fixed_probs(t + 2, unit, cnt, masked)

        def triple(g, carry):
            for k in range(3):
                tick(3 * g + k, k, None)
            return carry

        n_loop = (n - 2) // 3
        lax.fori_loop(0, n_loop, triple, 0)
        if isinstance(n, int):
            for t in range(3 * n_loop, n):
                tick(t, t % 3, n - 1)
        else:
            fixed_values(n - 2, unit, bufs[0])
            fixed_values(n - 1, unit, bufs[1])

    def listed_unit(t):
        code = ul_ref[0, 0, t]
        return code & (FOX_HEADS - 1), lax.shift_right_logical(code, 4)

    diag_unit = lambda t: (t, i)

    @pl.when(jnp.logical_and(fixed_ok == 1, n1 > 0))
    def _():
        cnt = cnt_ref[step]
        fixed_sweep(listed_unit, cnt + (2 - cnt) % 3, cnt, False)

    @pl.when(fixed_ok == 1)
    def _():
        fixed_sweep(diag_unit, FOX_HEADS, None, True)

    @pl.when(jnp.logical_and(fixed_ok == 0, n1 > 0))
    def _():
        sweep(listed_unit, n1, False)

    @pl.when(fixed_ok == 0)
    def _():
        sweep(diag_unit, FOX_HEADS, True)

    for h in range(FOX_HEADS):
        y_ref[0, h] = (o_ref[h, 0:FOX_DH, :] / o_ref[h, FOX_DH:FOX_DH + 1, :]).astype(BF)


def _attn_schedule(kst, qst):
    B, nt = kst.shape[0], kst.shape[1]
    cmax = kst[:, :, 0, :FOX_HEADS]
    cmin = kst[:, :, 1, :FOX_HEADS]
    kn = jnp.sqrt(kst[:, :, 2, :FOX_HEADS]) * NORM_SLACK
    qn = jnp.sqrt(qst[:, :, :, 0]) * NORM_SLACK
    upper = qn[:, :, None] * kn[:, None] + cmax[:, :, None] - cmin[:, None]
    lower = -qn * kn
    need = upper > lower[:, :, None] - UNDERFLOW_MARGIN
    ti = jnp.arange(nt)
    valid = (ti[None, :] < ti[:, None])[None, :, :, None]
    rank = jnp.where(valid & need, 0, jnp.where(valid, 1, 2)).reshape(B, nt, nt * FOX_HEADS)
    units = jnp.argsort(rank, axis=-1, stable=True).astype(jnp.int32)
    cnt = jnp.sum(rank == 0, axis=-1).astype(jnp.int32)
    n1 = jnp.where(cnt > 0, jnp.maximum(cnt + (cnt & 1), 4), 0)

    top = qn * lax.cummax(kn, axis=1)
    ref = 0.5 * (top + lower) - REF_HEADROOM
    fixed_ok = jnp.all(top - lower <= REF_SPREAD_LIMIT, axis=-1).astype(jnp.int32)
    ref = jnp.broadcast_to(ref.reshape(B * nt, FOX_HEADS, 1, 1), (B * nt, FOX_HEADS, 1, LANES))
    return (n1.reshape(-1), cnt.reshape(-1), fixed_ok.reshape(-1),
            units.reshape(B * nt, 1, nt * FOX_HEADS), ref)


def _attention(qT, cumT, rb, kst, qst, k, dk, vT, tq):
    B, H, dh, S = qT.shape
    nt = S // tq
    n1, cnt, fixed_ok, units, ref = _attn_schedule(kst, qst)
    resident = dict(pipeline_mode=pl.Buffered(1))
    grid_spec = pltpu.PrefetchScalarGridSpec(
        num_scalar_prefetch=3,
        grid=(B, nt),
        in_specs=[
            pl.BlockSpec((1, 1, nt * H), lambda b, i, *_: (b * nt + i, 0, 0), memory_space=pltpu.SMEM),
            pl.BlockSpec((1, H, dh, tq), lambda b, i, *_: (b, 0, 0, i)),
            pl.BlockSpec((1, H, 1, tq), lambda b, i, *_: (b, 0, 0, i)),
            pl.BlockSpec((1, nt * H, 1, LANES), lambda b, i, *_: (b, 0, 0, 0)),
            pl.BlockSpec((1, H, 1, LANES), lambda b, i, *_: (b * nt + i, 0, 0, 0)),
            pl.BlockSpec((1, HEAD_PAIRS, nt, tq, LANES), lambda b, i, *_: (b, 0, 0, 0, 0), **resident),
            pl.BlockSpec((1, nt, tq, LANES), lambda b, i, *_: (b, 0, 0, 0), **resident),
            pl.BlockSpec((1, H, nt, VROWS, tq), lambda b, i, *_: (b, 0, 0, 0, 0), **resident),
        ],
        out_specs=pl.BlockSpec((1, H, dh, tq), lambda b, i, *_: (b, 0, 0, i)),
        scratch_shapes=[
            pltpu.VMEM((H, QA_ROWS, tq), BF),
            pltpu.VMEM((H, VROWS, tq), F32),
            pltpu.VMEM((H, 1, tq), F32),
            pltpu.VMEM((2, tq, tq), F32),
            pltpu.VMEM((tq, tq), BF),
            pltpu.VMEM((tq, tq), BF),
            pltpu.VMEM((tq, tq), BF),
        ],
    )
    return pl.pallas_call(
        _attn_kernel,
        grid_spec=grid_spec,
        out_shape=jax.ShapeDtypeStruct((B, H, dh, S), BF),
        compiler_params=pltpu.CompilerParams(
            dimension_semantics=("arbitrary", "arbitrary"), vmem_limit_bytes=VMEM_LIMIT),
        name="fox_attn",
    )(n1, cnt, fixed_ok, units, qT, cumT, rb, ref, k, dk, vT)


def _layer1_kernel(x_ref, yT_ref, w_inT_ref, mk_ref, mvT_ref, w_outT_ref, lng_ref, lnb_ref,
                   o_ref, yg_ref):
    tc = x_ref.shape[1] // TOKEN_SPLIT
    for c in range(TOKEN_SPLIT):
        cols = slice(c * tc, (c + 1) * tc)
        x = x_ref[0, cols, :]
        xb = x.astype(BF)
        qmT = _nt(w_inT_ref[0:D_MEM, :], xb)
        gT = _nt(w_inT_ref[D_MEM:, :], xb)
        _finish_layer(x, yT_ref[0, :, cols].astype(F32), qmT, gT, mk_ref, mvT_ref, w_outT_ref,
                      lng_ref, lnb_ref, o_ref, yg_ref, cols)


def _layer1(x1, yT, w_inT, mk, mvT, w_outT, lng, lnb, tm):
    B, S, _ = x1.shape
    return pl.pallas_call(
        _layer1_kernel,
        grid=(B, S // tm),
        in_specs=[
            pl.BlockSpec((1, tm, D_MODEL), lambda b, t: (b, t, 0)),
            pl.BlockSpec((1, D_MAIN, tm), lambda b, t: (b, 0, t)),
            _const_spec(w_inT.shape),
            pl.BlockSpec((1, 1, MEM_HEADS, N_MEM, MEM_DH), lambda b, t: (1, b, 0, 0, 0)),
            pl.BlockSpec((1, 1, MEM_HEADS, MEM_DH, N_MEM), lambda b, t: (1, b, 0, 0, 0)),
            _const_spec(w_outT.shape),
            _const_spec(lng.shape),
            _const_spec(lnb.shape),
        ],
        out_specs=pl.BlockSpec((1, tm, D_MODEL), lambda b, t: (b, t, 0)),
        out_shape=jax.ShapeDtypeStruct((B, S, D_MODEL), F32),
        scratch_shapes=[pltpu.VMEM((D_MIX, tm), BF)],
        compiler_params=pltpu.CompilerParams(
            dimension_semantics=("arbitrary", "arbitrary"), vmem_limit_bytes=VMEM_LIMIT),
        name="layer1",
    )(x1, yT, w_inT, mk, mvT, w_outT, lng, lnb)


def kernel(x, mem, w_in, w_mem_kv, w_out, ln_g, ln_b, pool_w, pool_scale, w_kv_shared, b_forget):
    B, S, _ = x.shape
    tm = min(512, S)

    w_inT = jnp.swapaxes(w_in, 1, 2).astype(BF)
    w_outT = jnp.swapaxes(w_out, 1, 2).astype(BF)
    pool_wT = jnp.swapaxes(pool_w[0], 1, 2).astype(BF)
    ps = pool_scale[0].reshape(D_MAIN, 1)
    lng = ln_g.reshape(DEPTH, 1, D_MODEL)
    lnb = ln_b.reshape(DEPTH, 1, D_MODEL)
    wk = w_kv_shared[:, :D_MAIN].astype(BF)
    wvT = w_kv_shared[:, D_MAIN:2 * D_MAIN].T.astype(BF)
    wf = w_kv_shared[:, 2 * D_MAIN:]
    wf3 = jnp.concatenate([wf, wf, wf, jnp.zeros((D_MODEL, LANES - 3 * FOX_HEADS), F32)], axis=1).astype(BF)
    bias3 = jnp.concatenate([b_forget, b_forget, b_forget,
                             jnp.zeros((LANES - 3 * FOX_HEADS,), F32)]).reshape(1, LANES)

    mk, mvT = _mem_kv(mem, w_mem_kv.astype(BF))
    tl = 2 * tm if S % (2 * tm) == 0 else tm
    x1 = _layer0(x, w_inT[0], pool_wT, ps, mk, mvT, w_outT[0], lng[0], lnb[0], tl)
    k, dk, qT, vT, cumT, rb, kst, qst = _shared_kv(x1, wf3, bias3, wk, w_inT[1, :D_MAIN], wvT, tm)
    yT = _attention(qT, cumT, rb, kst, qst, k, dk, vT, tm)
    return _layer1(x1, yT.reshape(B, D_MAIN, S), w_inT[1, D_MAIN:], mk, mvT, w_outT[1],
                   lng[1], lnb[1], tl)
```

```python
import numpy as np
import jax
import jax.numpy as jnp
from jax import lax
from jax.experimental import pallas as pl
from jax.experimental.pallas import tpu as pltpu

D_MODEL = 1024
N_MEM = 256
D_MAIN = 1024
POOL_WINDOWS = (2, 4, 8, 16)
POOL_GROUP = 256
FOX_HEADS = 16
FOX_DH = 64
MEM_HEADS = 4
MEM_DH = 128
D_MEM = MEM_HEADS * MEM_DH
D_MIX = D_MAIN + D_MEM
DEPTH = 2
ALPHA = (2 * DEPTH) ** 0.25
LN_EPS = 1e-5

LANES = 128
BF16_ROWS = 16
HEAD_PAIRS = FOX_HEADS // 2
QA_ROWS = 2 * LANES
VROWS = FOX_DH + BF16_ROWS
NEG = -1e30
LOG2E = 1.4426950408889634
UNDERFLOW_MARGIN = 160.0
NORM_SLACK = 1.02
BODY_TICKS = 6
REF_HEADROOM = 40.0
REF_SPREAD_LIMIT = 100.0
TOKEN_SPLIT = 2
VMEM_LIMIT = 56 * 1024 * 1024

BF = jnp.bfloat16
F32 = jnp.float32


def _mm(a, b):
    return jnp.dot(a, b, preferred_element_type=F32)


def _nt(a, b):
    return lax.dot_general(a, b, (((1,), (1,)), ((), ())), preferred_element_type=F32)


def _split3(v):
    hi = v.astype(BF)
    r1 = v - hi.astype(F32)
    mid = r1.astype(BF)
    lo = (r1 - mid.astype(F32)).astype(BF)
    return hi, mid, lo


def _silu(g):
    hg = 0.5 * g
    return hg + hg * jnp.tanh(hg)


def _const_spec(shape):
    nd = len(shape)
    return pl.BlockSpec(shape, lambda *_: (0,) * nd)


def _mem_kv_kernel(mem_ref, w_ref, mk_ref, mvT_ref):
    memb = mem_ref[0].astype(BF)
    mkv = _mm(memb, w_ref[0])
    for h in range(MEM_HEADS):
        mk_ref[0, 0, h] = mkv[:, h * MEM_DH:(h + 1) * MEM_DH].astype(BF)
        mv = mkv[:, D_MEM + h * MEM_DH:D_MEM + (h + 1) * MEM_DH]
        mvT_ref[0, 0, h] = mv.T.astype(BF)


def _mem_kv(mem, w_mem_kv_bf):
    B = mem.shape[0]
    L = w_mem_kv_bf.shape[0]
    return pl.pallas_call(
        _mem_kv_kernel,
        grid=(L, B),
        in_specs=[
            pl.BlockSpec((1, N_MEM, D_MODEL), lambda l, b: (b, 0, 0)),
            pl.BlockSpec((1, D_MODEL, 2 * D_MEM), lambda l, b: (l, 0, 0)),
        ],
        out_specs=[
            pl.BlockSpec((1, 1, MEM_HEADS, N_MEM, MEM_DH), lambda l, b: (l, b, 0, 0, 0)),
            pl.BlockSpec((1, 1, MEM_HEADS, MEM_DH, N_MEM), lambda l, b: (l, b, 0, 0, 0)),
        ],
        out_shape=[
            jax.ShapeDtypeStruct((L, B, MEM_HEADS, N_MEM, MEM_DH), BF),
            jax.ShapeDtypeStruct((L, B, MEM_HEADS, MEM_DH, N_MEM), BF),
        ],
        compiler_params=pltpu.CompilerParams(
            dimension_semantics=("arbitrary", "arbitrary"), vmem_limit_bytes=VMEM_LIMIT),
        name="mem_kv",
    )(mem, w_mem_kv_bf)


def _finish_layer(x, mainT, qmT, gT, mk_ref, mvT_ref, w_outT_ref, lng_ref, lnb_ref, o_ref, yg_ref, cols):
    yg_ref = yg_ref.at[:, cols]
    yg_ref[0:D_MAIN, :] = (mainT * _silu(gT[0:D_MAIN])).astype(BF)
    for h in range(MEM_HEADS):
        rows = slice(h * MEM_DH, (h + 1) * MEM_DH)
        qh = qmT[rows].astype(BF)
        lg = _mm(mk_ref[0, 0, h], qh) * (MEM_DH ** -0.5 * LOG2E)
        mx = jnp.max(lg, axis=0, keepdims=True)
        e = jnp.exp2(lg - mx)
        den = jnp.sum(e, axis=0, keepdims=True)
        ym = _mm(mvT_ref[0, 0, h], e.astype(BF)) * (1.0 / den)
        gm = gT[D_MAIN + h * MEM_DH:D_MAIN + (h + 1) * MEM_DH]
        yg_ref[D_MAIN + h * MEM_DH:D_MAIN + (h + 1) * MEM_DH, :] = (ym * _silu(gm)).astype(BF)
    outT = _mm(w_outT_ref[...], yg_ref[...])
    z = ALPHA * x + outT.T
    mu = jnp.mean(z, axis=-1, keepdims=True)
    zc = z - mu
    var = jnp.mean(zc * zc, axis=-1, keepdims=True)
    o_ref[0, cols, :] = zc * lax.rsqrt(var + LN_EPS) * lng_ref[...] + lnb_ref[...]


def _layer0_kernel(x_ref, w_inT_ref, band_ref, pool_wT_ref, ps_ref, mk_ref, mvT_ref, w_outT_ref,
                   lng_ref, lnb_ref, o_ref, tail_ref, main_ref, yg_ref):
    t = pl.program_id(1)
    x = x_ref[0]
    tm = x.shape[0]
    xb = x.astype(BF)
    uT = _nt(w_inT_ref[0:D_MAIN, :], xb)
    qmT = _nt(w_inT_ref[D_MAIN:D_MIX, :], xb)
    gT = _nt(w_inT_ref[D_MIX:, :], xb)

    @pl.when(t == 0)
    def _():
        tail_ref[...] = jnp.zeros_like(tail_ref)

    tpos = t * tm + lax.broadcasted_iota(jnp.int32, (1, tm), 1)
    ub = uT.astype(BF)
    for g, w in enumerate(POOL_WINDOWS):
        rows = slice(g * POOL_GROUP, (g + 1) * POOL_GROUP)
        ext = jnp.concatenate([tail_ref[rows, :], ub[rows]], axis=1)
        wsum = jnp.concatenate(
            [_mm(ext[:, n * LANES:(n + 2) * LANES], band_ref[g]) for n in range(tm // LANES)], axis=1)
        inv_cnt = 1.0 / jnp.minimum(tpos + 1, w).astype(F32)
        pm = wsum * inv_cnt - uT[rows]
        main_ref[rows, :] = _mm(pool_wT_ref[g], pm.astype(BF)) * ps_ref[rows, :]
    tail_ref[...] = ub[:, tm - LANES:]

    _finish_layer(x, main_ref[...], qmT, gT, mk_ref, mvT_ref, w_outT_ref, lng_ref, lnb_ref,
                  o_ref, yg_ref, slice(0, tm))


def _pool_bands():
    s = np.arange(2 * LANES)[:, None]
    c = np.arange(LANES)[None, :] + LANES
    return np.stack([((s > c - w) & (s <= c)).astype(np.float32) for w in POOL_WINDOWS])


def _layer0(x, w_inT, pool_wT, ps, mk, mvT, w_outT, lng, lnb, tm):
    B, S, _ = x.shape
    band = jnp.asarray(_pool_bands(), BF)
    return pl.pallas_call(
        _layer0_kernel,
        grid=(B, S // tm),
        in_specs=[
            pl.BlockSpec((1, tm, D_MODEL), lambda b, t: (b, t, 0)),
            _const_spec(w_inT.shape),
            _const_spec(band.shape),
            _const_spec(pool_wT.shape),
            _const_spec(ps.shape),
            pl.BlockSpec((1, 1, MEM_HEADS, N_MEM, MEM_DH), lambda b, t: (0, b, 0, 0, 0)),
            pl.BlockSpec((1, 1, MEM_HEADS, MEM_DH, N_MEM), lambda b, t: (0, b, 0, 0, 0)),
            _const_spec(w_outT.shape),
            _const_spec(lng.shape),
            _const_spec(lnb.shape),
        ],
        out_specs=pl.BlockSpec((1, tm, D_MODEL), lambda b, t: (b, t, 0)),
        out_shape=jax.ShapeDtypeStruct((B, S, D_MODEL), F32),
        scratch_shapes=[
            pltpu.VMEM((D_MAIN, LANES), BF),
            pltpu.VMEM((D_MAIN, tm), F32),
            pltpu.VMEM((D_MIX, tm), BF),
        ],
        compiler_params=pltpu.CompilerParams(
            dimension_semantics=("arbitrary", "arbitrary"), vmem_limit_bytes=VMEM_LIMIT),
        name="layer0",
    )(x, w_inT, band, pool_wT, ps, mk, mvT, w_outT, lng, lnb)


def _shared_kv_kernel(x_ref, wf3_ref, bias3_ref, wk_ref, wqT_ref, wvT_ref, hsum_ref,
                      k_ref, dk_ref, qT_ref, vT_ref, cumT_ref, rb_ref, kst_ref, qst_ref, carry_ref):
    t = pl.program_id(1)

    @pl.when(t == 0)
    def _():
        carry_ref[...] = jnp.zeros_like(carry_ref)

    xb = x_ref[0].astype(BF)
    tm = xb.shape[0]
    lane = lax.broadcasted_iota(jnp.int32, (tm, LANES), 1)
    f3 = _mm(xb, wf3_ref[...]) + bias3_ref[...]
    lf = jnp.where(lane < 3 * FOX_HEADS, jax.nn.log_sigmoid(f3), 0.0)

    cb = min(256, tm)
    ri = lax.broadcasted_iota(jnp.int32, (cb, cb), 0)
    ci = lax.broadcasted_iota(jnp.int32, (cb, cb), 1)
    tri = jnp.where(ci <= ri, 1.0, 0.0).astype(BF)
    off = jnp.zeros((1, LANES), F32)
    blocks = []
    for i in range(tm // cb):
        hi, mid, lo = _split3(lf[i * cb:(i + 1) * cb])
        c = _mm(tri, hi) + _mm(tri, mid) + _mm(tri, lo) + off
        off = c[cb - 1:cb, :]
        blocks.append(c)
    c_loc = jnp.concatenate(blocks, axis=0) * LOG2E
    cum = c_loc + carry_ref[...]
    carry_ref[...] = cum[tm - 1:tm, :]

    hi, mid, lo = (v.astype(F32) for v in _split3(c_loc[0:1, :] - c_loc))
    dk_ref[0, 0] = jnp.where(lane < FOX_HEADS, hi,
                             jnp.where(lane < 2 * FOX_HEADS, mid,
                                       jnp.where(lane < 3 * FOX_HEADS, lo, 0.0))).astype(BF)
    kb = _mm(xb, wk_ref[...]).astype(BF)
    for g in range(HEAD_PAIRS):
        k_ref[0, g, 0] = kb[:, g * LANES:(g + 1) * LANES]

    qT = (_nt(wqT_ref[...], xb) * (FOX_DH ** -0.5 * LOG2E)).astype(BF)
    vT = _nt(wvT_ref[...], xb).astype(BF)

    kf32 = kb.astype(F32)
    kn2 = jnp.max(_mm((kf32 * kf32).astype(BF), hsum_ref[...]), axis=0, keepdims=True)
    cmax = jnp.max(cum, axis=0, keepdims=True)
    cmin = jnp.min(cum, axis=0, keepdims=True)
    srow = lax.broadcasted_iota(jnp.int32, (8, LANES), 0)
    kst_ref[0, 0] = jnp.where(srow == 0, cmax, jnp.where(srow == 1, cmin, jnp.where(srow == 2, kn2, 0.0)))
    qf32 = qT.astype(F32)
    qn2 = jnp.sum((qf32 * qf32).reshape(FOX_HEADS, FOX_DH, tm), axis=1)
    qst_ref[0, 0] = jnp.broadcast_to(jnp.max(qn2, axis=1, keepdims=True), (FOX_HEADS, LANES))
    ones_rows = jnp.where(lax.broadcasted_iota(jnp.int32, (BF16_ROWS, tm), 0) == 0, 1.0, 0.0).astype(BF)
    for h in range(FOX_HEADS):
        qT_ref[0, h] = qT[h * FOX_DH:(h + 1) * FOX_DH]
        vT_ref[0, h, 0, 0:FOX_DH, :] = vT[h * FOX_DH:(h + 1) * FOX_DH]
        vT_ref[0, h, 0, FOX_DH:, :] = ones_rows

    cT = cum.T
    for h in range(FOX_HEADS):
        cumT_ref[0, h] = cT[h:h + 1]
        rb_ref[0, h] = jnp.broadcast_to(cT[h:h + 1, 0:1], (1, LANES))


def _head_sum_matrix():
    g = np.zeros((D_MAIN, LANES), np.float32)
    for h in range(FOX_HEADS):
        g[h * FOX_DH:(h + 1) * FOX_DH, h] = 1.0
    return g


def _shared_kv(x1, wf3, bias3, wk, wqT, wvT, tm):
    B, S, _ = x1.shape
    nt = S // tm
    hsum = jnp.asarray(_head_sum_matrix(), BF)
    return pl.pallas_call(
        _shared_kv_kernel,
        grid=(B, nt),
        in_specs=[
            pl.BlockSpec((1, tm, D_MODEL), lambda b, t: (b, t, 0)),
            _const_spec(wf3.shape),
            _const_spec(bias3.shape),
            _const_spec(wk.shape),
            _const_spec(wqT.shape),
            _const_spec(wvT.shape),
            _const_spec(hsum.shape),
        ],
        out_specs=[
            pl.BlockSpec((1, HEAD_PAIRS, 1, tm, LANES), lambda b, t: (b, 0, t, 0, 0)),
            pl.BlockSpec((1, 1, tm, LANES), lambda b, t: (b, t, 0, 0)),
            pl.BlockSpec((1, FOX_HEADS, FOX_DH, tm), lambda b, t: (b, 0, 0, t)),
            pl.BlockSpec((1, FOX_HEADS, 1, VROWS, tm), lambda b, t: (b, 0, t, 0, 0)),
            pl.BlockSpec((1, FOX_HEADS, 1, tm), lambda b, t: (b, 0, 0, t)),
            pl.BlockSpec((1, FOX_HEADS, 1, LANES), lambda b, t: (b, t, 0, 0)),
            pl.BlockSpec((1, 1, 8, LANES), lambda b, t: (b, t, 0, 0)),
            pl.BlockSpec((1, 1, FOX_HEADS, LANES), lambda b, t: (b, t, 0, 0)),
        ],
        out_shape=[
            jax.ShapeDtypeStruct((B, HEAD_PAIRS, nt, tm, LANES), BF),
            jax.ShapeDtypeStruct((B, nt, tm, LANES), BF),
            jax.ShapeDtypeStruct((B, FOX_HEADS, FOX_DH, S), BF),
            jax.ShapeDtypeStruct((B, FOX_HEADS, nt, VROWS, tm), BF),
            jax.ShapeDtypeStruct((B, FOX_HEADS, 1, S), F32),
            jax.ShapeDtypeStruct((B, nt * FOX_HEADS, 1, LANES), F32),
            jax.ShapeDtypeStruct((B, nt, 8, LANES), F32),
            jax.ShapeDtypeStruct((B, nt, FOX_HEADS, LANES), F32),
        ],
        scratch_shapes=[pltpu.VMEM((1, LANES), F32)],
        compiler_params=pltpu.CompilerParams(
            dimension_semantics=("arbitrary", "arbitrary"), vmem_limit_bytes=VMEM_LIMIT),
        name="shared_kv",
    )(x1, wf3, bias3, wk, wqT, wvT, hsum)


def _attn_kernel(n1_ref, cnt_ref, fx_ref, ul_ref, qT_ref, cumT_ref, rb_ref, ref_ref, k_ref, dk_ref, vT_ref,
                 y_ref, qa_ref, o_ref, m_ref, s_ref, pa_ref, pb_ref, pc_ref):
    b = pl.program_id(0)
    i = pl.program_id(1)
    step = b * pl.num_programs(1) + i
    n1 = n1_ref[step]
    fixed_ok = fx_ref[step]
    tq = qa_ref.shape[2]
    tk = k_ref.shape[3]

    rowi = lax.broadcasted_iota(jnp.int32, (LANES, tq), 0)
    zeros = jnp.zeros((FOX_DH, tq), BF)
    for h in range(FOX_HEADS):
        lo = (h % 2) * FOX_DH
        qa_ref[h, lo:lo + FOX_DH, :] = qT_ref[0, h]
        qa_ref[h, FOX_DH - lo:2 * FOX_DH - lo, :] = zeros
        piece_rows = (rowi == h) | (rowi == FOX_HEADS + h) | (rowi == 2 * FOX_HEADS + h)
        qa_ref[h, LANES:, :] = jnp.where(piece_rows, 1.0, 0.0).astype(BF)
    m_ref[...] = jnp.full(m_ref.shape, NEG, F32)
    o_ref[...] = jnp.zeros_like(o_ref)

    def logits(h, j, m_prev, slot, masked):
        pair_idx = lax.shift_right_logical(jnp.int32(h), 1)
        lhs = jnp.concatenate([k_ref[0, pair_idx, j], dk_ref[0, j]], axis=1)
        s = _mm(lhs, qa_ref[h])
        if masked:
            r = lax.broadcasted_iota(jnp.int32, (tk, tq), 0)
            c = lax.broadcasted_iota(jnp.int32, (tk, tq), 1)
            s = jnp.where(r <= c, s, NEG)
        a = cumT_ref[0, h] - jnp.tile(rb_ref[0, j * FOX_HEADS + h], (1, tq // LANES))
        m_cur = jnp.maximum(m_prev, jnp.max(s, axis=0, keepdims=True) + a)
        s_ref[slot] = s
        return m_cur, m_cur - a, jnp.exp2(m_prev - m_cur)

    def probs(slot, shift):
        (pa_ref, pb_ref)[slot][...] = jnp.exp2(s_ref[slot] - shift).astype(BF)

    def values(h, j, slot, al):
        o_ref[h] = al * o_ref[h] + _mm(vT_ref[0, h, j], (pa_ref, pb_ref)[slot][...])

    def sweep(unit, n, masked):
        (h0, j0), (h1, j1), (h2, j2), (h3, j3) = unit(0), unit(1), unit(2), unit(3)
        m, sh_e, a_e = logits(h0, j0, m_ref[h0], 0, masked)
        m_ref[h0] = m
        m, sh_o, a_o = logits(h1, j1, m_ref[h1], 1, masked)
        m_ref[h1] = m
        probs(0, sh_e)
        a_c = a_e
        m, sh_e, a_e = logits(h2, j2, m_ref[h2], 0, masked)
        m_ref[h2] = m
        probs(1, sh_o)
        values(h0, j0, 0, a_c)
        a_c = a_o
        m, sh_o, a_o = logits(h3, j3, m_ref[h3], 1, masked)
        m_ref[h3] = m

        def pair(g, carry):
            sh_e, a_e, sh_o, a_o, a_c = carry
            t = 2 * g
            (hp, jp), (he, je) = unit(t - 1), unit(t)
            (he2, je2), (ho2, jo2) = unit(t + 2), unit(t + 3)
            mp_e, mp_o = m_ref[he2], m_ref[ho2]
            probs(0, sh_e)
            m_e2, sh_e2, a_e2 = logits(he2, je2, mp_e, 0, masked)
            values(hp, jp, 1, a_c)
            probs(1, sh_o)
            mp_o = jnp.where(ho2 == he2, m_e2, mp_o)
            m_o2, sh_o2, a_o2 = logits(ho2, jo2, mp_o, 1, masked)
            values(he, je, 0, a_e)
            m_ref[he2] = m_e2
            m_ref[ho2] = m_o2
            return sh_e2, a_e2, sh_o2, a_o2, a_o

        sh_e, a_e, sh_o, a_o, a_c = lax.fori_loop(1, n // 2 - 1, pair, (sh_e, a_e, sh_o, a_o, a_c))
        (hp, jp), (he, je), (ho, jo) = unit(n - 3), unit(n - 2), unit(n - 1)
        probs(0, sh_e)
        values(hp, jp, 1, a_c)
        probs(1, sh_o)
        values(he, je, 0, a_e)
        values(ho, jo, 1, a_o)

    def fixed_probs(t, unit, cnt, masked):
        h, j = unit(t)
        pair_idx = lax.shift_right_logical(jnp.int32(h), 1)
        lhs = jnp.concatenate([k_ref[0, pair_idx, j], dk_ref[0, j]], axis=1)
        s = _mm(lhs, qa_ref[h])
        if masked:
            r = lax.broadcasted_iota(jnp.int32, (tk, tq), 0)
            c = lax.broadcasted_iota(jnp.int32, (tk, tq), 1)
            s = jnp.where(r <= c, s, NEG)
        shift = jnp.tile(rb_ref[0, j * FOX_HEADS + h] + ref_ref[0, h], (1, tq // LANES)) - cumT_ref[0, h]
        if cnt is not None:
            shift = shift + jnp.where(t >= cnt, -NEG, 0.0)
        return jnp.exp2(s - shift).astype(BF)

    def fixed_values(t, unit, p_buf):
        h, j = unit(t)
        o_ref[h] += _mm(vT_ref[0, h, j], p_buf[...])

    def fixed_sweep(unit, n, cnt, masked):
        bufs = (pa_ref, pb_ref, pc_ref)
        bufs[0][...] = fixed_probs(0, unit, cnt, masked)
        bufs[1][...] = fixed_probs(1, unit, cnt, masked)

        def tick(t, k, last_ab):
            fixed_values(t, unit, bufs[k])
            if last_ab is None or t + 2 <= last_ab:
                bufs[(k + 2) % 3][...] = fixed_probs(t + 2, unit, cnt, masked)

        def body(g, carry):
            for k in range(BODY_TICKS):
                tick(BODY_TICKS * g + k, k % 3, None)
            return carry

        n_loop = (n - 2) // BODY_TICKS
        lax.fori_loop(0, n_loop, body, 0)
        if isinstance(n, int):
            for t in range(BODY_TICKS * n_loop, n):
                tick(t, t % 3, n - 1)
        else:
            fixed_values(n - 2, unit, bufs[0])
            fixed_values(n - 1, unit, bufs[1])

    def listed_unit(t):
        code = ul_ref[0, 0, t]
        return code & (FOX_HEADS - 1), lax.shift_right_logical(code, 4)

    diag_unit = lambda t: (t, i)

    @pl.when(jnp.logical_and(fixed_ok == 1, n1 > 0))
    def _():
        cnt = cnt_ref[step]
        fixed_sweep(listed_unit, cnt + (2 - cnt) % BODY_TICKS, cnt, False)

    @pl.when(fixed_ok == 1)
    def _():
        fixed_sweep(diag_unit, FOX_HEADS, None, True)

    @pl.when(jnp.logical_and(fixed_ok == 0, n1 > 0))
    def _():
        sweep(listed_unit, n1, False)

    @pl.when(fixed_ok == 0)
    def _():
        sweep(diag_unit, FOX_HEADS, True)

    for h in range(FOX_HEADS):
        y_ref[0, h] = (o_ref[h, 0:FOX_DH, :] / o_ref[h, FOX_DH:FOX_DH + 1, :]).astype(BF)


def _attn_schedule(kst, qst):
    B, nt = kst.shape[0], kst.shape[1]
    cmax = kst[:, :, 0, :FOX_HEADS]
    cmin = kst[:, :, 1, :FOX_HEADS]
    kn = jnp.sqrt(kst[:, :, 2, :FOX_HEADS]) * NORM_SLACK
    qn = jnp.sqrt(qst[:, :, :, 0]) * NORM_SLACK
    upper = qn[:, :, None] * kn[:, None] + cmax[:, :, None] - cmin[:, None]
    lower = -qn * kn
    need = upper > lower[:, :, None] - UNDERFLOW_MARGIN
    ti = jnp.arange(nt)
    valid = (ti[None, :] < ti[:, None])[None, :, :, None]
    rank = jnp.where(valid & need, 0, jnp.where(valid, 1, 2)).reshape(B, nt, nt * FOX_HEADS)
    units = jnp.argsort(rank, axis=-1, stable=True).astype(jnp.int32)
    cnt = jnp.sum(rank == 0, axis=-1).astype(jnp.int32)
    n1 = jnp.where(cnt > 0, jnp.maximum(cnt + (cnt & 1), 4), 0)

    top = qn * lax.cummax(kn, axis=1)
    ref = 0.5 * (top + lower) - REF_HEADROOM
    fixed_ok = jnp.all(top - lower <= REF_SPREAD_LIMIT, axis=-1).astype(jnp.int32)
    ref = jnp.broadcast_to(ref.reshape(B * nt, FOX_HEADS, 1, 1), (B * nt, FOX_HEADS, 1, LANES))
    return (n1.reshape(-1), cnt.reshape(-1), fixed_ok.reshape(-1),
            units.reshape(B * nt, 1, nt * FOX_HEADS), ref)


def _attention(qT, cumT, rb, kst, qst, k, dk, vT, tq):
    B, H, dh, S = qT.shape
    nt = S // tq
    n1, cnt, fixed_ok, units, ref = _attn_schedule(kst, qst)
    resident = dict(pipeline_mode=pl.Buffered(1))
    grid_spec = pltpu.PrefetchScalarGridSpec(
        num_scalar_prefetch=3,
        grid=(B, nt),
        in_specs=[
            pl.BlockSpec((1, 1, nt * H), lambda b, i, *_: (b * nt + i, 0, 0), memory_space=pltpu.SMEM),
            pl.BlockSpec((1, H, dh, tq), lambda b, i, *_: (b, 0, 0, i)),
            pl.BlockSpec((1, H, 1, tq), lambda b, i, *_: (b, 0, 0, i)),
            pl.BlockSpec((1, nt * H, 1, LANES), lambda b, i, *_: (b, 0, 0, 0)),
            pl.BlockSpec((1, H, 1, LANES), lambda b, i, *_: (b * nt + i, 0, 0, 0)),
            pl.BlockSpec((1, HEAD_PAIRS, nt, tq, LANES), lambda b, i, *_: (b, 0, 0, 0, 0), **resident),
            pl.BlockSpec((1, nt, tq, LANES), lambda b, i, *_: (b, 0, 0, 0), **resident),
            pl.BlockSpec((1, H, nt, VROWS, tq), lambda b, i, *_: (b, 0, 0, 0, 0), **resident),
        ],
        out_specs=pl.BlockSpec((1, H, dh, tq), lambda b, i, *_: (b, 0, 0, i)),
        scratch_shapes=[
            pltpu.VMEM((H, QA_ROWS, tq), BF),
            pltpu.VMEM((H, VROWS, tq), F32),
            pltpu.VMEM((H, 1, tq), F32),
            pltpu.VMEM((2, tq, tq), F32),
            pltpu.VMEM((tq, tq), BF),
            pltpu.VMEM((tq, tq), BF),
            pltpu.VMEM((tq, tq), BF),
        ],
    )
    return pl.pallas_call(
        _attn_kernel,
        grid_spec=grid_spec,
        out_shape=jax.ShapeDtypeStruct((B, H, dh, S), BF),
        compiler_params=pltpu.CompilerParams(
            dimension_semantics=("arbitrary", "arbitrary"), vmem_limit_bytes=VMEM_LIMIT),
        name="fox_attn",
    )(n1, cnt, fixed_ok, units, qT, cumT, rb, ref, k, dk, vT)


def _layer1_kernel(x_ref, yT_ref, w_inT_ref, mk_ref, mvT_ref, w_outT_ref, lng_ref, lnb_ref,
                   o_ref, yg_ref):
    tc = x_ref.shape[1] // TOKEN_SPLIT
    for c in range(TOKEN_SPLIT):
        cols = slice(c * tc, (c + 1) * tc)
        x = x_ref[0, cols, :]
        xb = x.astype(BF)
        qmT = _nt(w_inT_ref[0:D_MEM, :], xb)
        gT = _nt(w_inT_ref[D_MEM:, :], xb)
        _finish_layer(x, yT_ref[0, :, cols].astype(F32), qmT, gT, mk_ref, mvT_ref, w_outT_ref,
                      lng_ref, lnb_ref, o_ref, yg_ref, cols)


def _layer1(x1, yT, w_inT, mk, mvT, w_outT, lng, lnb, tm):
    B, S, _ = x1.shape
    return pl.pallas_call(
        _layer1_kernel,
        grid=(B, S // tm),
        in_specs=[
            pl.BlockSpec((1, tm, D_MODEL), lambda b, t: (b, t, 0)),
            pl.BlockSpec((1, D_MAIN, tm), lambda b, t: (b, 0, t)),
            _const_spec(w_inT.shape),
            pl.BlockSpec((1, 1, MEM_HEADS, N_MEM, MEM_DH), lambda b, t: (1, b, 0, 0, 0)),
            pl.BlockSpec((1, 1, MEM_HEADS, MEM_DH, N_MEM), lambda b, t: (1, b, 0, 0, 0)),
            _const_spec(w_outT.shape),
            _const_spec(lng.shape),
            _const_spec(lnb.shape),
        ],
        out_specs=pl.BlockSpec((1, tm, D_MODEL), lambda b, t: (b, t, 0)),
        out_shape=jax.ShapeDtypeStruct((B, S, D_MODEL), F32),
        scratch_shapes=[pltpu.VMEM((D_MIX, tm), BF)],
        compiler_params=pltpu.CompilerParams(
            dimension_semantics=("arbitrary", "arbitrary"), vmem_limit_bytes=VMEM_LIMIT),
        name="layer1",
    )(x1, yT, w_inT, mk, mvT, w_outT, lng, lnb)


def kernel(x, mem, w_in, w_mem_kv, w_out, ln_g, ln_b, pool_w, pool_scale, w_kv_shared, b_forget):
    B, S, _ = x.shape
    tm = min(512, S)

    w_inT = jnp.swapaxes(w_in, 1, 2).astype(BF)
    w_outT = jnp.swapaxes(w_out, 1, 2).astype(BF)
    pool_wT = jnp.swapaxes(pool_w[0], 1, 2).astype(BF)
    ps = pool_scale[0].reshape(D_MAIN, 1)
    lng = ln_g.reshape(DEPTH, 1, D_MODEL)
    lnb = ln_b.reshape(DEPTH, 1, D_MODEL)
    wk = w_kv_shared[:, :D_MAIN].astype(BF)
    wvT = w_kv_shared[:, D_MAIN:2 * D_MAIN].T.astype(BF)
    wf = w_kv_shared[:, 2 * D_MAIN:]
    wf3 = jnp.concatenate([wf, wf, wf, jnp.zeros((D_MODEL, LANES - 3 * FOX_HEADS), F32)], axis=1).astype(BF)
    bias3 = jnp.concatenate([b_forget, b_forget, b_forget,
                             jnp.zeros((LANES - 3 * FOX_HEADS,), F32)]).reshape(1, LANES)

    mk, mvT = _mem_kv(mem, w_mem_kv.astype(BF))
    tl = 2 * tm if S % (2 * tm) == 0 else tm
    x1 = _layer0(x, w_inT[0], pool_wT, ps, mk, mvT, w_outT[0], lng[0], lnb[0], tl)
    k, dk, qT, vT, cumT, rb, kst, qst = _shared_kv(x1, wf3, bias3, wk, w_inT[1, :D_MAIN], wvT, tm)
    yT = _attention(qT, cumT, rb, kst, qst, k, dk, vT, tm)
    return _layer1(x1, yT.reshape(B, D_MAIN, S), w_inT[1, D_MAIN:], mk, mvT, w_outT[1],
                   lng[1], lnb[1], tl)
```

```python
import numpy as np
import jax
import jax.numpy as jnp
from jax import lax
from jax.experimental import pallas as pl
from jax.experimental.pallas import tpu as pltpu

D_MODEL = 1024
N_MEM = 256
D_MAIN = 1024
POOL_WINDOWS = (2, 4, 8, 16)
POOL_GROUP = 256
FOX_HEADS = 16
FOX_DH = 64
MEM_HEADS = 4
MEM_DH = 128
D_MEM = MEM_HEADS * MEM_DH
D_MIX = D_MAIN + D_MEM
DEPTH = 2
ALPHA = (2 * DEPTH) ** 0.25
LN_EPS = 1e-5

LANES = 128
BF16_ROWS = 16
HEAD_PAIRS = FOX_HEADS // 2
QA_ROWS = 2 * LANES
VROWS = FOX_DH + BF16_ROWS
NEG = -1e30
LOG2E = 1.4426950408889634
UNDERFLOW_MARGIN = 160.0
NORM_SLACK = 1.02
BODY_TICKS = 6
REF_HEADROOM = 40.0
REF_SPREAD_LIMIT = 100.0
TOKEN_SPLIT = 2
VMEM_LIMIT = 56 * 1024 * 1024

BF = jnp.bfloat16
F32 = jnp.float32


def _mm(a, b):
    return jnp.dot(a, b, preferred_element_type=F32)


def _nt(a, b):
    return lax.dot_general(a, b, (((1,), (1,)), ((), ())), preferred_element_type=F32)


def _split3(v):
    hi = v.astype(BF)
    r1 = v - hi.astype(F32)
    mid = r1.astype(BF)
    lo = (r1 - mid.astype(F32)).astype(BF)
    return hi, mid, lo


def _silu(g):
    hg = 0.5 * g
    return hg + hg * jnp.tanh(hg)


def _const_spec(shape):
    nd = len(shape)
    return pl.BlockSpec(shape, lambda *_: (0,) * nd)


def _mem_kv_kernel(mem_ref, w_ref, mk_ref, mvT_ref):
    memb = mem_ref[0].astype(BF)
    mkv = _mm(memb, w_ref[0])
    for h in range(MEM_HEADS):
        mk_ref[0, 0, h] = mkv[:, h * MEM_DH:(h + 1) * MEM_DH].astype(BF)
        mv = mkv[:, D_MEM + h * MEM_DH:D_MEM + (h + 1) * MEM_DH]
        mvT_ref[0, 0, h] = mv.T.astype(BF)


def _mem_kv(mem, w_mem_kv_bf):
    B = mem.shape[0]
    L = w_mem_kv_bf.shape[0]
    return pl.pallas_call(
        _mem_kv_kernel,
        grid=(L, B),
        in_specs=[
            pl.BlockSpec((1, N_MEM, D_MODEL), lambda l, b: (b, 0, 0)),
            pl.BlockSpec((1, D_MODEL, 2 * D_MEM), lambda l, b: (l, 0, 0)),
        ],
        out_specs=[
            pl.BlockSpec((1, 1, MEM_HEADS, N_MEM, MEM_DH), lambda l, b: (l, b, 0, 0, 0)),
            pl.BlockSpec((1, 1, MEM_HEADS, MEM_DH, N_MEM), lambda l, b: (l, b, 0, 0, 0)),
        ],
        out_shape=[
            jax.ShapeDtypeStruct((L, B, MEM_HEADS, N_MEM, MEM_DH), BF),
            jax.ShapeDtypeStruct((L, B, MEM_HEADS, MEM_DH, N_MEM), BF),
        ],
        compiler_params=pltpu.CompilerParams(
            dimension_semantics=("arbitrary", "arbitrary"), vmem_limit_bytes=VMEM_LIMIT),
        name="mem_kv",
    )(mem, w_mem_kv_bf)


def _finish_layer(x, mainT, qmT, gT, mk_ref, mvT_ref, w_outT_ref, lng_ref, lnb_ref, o_ref, yg_ref, cols):
    yg_ref = yg_ref.at[:, cols]
    yg_ref[0:D_MAIN, :] = (mainT * _silu(gT[0:D_MAIN])).astype(BF)
    for h in range(MEM_HEADS):
        rows = slice(h * MEM_DH, (h + 1) * MEM_DH)
        qh = qmT[rows].astype(BF)
        lg = _mm(mk_ref[0, 0, h], qh) * (MEM_DH ** -0.5 * LOG2E)
        mx = jnp.max(lg, axis=0, keepdims=True)
        e = jnp.exp2(lg - mx)
        den = jnp.sum(e, axis=0, keepdims=True)
        ym = _mm(mvT_ref[0, 0, h], e.astype(BF)) * (1.0 / den)
        gm = gT[D_MAIN + h * MEM_DH:D_MAIN + (h + 1) * MEM_DH]
        yg_ref[D_MAIN + h * MEM_DH:D_MAIN + (h + 1) * MEM_DH, :] = (ym * _silu(gm)).astype(BF)
    outT = _mm(w_outT_ref[...], yg_ref[...])
    z = ALPHA * x + outT.T
    mu = jnp.mean(z, axis=-1, keepdims=True)
    zc = z - mu
    var = jnp.mean(zc * zc, axis=-1, keepdims=True)
    o_ref[0, cols, :] = zc * lax.rsqrt(var + LN_EPS) * lng_ref[...] + lnb_ref[...]


def _layer0_kernel(x_ref, w_inT_ref, band_ref, pool_wT_ref, ps_ref, mk_ref, mvT_ref, w_outT_ref,
                   lng_ref, lnb_ref, o_ref, tail_ref, main_ref, yg_ref):
    t = pl.program_id(1)
    x = x_ref[0]
    tm = x.shape[0]
    xb = x.astype(BF)
    uT = _nt(w_inT_ref[0:D_MAIN, :], xb)
    qmT = _nt(w_inT_ref[D_MAIN:D_MIX, :], xb)
    gT = _nt(w_inT_ref[D_MIX:, :], xb)

    @pl.when(t == 0)
    def _():
        tail_ref[...] = jnp.zeros_like(tail_ref)

    tpos = t * tm + lax.broadcasted_iota(jnp.int32, (1, tm), 1)
    ub = uT.astype(BF)
    for g, w in enumerate(POOL_WINDOWS):
        rows = slice(g * POOL_GROUP, (g + 1) * POOL_GROUP)
        ext = jnp.concatenate([tail_ref[rows, :], ub[rows]], axis=1)
        wsum = jnp.concatenate(
            [_mm(ext[:, n * LANES:(n + 2) * LANES], band_ref[g]) for n in range(tm // LANES)], axis=1)
        inv_cnt = 1.0 / jnp.minimum(tpos + 1, w).astype(F32)
        pm = wsum * inv_cnt - uT[rows]
        main_ref[rows, :] = _mm(pool_wT_ref[g], pm.astype(BF)) * ps_ref[rows, :]
    tail_ref[...] = ub[:, tm - LANES:]

    _finish_layer(x, main_ref[...], qmT, gT, mk_ref, mvT_ref, w_outT_ref, lng_ref, lnb_ref,
                  o_ref, yg_ref, slice(0, tm))


def _pool_bands():
    s = np.arange(2 * LANES)[:, None]
    c = np.arange(LANES)[None, :] + LANES
    return np.stack([((s > c - w) & (s <= c)).astype(np.float32) for w in POOL_WINDOWS])


def _layer0(x, w_inT, pool_wT, ps, mk, mvT, w_outT, lng, lnb, tm):
    B, S, _ = x.shape
    band = jnp.asarray(_pool_bands(), BF)
    return pl.pallas_call(
        _layer0_kernel,
        grid=(B, S // tm),
        in_specs=[
            pl.BlockSpec((1, tm, D_MODEL), lambda b, t: (b, t, 0)),
            _const_spec(w_inT.shape),
            _const_spec(band.shape),
            _const_spec(pool_wT.shape),
            _const_spec(ps.shape),
            pl.BlockSpec((1, 1, MEM_HEADS, N_MEM, MEM_DH), lambda b, t: (0, b, 0, 0, 0)),
            pl.BlockSpec((1, 1, MEM_HEADS, MEM_DH, N_MEM), lambda b, t: (0, b, 0, 0, 0)),
            _const_spec(w_outT.shape),
            _const_spec(lng.shape),
            _const_spec(lnb.shape),
        ],
        out_specs=pl.BlockSpec((1, tm, D_MODEL), lambda b, t: (b, t, 0)),
        out_shape=jax.ShapeDtypeStruct((B, S, D_MODEL), F32),
        scratch_shapes=[
            pltpu.VMEM((D_MAIN, LANES), BF),
            pltpu.VMEM((D_MAIN, tm), F32),
            pltpu.VMEM((D_MIX, tm), BF),
        ],
        compiler_params=pltpu.CompilerParams(
            dimension_semantics=("arbitrary", "arbitrary"), vmem_limit_bytes=VMEM_LIMIT),
        name="layer0",
    )(x, w_inT, band, pool_wT, ps, mk, mvT, w_outT, lng, lnb)


def _shared_kv_kernel(x_ref, wf3_ref, bias3_ref, wk_ref, wqT_ref, wvT_ref, hsum_ref,
                      k_ref, dk_ref, qT_ref, vT_ref, cumT_ref, rb_ref, kst_ref, qst_ref, carry_ref):
    t = pl.program_id(1)

    @pl.when(t == 0)
    def _():
        carry_ref[...] = jnp.zeros_like(carry_ref)

    xb = x_ref[0].astype(BF)
    tm = xb.shape[0]
    lane = lax.broadcasted_iota(jnp.int32, (tm, LANES), 1)
    f3 = _mm(xb, wf3_ref[...]) + bias3_ref[...]
    lf = jnp.where(lane < 3 * FOX_HEADS, jax.nn.log_sigmoid(f3), 0.0)

    cb = min(256, tm)
    ri = lax.broadcasted_iota(jnp.int32, (cb, cb), 0)
    ci = lax.broadcasted_iota(jnp.int32, (cb, cb), 1)
    tri = jnp.where(ci <= ri, 1.0, 0.0).astype(BF)
    off = jnp.zeros((1, LANES), F32)
    blocks = []
    for i in range(tm // cb):
        hi, mid, lo = _split3(lf[i * cb:(i + 1) * cb])
        c = _mm(tri, hi) + _mm(tri, mid) + _mm(tri, lo) + off
        off = c[cb - 1:cb, :]
        blocks.append(c)
    c_loc = jnp.concatenate(blocks, axis=0) * LOG2E
    cum = c_loc + carry_ref[...]
    carry_ref[...] = cum[tm - 1:tm, :]

    hi, mid, lo = (v.astype(F32) for v in _split3(c_loc[0:1, :] - c_loc))
    dk_ref[0, 0] = jnp.where(lane < FOX_HEADS, hi,
                             jnp.where(lane < 2 * FOX_HEADS, mid,
                                       jnp.where(lane < 3 * FOX_HEADS, lo, 0.0))).astype(BF)
    kb = _mm(xb, wk_ref[...]).astype(BF)
    for g in range(HEAD_PAIRS):
        k_ref[0, g, 0] = kb[:, g * LANES:(g + 1) * LANES]

    qT = (_nt(wqT_ref[...], xb) * (FOX_DH ** -0.5 * LOG2E)).astype(BF)
    vT = _nt(wvT_ref[...], xb).astype(BF)

    kf32 = kb.astype(F32)
    kn2 = jnp.max(_mm((kf32 * kf32).astype(BF), hsum_ref[...]), axis=0, keepdims=True)
    cmax = jnp.max(cum, axis=0, keepdims=True)
    cmin = jnp.min(cum, axis=0, keepdims=True)
    srow = lax.broadcasted_iota(jnp.int32, (8, LANES), 0)
    kst_ref[0, 0] = jnp.where(srow == 0, cmax, jnp.where(srow == 1, cmin, jnp.where(srow == 2, kn2, 0.0)))
    qf32 = qT.astype(F32)
    qn2 = jnp.sum((qf32 * qf32).reshape(FOX_HEADS, FOX_DH, tm), axis=1)
    qst_ref[0, 0] = jnp.broadcast_to(jnp.max(qn2, axis=1, keepdims=True), (FOX_HEADS, LANES))
    ones_rows = jnp.where(lax.broadcasted_iota(jnp.int32, (BF16_ROWS, tm), 0) == 0, 1.0, 0.0).astype(BF)
    for h in range(FOX_HEADS):
        qT_ref[0, h] = qT[h * FOX_DH:(h + 1) * FOX_DH]
        vT_ref[0, h, 0, 0:FOX_DH, :] = vT[h * FOX_DH:(h + 1) * FOX_DH]
        vT_ref[0, h, 0, FOX_DH:, :] = ones_rows

    cT = cum.T
    for h in range(FOX_HEADS):
        cumT_ref[0, h] = cT[h:h + 1]
        rb_ref[0, h] = jnp.broadcast_to(cT[h:h + 1, 0:1], (1, LANES))


def _head_sum_matrix():
    g = np.zeros((D_MAIN, LANES), np.float32)
    for h in range(FOX_HEADS):
        g[h * FOX_DH:(h + 1) * FOX_DH, h] = 1.0
    return g


def _shared_kv(x1, wf3, bias3, wk, wqT, wvT, tm):
    B, S, _ = x1.shape
    nt = S // tm
    hsum = jnp.asarray(_head_sum_matrix(), BF)
    return pl.pallas_call(
        _shared_kv_kernel,
        grid=(B, nt),
        in_specs=[
            pl.BlockSpec((1, tm, D_MODEL), lambda b, t: (b, t, 0)),
            _const_spec(wf3.shape),
            _const_spec(bias3.shape),
            _const_spec(wk.shape),
            _const_spec(wqT.shape),
            _const_spec(wvT.shape),
            _const_spec(hsum.shape),
        ],
        out_specs=[
            pl.BlockSpec((1, HEAD_PAIRS, 1, tm, LANES), lambda b, t: (b, 0, t, 0, 0)),
            pl.BlockSpec((1, 1, tm, LANES), lambda b, t: (b, t, 0, 0)),
            pl.BlockSpec((1, FOX_HEADS, FOX_DH, tm), lambda b, t: (b, 0, 0, t)),
            pl.BlockSpec((1, FOX_HEADS, 1, VROWS, tm), lambda b, t: (b, 0, t, 0, 0)),
            pl.BlockSpec((1, FOX_HEADS, 1, tm), lambda b, t: (b, 0, 0, t)),
            pl.BlockSpec((1, FOX_HEADS, 1, LANES), lambda b, t: (b, t, 0, 0)),
            pl.BlockSpec((1, 1, 8, LANES), lambda b, t: (b, t, 0, 0)),
            pl.BlockSpec((1, 1, FOX_HEADS, LANES), lambda b, t: (b, t, 0, 0)),
        ],
        out_shape=[
            jax.ShapeDtypeStruct((B, HEAD_PAIRS, nt, tm, LANES), BF),
            jax.ShapeDtypeStruct((B, nt, tm, LANES), BF),
            jax.ShapeDtypeStruct((B, FOX_HEADS, FOX_DH, S), BF),
            jax.ShapeDtypeStruct((B, FOX_HEADS, nt, VROWS, tm), BF),
            jax.ShapeDtypeStruct((B, FOX_HEADS, 1, S), F32),
            jax.ShapeDtypeStruct((B, nt * FOX_HEADS, 1, LANES), F32),
            jax.ShapeDtypeStruct((B, nt, 8, LANES), F32),
            jax.ShapeDtypeStruct((B, nt, FOX_HEADS, LANES), F32),
        ],
        scratch_shapes=[pltpu.VMEM((1, LANES), F32)],
        compiler_params=pltpu.CompilerParams(
            dimension_semantics=("arbitrary", "arbitrary"), vmem_limit_bytes=VMEM_LIMIT),
        name="shared_kv",
    )(x1, wf3, bias3, wk, wqT, wvT, hsum)


def _attn_kernel(n1_ref, cnt_ref, fx_ref, ul_ref, qT_ref, cumT_ref, rb_ref, ref_ref, k_ref, dk_ref, vT_ref,
                 y_ref, qa_ref, o_ref, m_ref, s_ref, pa_ref, pb_ref, pc_ref):
    b = pl.program_id(0)
    i = pl.program_id(1)
    step = b * pl.num_programs(1) + i
    n1 = n1_ref[step]
    fixed_ok = fx_ref[step]
    tq = qa_ref.shape[2]
    tk = k_ref.shape[3]

    rowi = lax.broadcasted_iota(jnp.int32, (LANES, tq), 0)
    zeros = jnp.zeros((FOX_DH, tq), BF)
    for h in range(FOX_HEADS):
        lo = (h % 2) * FOX_DH
        qa_ref[h, lo:lo + FOX_DH, :] = qT_ref[0, h]
        qa_ref[h, FOX_DH - lo:2 * FOX_DH - lo, :] = zeros
        piece_rows = (rowi == h) | (rowi == FOX_HEADS + h) | (rowi == 2 * FOX_HEADS + h)
        qa_ref[h, LANES:, :] = jnp.where(piece_rows, 1.0, 0.0).astype(BF)
    m_ref[...] = jnp.full(m_ref.shape, NEG, F32)
    o_ref[...] = jnp.zeros_like(o_ref)

    def logits(h, j, m_prev, slot, masked):
        pair_idx = lax.shift_right_logical(jnp.int32(h), 1)
        lhs = jnp.concatenate([k_ref[0, pair_idx, j], dk_ref[0, j]], axis=1)
        s = _mm(lhs, qa_ref[h])
        if masked:
            r = lax.broadcasted_iota(jnp.int32, (tk, tq), 0)
            c = lax.broadcasted_iota(jnp.int32, (tk, tq), 1)
            s = jnp.where(r <= c, s, NEG)
        a = cumT_ref[0, h] - jnp.tile(rb_ref[0, j * FOX_HEADS + h], (1, tq // LANES))
        m_cur = jnp.maximum(m_prev, jnp.max(s, axis=0, keepdims=True) + a)
        s_ref[slot] = s
        return m_cur, m_cur - a, jnp.exp2(m_prev - m_cur)

    def probs(slot, shift):
        (pa_ref, pb_ref)[slot][...] = jnp.exp2(s_ref[slot] - shift).astype(BF)

    def values(h, j, slot, al):
        o_ref[h] = al * o_ref[h] + _mm(vT_ref[0, h, j], (pa_ref, pb_ref)[slot][...])

    def sweep(unit, n, masked):
        (h0, j0), (h1, j1), (h2, j2), (h3, j3) = unit(0), unit(1), unit(2), unit(3)
        m, sh_e, a_e = logits(h0, j0, m_ref[h0], 0, masked)
        m_ref[h0] = m
        m, sh_o, a_o = logits(h1, j1, m_ref[h1], 1, masked)
        m_ref[h1] = m
        probs(0, sh_e)
        a_c = a_e
        m, sh_e, a_e = logits(h2, j2, m_ref[h2], 0, masked)
        m_ref[h2] = m
        probs(1, sh_o)
        values(h0, j0, 0, a_c)
        a_c = a_o
        m, sh_o, a_o = logits(h3, j3, m_ref[h3], 1, masked)
        m_ref[h3] = m

        def pair(g, carry):
            sh_e, a_e, sh_o, a_o, a_c = carry
            t = 2 * g
            (hp, jp), (he, je) = unit(t - 1), unit(t)
            (he2, je2), (ho2, jo2) = unit(t + 2), unit(t + 3)
            mp_e, mp_o = m_ref[he2], m_ref[ho2]
            probs(0, sh_e)
            m_e2, sh_e2, a_e2 = logits(he2, je2, mp_e, 0, masked)
            values(hp, jp, 1, a_c)
            probs(1, sh_o)
            mp_o = jnp.where(ho2 == he2, m_e2, mp_o)
            m_o2, sh_o2, a_o2 = logits(ho2, jo2, mp_o, 1, masked)
            values(he, je, 0, a_e)
            m_ref[he2] = m_e2
            m_ref[ho2] = m_o2
            return sh_e2, a_e2, sh_o2, a_o2, a_o

        sh_e, a_e, sh_o, a_o, a_c = lax.fori_loop(1, n // 2 - 1, pair, (sh_e, a_e, sh_o, a_o, a_c))
        (hp, jp), (he, je), (ho, jo) = unit(n - 3), unit(n - 2), unit(n - 1)
        probs(0, sh_e)
        values(hp, jp, 1, a_c)
        probs(1, sh_o)
        values(he, je, 0, a_e)
        values(ho, jo, 1, a_o)

    half = tk // 2

    def fixed_probs(t, unit, cnt, masked, p_buf):
        h, j = unit(t)
        pair_idx = lax.shift_right_logical(jnp.int32(h), 1)
        base = rb_ref[0, j * FOX_HEADS + h] + ref_ref[0, h]
        shift = jnp.tile(base, (1, tq // LANES)) - cumT_ref[0, h]
        if cnt is not None:
            shift = shift + jnp.where(t >= cnt, -NEG, 0.0)
        if not masked:
            lhs = jnp.concatenate([k_ref[0, pair_idx, j], dk_ref[0, j]], axis=1)
            p_buf[...] = jnp.exp2(_mm(lhs, qa_ref[h]) - shift).astype(BF)
            return
        top = jnp.concatenate([k_ref[0, pair_idx, j, 0:half, :], dk_ref[0, j, 0:half, :]], axis=1)
        bot = jnp.concatenate([k_ref[0, pair_idx, j, half:, :], dk_ref[0, j, half:, :]], axis=1)
        s_top = _mm(top, qa_ref[h])
        s_bot = _mm(bot, qa_ref[h, :, half:])
        keep_top = (lax.broadcasted_iota(jnp.int32, (half, tq), 0)
                    <= lax.broadcasted_iota(jnp.int32, (half, tq), 1))
        keep_bot = (lax.broadcasted_iota(jnp.int32, (half, tq - half), 0)
                    <= lax.broadcasted_iota(jnp.int32, (half, tq - half), 1))
        shift_bot = jnp.tile(base, (1, (tq - half) // LANES)) - cumT_ref[0, h, :, half:]
        p_buf[0:half, :] = jnp.exp2(jnp.where(keep_top, s_top, NEG) - shift).astype(BF)
        p_buf[half:, half:] = jnp.exp2(jnp.where(keep_bot, s_bot, NEG) - shift_bot).astype(BF)

    def fixed_values(t, unit, p_buf, masked):
        h, j = unit(t)
        if not masked:
            o_ref[h] += _mm(vT_ref[0, h, j], p_buf[...])
            return
        o_ref[h] += _mm(vT_ref[0, h, j, :, 0:half], p_buf[0:half, :])
        o_ref[h, :, half:] += _mm(vT_ref[0, h, j, :, half:], p_buf[half:, half:])

    def fixed_sweep(unit, n, cnt, masked):
        bufs = (pa_ref, pb_ref, pc_ref)
        fixed_probs(0, unit, cnt, masked, bufs[0])
        fixed_probs(1, unit, cnt, masked, bufs[1])

        def tick(t, k, last_ab):
            fixed_values(t, unit, bufs[k], masked)
            if last_ab is None or t + 2 <= last_ab:
                fixed_probs(t + 2, unit, cnt, masked, bufs[(k + 2) % 3])

        def body(g, carry):
            for k in range(BODY_TICKS):
                tick(BODY_TICKS * g + k, k % 3, None)
            return carry

        n_loop = (n - 2) // BODY_TICKS
        lax.fori_loop(0, n_loop, body, 0)
        if isinstance(n, int):
            for t in range(BODY_TICKS * n_loop, n):
                tick(t, t % 3, n - 1)
        else:
            fixed_values(n - 2, unit, bufs[0], masked)
            fixed_values(n - 1, unit, bufs[1], masked)

    def listed_unit(t):
        code = ul_ref[0, 0, t]
        return code & (FOX_HEADS - 1), lax.shift_right_logical(code, 4)

    diag_unit = lambda t: (t, i)

    @pl.when(jnp.logical_and(fixed_ok == 1, n1 > 0))
    def _():
        cnt = cnt_ref[step]
        fixed_sweep(listed_unit, cnt + (2 - cnt) % BODY_TICKS, cnt, False)

    @pl.when(fixed_ok == 1)
    def _():
        fixed_sweep(diag_unit, FOX_HEADS, None, True)

    @pl.when(jnp.logical_and(fixed_ok == 0, n1 > 0))
    def _():
        sweep(listed_unit, n1, False)

    @pl.when(fixed_ok == 0)
    def _():
        sweep(diag_unit, FOX_HEADS, True)

    for h in range(FOX_HEADS):
        y_ref[0, h] = (o_ref[h, 0:FOX_DH, :] / o_ref[h, FOX_DH:FOX_DH + 1, :]).astype(BF)


def _attn_schedule(kst, qst):
    B, nt = kst.shape[0], kst.shape[1]
    cmax = kst[:, :, 0, :FOX_HEADS]
    cmin = kst[:, :, 1, :FOX_HEADS]
    kn = jnp.sqrt(kst[:, :, 2, :FOX_HEADS]) * NORM_SLACK
    qn = jnp.sqrt(qst[:, :, :, 0]) * NORM_SLACK
    upper = qn[:, :, None] * kn[:, None] + cmax[:, :, None] - cmin[:, None]
    lower = -qn * kn
    need = upper > lower[:, :, None] - UNDERFLOW_MARGIN
    ti = jnp.arange(nt)
    valid = (ti[None, :] < ti[:, None])[None, :, :, None]
    rank = jnp.where(valid & need, 0, jnp.where(valid, 1, 2)).reshape(B, nt, nt * FOX_HEADS)
    units = jnp.argsort(rank, axis=-1, stable=True).astype(jnp.int32)
    cnt = jnp.sum(rank == 0, axis=-1).astype(jnp.int32)
    n1 = jnp.where(cnt > 0, jnp.maximum(cnt + (cnt & 1), 4), 0)

    top = qn * lax.cummax(kn, axis=1)
    ref = 0.5 * (top + lower) - REF_HEADROOM
    fixed_ok = jnp.all(top - lower <= REF_SPREAD_LIMIT, axis=-1).astype(jnp.int32)
    ref = jnp.broadcast_to(ref.reshape(B * nt, FOX_HEADS, 1, 1), (B * nt, FOX_HEADS, 1, LANES))
    return (n1.reshape(-1), cnt.reshape(-1), fixed_ok.reshape(-1),
            units.reshape(B * nt, 1, nt * FOX_HEADS), ref)


def _attention(qT, cumT, rb, kst, qst, k, dk, vT, tq):
    B, H, dh, S = qT.shape
    nt = S // tq
    n1, cnt, fixed_ok, units, ref = _attn_schedule(kst, qst)
    resident = dict(pipeline_mode=pl.Buffered(1))
    grid_spec = pltpu.PrefetchScalarGridSpec(
        num_scalar_prefetch=3,
        grid=(B, nt),
        in_specs=[
            pl.BlockSpec((1, 1, nt * H), lambda b, i, *_: (b * nt + i, 0, 0), memory_space=pltpu.SMEM),
            pl.BlockSpec((1, H, dh, tq), lambda b, i, *_: (b, 0, 0, i)),
            pl.BlockSpec((1, H, 1, tq), lambda b, i, *_: (b, 0, 0, i)),
            pl.BlockSpec((1, nt * H, 1, LANES), lambda b, i, *_: (b, 0, 0, 0)),
            pl.BlockSpec((1, H, 1, LANES), lambda b, i, *_: (b * nt + i, 0, 0, 0)),
            pl.BlockSpec((1, HEAD_PAIRS, nt, tq, LANES), lambda b, i, *_: (b, 0, 0, 0, 0), **resident),
            pl.BlockSpec((1, nt, tq, LANES), lambda b, i, *_: (b, 0, 0, 0), **resident),
            pl.BlockSpec((1, H, nt, VROWS, tq), lambda b, i, *_: (b, 0, 0, 0, 0), **resident),
        ],
        out_specs=pl.BlockSpec((1, H, dh, tq), lambda b, i, *_: (b, 0, 0, i)),
        scratch_shapes=[
            pltpu.VMEM((H, QA_ROWS, tq), BF),
            pltpu.VMEM((H, VROWS, tq), F32),
            pltpu.VMEM((H, 1, tq), F32),
            pltpu.VMEM((2, tq, tq), F32),
            pltpu.VMEM((tq, tq), BF),
            pltpu.VMEM((tq, tq), BF),
            pltpu.VMEM((tq, tq), BF),
        ],
    )
    return pl.pallas_call(
        _attn_kernel,
        grid_spec=grid_spec,
        out_shape=jax.ShapeDtypeStruct((B, H, dh, S), BF),
        compiler_params=pltpu.CompilerParams(
            dimension_semantics=("arbitrary", "arbitrary"), vmem_limit_bytes=VMEM_LIMIT),
        name="fox_attn",
    )(n1, cnt, fixed_ok, units, qT, cumT, rb, ref, k, dk, vT)


def _layer1_kernel(x_ref, yT_ref, w_inT_ref, mk_ref, mvT_ref, w_outT_ref, lng_ref, lnb_ref,
                   o_ref, yg_ref):
    tc = x_ref.shape[1] // TOKEN_SPLIT
    for c in range(TOKEN_SPLIT):
        cols = slice(c * tc, (c + 1) * tc)
        x = x_ref[0, cols, :]
        xb = x.astype(BF)
        qmT = _nt(w_inT_ref[0:D_MEM, :], xb)
        gT = _nt(w_inT_ref[D_MEM:, :], xb)
        _finish_layer(x, yT_ref[0, :, cols].astype(F32), qmT, gT, mk_ref, mvT_ref, w_outT_ref,
                      lng_ref, lnb_ref, o_ref, yg_ref, cols)


def _layer1(x1, yT, w_inT, mk, mvT, w_outT, lng, lnb, tm):
    B, S, _ = x1.shape
    return pl.pallas_call(
        _layer1_kernel,
        grid=(B, S // tm),
        in_specs=[
            pl.BlockSpec((1, tm, D_MODEL), lambda b, t: (b, t, 0)),
            pl.BlockSpec((1, D_MAIN, tm), lambda b, t: (b, 0, t)),
            _const_spec(w_inT.shape),
            pl.BlockSpec((1, 1, MEM_HEADS, N_MEM, MEM_DH), lambda b, t: (1, b, 0, 0, 0)),
            pl.BlockSpec((1, 1, MEM_HEADS, MEM_DH, N_MEM), lambda b, t: (1, b, 0, 0, 0)),
            _const_spec(w_outT.shape),
            _const_spec(lng.shape),
            _const_spec(lnb.shape),
        ],
        out_specs=pl.BlockSpec((1, tm, D_MODEL), lambda b, t: (b, t, 0)),
        out_shape=jax.ShapeDtypeStruct((B, S, D_MODEL), F32),
        scratch_shapes=[pltpu.VMEM((D_MIX, tm), BF)],
        compiler_params=pltpu.CompilerParams(
            dimension_semantics=("arbitrary", "arbitrary"), vmem_limit_bytes=VMEM_LIMIT),
        name="layer1",
    )(x1, yT, w_inT, mk, mvT, w_outT, lng, lnb)


def kernel(x, mem, w_in, w_mem_kv, w_out, ln_g, ln_b, pool_w, pool_scale, w_kv_shared, b_forget):
    B, S, _ = x.shape
    tm = min(512, S)

    w_inT = jnp.swapaxes(w_in, 1, 2).astype(BF)
    w_outT = jnp.swapaxes(w_out, 1, 2).astype(BF)
    pool_wT = jnp.swapaxes(pool_w[0], 1, 2).astype(BF)
    ps = pool_scale[0].reshape(D_MAIN, 1)
    lng = ln_g.reshape(DEPTH, 1, D_MODEL)
    lnb = ln_b.reshape(DEPTH, 1, D_MODEL)
    wk = w_kv_shared[:, :D_MAIN].astype(BF)
    wvT = w_kv_shared[:, D_MAIN:2 * D_MAIN].T.astype(BF)
    wf = w_kv_shared[:, 2 * D_MAIN:]
    wf3 = jnp.concatenate([wf, wf, wf, jnp.zeros((D_MODEL, LANES - 3 * FOX_HEADS), F32)], axis=1).astype(BF)
    bias3 = jnp.concatenate([b_forget, b_forget, b_forget,
                             jnp.zeros((LANES - 3 * FOX_HEADS,), F32)]).reshape(1, LANES)

    mk, mvT = _mem_kv(mem, w_mem_kv.astype(BF))
    tl = 2 * tm if S % (2 * tm) == 0 else tm
    x1 = _layer0(x, w_inT[0], pool_wT, ps, mk, mvT, w_outT[0], lng[0], lnb[0], tl)
    k, dk, qT, vT, cumT, rb, kst, qst = _shared_kv(x1, wf3, bias3, wk, w_inT[1, :D_MAIN], wvT, tm)
    yT = _attention(qT, cumT, rb, kst, qst, k, dk, vT, tm)
    return _layer1(x1, yT.reshape(B, D_MAIN, S), w_inT[1, D_MAIN:], mk, mvT, w_outT[1],
                   lng[1], lnb[1], tl)
```

```python
import numpy as np
import jax
import jax.numpy as jnp
from jax import lax
from jax.experimental import pallas as pl
from jax.experimental.pallas import tpu as pltpu

D_MODEL = 1024
N_MEM = 256
D_MAIN = 1024
POOL_WINDOWS = (2, 4, 8, 16)
POOL_GROUP = 256
FOX_HEADS = 16
FOX_DH = 64
MEM_HEADS = 4
MEM_DH = 128
D_MEM = MEM_HEADS * MEM_DH
D_MIX = D_MAIN + D_MEM
DEPTH = 2
ALPHA = (2 * DEPTH) ** 0.25
LN_EPS = 1e-5

LANES = 128
BF16_ROWS = 16
HEAD_PAIRS = FOX_HEADS // 2
QA_ROWS = 2 * LANES
VROWS = FOX_DH + BF16_ROWS
NEG = -1e30
LOG2E = 1.4426950408889634
UNDERFLOW_MARGIN = 160.0
NORM_SLACK = 1.02
BODY_TICKS = 6
REF_HEADROOM = 40.0
REF_SPREAD_LIMIT = 100.0
TOKEN_SPLIT = 2
VMEM_LIMIT = 56 * 1024 * 1024

BF = jnp.bfloat16
F32 = jnp.float32


def _mm(a, b):
    return jnp.dot(a, b, preferred_element_type=F32)


def _nt(a, b):
    return lax.dot_general(a, b, (((1,), (1,)), ((), ())), preferred_element_type=F32)


def _split3(v):
    hi = v.astype(BF)
    r1 = v - hi.astype(F32)
    mid = r1.astype(BF)
    lo = (r1 - mid.astype(F32)).astype(BF)
    return hi, mid, lo


def _silu(g):
    hg = 0.5 * g
    return hg + hg * jnp.tanh(hg)


def _const_spec(shape):
    nd = len(shape)
    return pl.BlockSpec(shape, lambda *_: (0,) * nd)


def _layer_spec(arr, layer):
    nd = arr.ndim
    return pl.BlockSpec((1,) + arr.shape[1:], lambda *_: (layer,) + (0,) * (nd - 1))


def _mem_kv_kernel(mem_ref, w_ref, mk_ref, mvT_ref):
    memb = mem_ref[0].astype(BF)
    mkv = _mm(memb, w_ref[0])
    for h in range(MEM_HEADS):
        mk_ref[0, 0, h] = mkv[:, h * MEM_DH:(h + 1) * MEM_DH].astype(BF)
        mv = mkv[:, D_MEM + h * MEM_DH:D_MEM + (h + 1) * MEM_DH]
        mvT_ref[0, 0, h] = mv.T.astype(BF)


def _mem_kv(mem, w_mem_kv_bf):
    B = mem.shape[0]
    L = w_mem_kv_bf.shape[0]
    return pl.pallas_call(
        _mem_kv_kernel,
        grid=(L, B),
        in_specs=[
            pl.BlockSpec((1, N_MEM, D_MODEL), lambda l, b: (b, 0, 0)),
            pl.BlockSpec((1, D_MODEL, 2 * D_MEM), lambda l, b: (l, 0, 0)),
        ],
        out_specs=[
            pl.BlockSpec((1, 1, MEM_HEADS, N_MEM, MEM_DH), lambda l, b: (l, b, 0, 0, 0)),
            pl.BlockSpec((1, 1, MEM_HEADS, MEM_DH, N_MEM), lambda l, b: (l, b, 0, 0, 0)),
        ],
        out_shape=[
            jax.ShapeDtypeStruct((L, B, MEM_HEADS, N_MEM, MEM_DH), BF),
            jax.ShapeDtypeStruct((L, B, MEM_HEADS, MEM_DH, N_MEM), BF),
        ],
        compiler_params=pltpu.CompilerParams(
            dimension_semantics=("arbitrary", "arbitrary"), vmem_limit_bytes=VMEM_LIMIT),
        name="mem_kv",
    )(mem, w_mem_kv_bf)


def _finish_layer(x, mainT, qmT, gT, mk_ref, mvT_ref, w_outT_ref, lng_ref, lnb_ref, o_ref, yg_ref, cols):
    yg_ref = yg_ref.at[:, cols]
    yg_ref[0:D_MAIN, :] = (mainT * _silu(gT[0:D_MAIN])).astype(BF)
    for h in range(MEM_HEADS):
        rows = slice(h * MEM_DH, (h + 1) * MEM_DH)
        qh = qmT[rows].astype(BF)
        lg = _mm(mk_ref[0, 0, h], qh) * (MEM_DH ** -0.5 * LOG2E)
        mx = jnp.max(lg, axis=0, keepdims=True)
        e = jnp.exp2(lg - mx)
        den = jnp.sum(e, axis=0, keepdims=True)
        ym = _mm(mvT_ref[0, 0, h], e.astype(BF)) * (1.0 / den)
        gm = gT[D_MAIN + h * MEM_DH:D_MAIN + (h + 1) * MEM_DH]
        yg_ref[D_MAIN + h * MEM_DH:D_MAIN + (h + 1) * MEM_DH, :] = (ym * _silu(gm)).astype(BF)
    outT = _mm(w_outT_ref[0], yg_ref[...])
    z = ALPHA * x + outT.T
    mu = jnp.mean(z, axis=-1, keepdims=True)
    zc = z - mu
    var = jnp.mean(zc * zc, axis=-1, keepdims=True)
    o_ref[0, cols, :] = zc * lax.rsqrt(var + LN_EPS) * lng_ref[0] + lnb_ref[0]


def _layer0_kernel(x_ref, w_inT_ref, band_ref, pool_wT_ref, ps_ref, mk_ref, mvT_ref, w_outT_ref,
                   lng_ref, lnb_ref, o_ref, tail_ref, main_ref, yg_ref):
    t = pl.program_id(1)
    x = x_ref[0]
    tm = x.shape[0]
    xb = x.astype(BF)
    uT = _nt(w_inT_ref[0, 0:D_MAIN, :], xb)
    qmT = _nt(w_inT_ref[0, D_MAIN:D_MIX, :], xb)
    gT = _nt(w_inT_ref[0, D_MIX:, :], xb)

    @pl.when(t == 0)
    def _():
        tail_ref[...] = jnp.zeros_like(tail_ref)

    tpos = t * tm + lax.broadcasted_iota(jnp.int32, (1, tm), 1)
    ub = uT.astype(BF)
    for g, w in enumerate(POOL_WINDOWS):
        rows = slice(g * POOL_GROUP, (g + 1) * POOL_GROUP)
        ext = jnp.concatenate([tail_ref[rows, :], ub[rows]], axis=1)
        wsum = jnp.concatenate(
            [_mm(ext[:, n * LANES:(n + 2) * LANES], band_ref[g]) for n in range(tm // LANES)], axis=1)
        inv_cnt = 1.0 / jnp.minimum(tpos + 1, w).astype(F32)
        pm = wsum * inv_cnt - uT[rows]
        main_ref[rows, :] = _mm(pool_wT_ref[g], pm.astype(BF)) * ps_ref[rows, :]
    tail_ref[...] = ub[:, tm - LANES:]

    _finish_layer(x, main_ref[...], qmT, gT, mk_ref, mvT_ref, w_outT_ref, lng_ref, lnb_ref,
                  o_ref, yg_ref, slice(0, tm))


def _pool_bands():
    s = np.arange(2 * LANES)[:, None]
    c = np.arange(LANES)[None, :] + LANES
    return np.stack([((s > c - w) & (s <= c)).astype(np.float32) for w in POOL_WINDOWS])


def _layer0(x, w_inT, pool_wT, ps, mk, mvT, w_outT, lng, lnb, tm):
    B, S, _ = x.shape
    band = jnp.asarray(_pool_bands(), BF)
    return pl.pallas_call(
        _layer0_kernel,
        grid=(B, S // tm),
        in_specs=[
            pl.BlockSpec((1, tm, D_MODEL), lambda b, t: (b, t, 0)),
            _layer_spec(w_inT, 0),
            _const_spec(band.shape),
            _const_spec(pool_wT.shape),
            _const_spec(ps.shape),
            pl.BlockSpec((1, 1, MEM_HEADS, N_MEM, MEM_DH), lambda b, t: (0, b, 0, 0, 0)),
            pl.BlockSpec((1, 1, MEM_HEADS, MEM_DH, N_MEM), lambda b, t: (0, b, 0, 0, 0)),
            _layer_spec(w_outT, 0),
            _layer_spec(lng, 0),
            _layer_spec(lnb, 0),
        ],
        out_specs=pl.BlockSpec((1, tm, D_MODEL), lambda b, t: (b, t, 0)),
        out_shape=jax.ShapeDtypeStruct((B, S, D_MODEL), F32),
        scratch_shapes=[
            pltpu.VMEM((D_MAIN, LANES), BF),
            pltpu.VMEM((D_MAIN, tm), F32),
            pltpu.VMEM((D_MIX, tm), BF),
        ],
        compiler_params=pltpu.CompilerParams(
            dimension_semantics=("arbitrary", "arbitrary"), vmem_limit_bytes=VMEM_LIMIT),
        name="layer0",
    )(x, w_inT, band, pool_wT, ps, mk, mvT, w_outT, lng, lnb)


def _shared_kv_kernel(x_ref, wf3_ref, bias3_ref, wk_ref, wqT_ref, wvT_ref, hsum_ref,
                      k_ref, dk_ref, qT_ref, vT_ref, cumT_ref, rb_ref, kst_ref, qst_ref, carry_ref):
    t = pl.program_id(1)

    @pl.when(t == 0)
    def _():
        carry_ref[...] = jnp.zeros_like(carry_ref)

    xb = x_ref[0].astype(BF)
    tm = xb.shape[0]
    lane = lax.broadcasted_iota(jnp.int32, (tm, LANES), 1)
    f3 = _mm(xb, wf3_ref[...]) + bias3_ref[...]
    lf = jnp.where(lane < 3 * FOX_HEADS, jax.nn.log_sigmoid(f3), 0.0)

    cb = min(256, tm)
    ri = lax.broadcasted_iota(jnp.int32, (cb, cb), 0)
    ci = lax.broadcasted_iota(jnp.int32, (cb, cb), 1)
    tri = jnp.where(ci <= ri, 1.0, 0.0).astype(BF)
    off = jnp.zeros((1, LANES), F32)
    blocks = []
    for i in range(tm // cb):
        hi, mid, lo = _split3(lf[i * cb:(i + 1) * cb])
        c = _mm(tri, hi) + _mm(tri, mid) + _mm(tri, lo) + off
        off = c[cb - 1:cb, :]
        blocks.append(c)
    c_loc = jnp.concatenate(blocks, axis=0) * LOG2E
    cum = c_loc + carry_ref[...]
    carry_ref[...] = cum[tm - 1:tm, :]

    hi, mid, lo = (v.astype(F32) for v in _split3(c_loc[0:1, :] - c_loc))
    dk_ref[0, 0] = jnp.where(lane < FOX_HEADS, hi,
                             jnp.where(lane < 2 * FOX_HEADS, mid,
                                       jnp.where(lane < 3 * FOX_HEADS, lo, 0.0))).astype(BF)
    kb = _mm(xb, wk_ref[...]).astype(BF)
    for g in range(HEAD_PAIRS):
        k_ref[0, g, 0] = kb[:, g * LANES:(g + 1) * LANES]

    qT = (_nt(wqT_ref[0], xb) * (FOX_DH ** -0.5 * LOG2E)).astype(BF)
    vT = _nt(wvT_ref[...], xb).astype(BF)

    kf32 = kb.astype(F32)
    kn2 = jnp.max(_mm((kf32 * kf32).astype(BF), hsum_ref[...]), axis=0, keepdims=True)
    cmax = jnp.max(cum, axis=0, keepdims=True)
    cmin = jnp.min(cum, axis=0, keepdims=True)
    srow = lax.broadcasted_iota(jnp.int32, (8, LANES), 0)
    kst_ref[0, 0] = jnp.where(srow == 0, cmax, jnp.where(srow == 1, cmin, jnp.where(srow == 2, kn2, 0.0)))
    qf32 = qT.astype(F32)
    qn2 = jnp.sum((qf32 * qf32).reshape(FOX_HEADS, FOX_DH, tm), axis=1)
    qst_ref[0, 0] = jnp.broadcast_to(jnp.max(qn2, axis=1, keepdims=True), (FOX_HEADS, LANES))
    ones_rows = jnp.where(lax.broadcasted_iota(jnp.int32, (BF16_ROWS, tm), 0) == 0, 1.0, 0.0).astype(BF)
    for h in range(FOX_HEADS):
        qT_ref[0, h] = qT[h * FOX_DH:(h + 1) * FOX_DH]
        vT_ref[0, h, 0, 0:FOX_DH, :] = vT[h * FOX_DH:(h + 1) * FOX_DH]
        vT_ref[0, h, 0, FOX_DH:, :] = ones_rows

    cT = cum.T
    for h in range(FOX_HEADS):
        cumT_ref[0, h] = cT[h:h + 1]
        rb_ref[0, h] = jnp.broadcast_to(cT[h:h + 1, 0:1], (1, LANES))


def _head_sum_matrix():
    g = np.zeros((D_MAIN, LANES), np.float32)
    for h in range(FOX_HEADS):
        g[h * FOX_DH:(h + 1) * FOX_DH, h] = 1.0
    return g


def _shared_kv(x1, wf3, bias3, wk, wqT, wvT, tm):
    B, S, _ = x1.shape
    nt = S // tm
    hsum = jnp.asarray(_head_sum_matrix(), BF)
    return pl.pallas_call(
        _shared_kv_kernel,
        grid=(B, nt),
        in_specs=[
            pl.BlockSpec((1, tm, D_MODEL), lambda b, t: (b, t, 0)),
            _const_spec(wf3.shape),
            _const_spec(bias3.shape),
            _const_spec(wk.shape),
            pl.BlockSpec((1, D_MAIN, D_MODEL), lambda b, t: (1, 0, 0)),
            _const_spec(wvT.shape),
            _const_spec(hsum.shape),
        ],
        out_specs=[
            pl.BlockSpec((1, HEAD_PAIRS, 1, tm, LANES), lambda b, t: (b, 0, t, 0, 0)),
            pl.BlockSpec((1, 1, tm, LANES), lambda b, t: (b, t, 0, 0)),
            pl.BlockSpec((1, FOX_HEADS, FOX_DH, tm), lambda b, t: (b, 0, 0, t)),
            pl.BlockSpec((1, FOX_HEADS, 1, VROWS, tm), lambda b, t: (b, 0, t, 0, 0)),
            pl.BlockSpec((1, FOX_HEADS, 1, tm), lambda b, t: (b, 0, 0, t)),
            pl.BlockSpec((1, FOX_HEADS, 1, LANES), lambda b, t: (b, t, 0, 0)),
            pl.BlockSpec((1, 1, 8, LANES), lambda b, t: (b, t, 0, 0)),
            pl.BlockSpec((1, 1, FOX_HEADS, LANES), lambda b, t: (b, t, 0, 0)),
        ],
        out_shape=[
            jax.ShapeDtypeStruct((B, HEAD_PAIRS, nt, tm, LANES), BF),
            jax.ShapeDtypeStruct((B, nt, tm, LANES), BF),
            jax.ShapeDtypeStruct((B, FOX_HEADS, FOX_DH, S), BF),
            jax.ShapeDtypeStruct((B, FOX_HEADS, nt, VROWS, tm), BF),
            jax.ShapeDtypeStruct((B, FOX_HEADS, 1, S), F32),
            jax.ShapeDtypeStruct((B, nt * FOX_HEADS, 1, LANES), F32),
            jax.ShapeDtypeStruct((B, nt, 8, LANES), F32),
            jax.ShapeDtypeStruct((B, nt, FOX_HEADS, LANES), F32),
        ],
        scratch_shapes=[pltpu.VMEM((1, LANES), F32)],
        compiler_params=pltpu.CompilerParams(
            dimension_semantics=("arbitrary", "arbitrary"), vmem_limit_bytes=VMEM_LIMIT),
        name="shared_kv",
    )(x1, wf3, bias3, wk, wqT, wvT, hsum)


def _attn_kernel(n1_ref, cnt_ref, fx_ref, ul_ref, qT_ref, cumT_ref, rb_ref, ref_ref, k_ref, dk_ref, vT_ref,
                 y_ref, qa_ref, o_ref, m_ref, s_ref, pa_ref, pb_ref, pc_ref):
    b = pl.program_id(0)
    i = pl.program_id(1)
    step = b * pl.num_programs(1) + i
    n1 = n1_ref[step]
    fixed_ok = fx_ref[step]
    tq = qa_ref.shape[2]
    tk = k_ref.shape[3]

    @pl.when(i == 0)
    def _():
        rowi = lax.broadcasted_iota(jnp.int32, (LANES, tq), 0)
        zeros = jnp.zeros((FOX_DH, tq), BF)
        for h in range(FOX_HEADS):
            lo = (h % 2) * FOX_DH
            qa_ref[h, FOX_DH - lo:2 * FOX_DH - lo, :] = zeros
            piece_rows = (rowi == h) | (rowi == FOX_HEADS + h) | (rowi == 2 * FOX_HEADS + h)
            qa_ref[h, LANES:, :] = jnp.where(piece_rows, 1.0, 0.0).astype(BF)

    for h in range(FOX_HEADS):
        lo = (h % 2) * FOX_DH
        qa_ref[h, lo:lo + FOX_DH, :] = qT_ref[0, h]
    m_ref[...] = jnp.full(m_ref.shape, NEG, F32)
    o_ref[...] = jnp.zeros_like(o_ref)

    def logits(h, j, m_prev, slot, masked):
        pair_idx = lax.shift_right_logical(jnp.int32(h), 1)
        lhs = jnp.concatenate([k_ref[0, pair_idx, j], dk_ref[0, j]], axis=1)
        s = _mm(lhs, qa_ref[h])
        if masked:
            r = lax.broadcasted_iota(jnp.int32, (tk, tq), 0)
            c = lax.broadcasted_iota(jnp.int32, (tk, tq), 1)
            s = jnp.where(r <= c, s, NEG)
        a = cumT_ref[0, h] - jnp.tile(rb_ref[0, j * FOX_HEADS + h], (1, tq // LANES))
        m_cur = jnp.maximum(m_prev, jnp.max(s, axis=0, keepdims=True) + a)
        s_ref[slot] = s
        return m_cur, m_cur - a, jnp.exp2(m_prev - m_cur)

    def probs(slot, shift):
        (pa_ref, pb_ref)[slot][...] = jnp.exp2(s_ref[slot] - shift).astype(BF)

    def values(h, j, slot, al):
        o_ref[h] = al * o_ref[h] + _mm(vT_ref[0, h, j], (pa_ref, pb_ref)[slot][...])

    def sweep(unit, n, masked):
        (h0, j0), (h1, j1), (h2, j2), (h3, j3) = unit(0), unit(1), unit(2), unit(3)
        m, sh_e, a_e = logits(h0, j0, m_ref[h0], 0, masked)
        m_ref[h0] = m
        m, sh_o, a_o = logits(h1, j1, m_ref[h1], 1, masked)
        m_ref[h1] = m
        probs(0, sh_e)
        a_c = a_e
        m, sh_e, a_e = logits(h2, j2, m_ref[h2], 0, masked)
        m_ref[h2] = m
        probs(1, sh_o)
        values(h0, j0, 0, a_c)
        a_c = a_o
        m, sh_o, a_o = logits(h3, j3, m_ref[h3], 1, masked)
        m_ref[h3] = m

        def pair(g, carry):
            sh_e, a_e, sh_o, a_o, a_c = carry
            t = 2 * g
            (hp, jp), (he, je) = unit(t - 1), unit(t)
            (he2, je2), (ho2, jo2) = unit(t + 2), unit(t + 3)
            mp_e, mp_o = m_ref[he2], m_ref[ho2]
            probs(0, sh_e)
            m_e2, sh_e2, a_e2 = logits(he2, je2, mp_e, 0, masked)
            values(hp, jp, 1, a_c)
            probs(1, sh_o)
            mp_o = jnp.where(ho2 == he2, m_e2, mp_o)
            m_o2, sh_o2, a_o2 = logits(ho2, jo2, mp_o, 1, masked)
            values(he, je, 0, a_e)
            m_ref[he2] = m_e2
            m_ref[ho2] = m_o2
            return sh_e2, a_e2, sh_o2, a_o2, a_o

        sh_e, a_e, sh_o, a_o, a_c = lax.fori_loop(1, n // 2 - 1, pair, (sh_e, a_e, sh_o, a_o, a_c))
        (hp, jp), (he, je), (ho, jo) = unit(n - 3), unit(n - 2), unit(n - 1)
        probs(0, sh_e)
        values(hp, jp, 1, a_c)
        probs(1, sh_o)
        values(he, je, 0, a_e)
        values(ho, jo, 1, a_o)

    half = tk // 2

    def fixed_probs(t, unit, cnt, masked, p_buf):
        h, j = unit(t)
        pair_idx = lax.shift_right_logical(jnp.int32(h), 1)
        base = rb_ref[0, j * FOX_HEADS + h] + ref_ref[0, h]
        shift = jnp.tile(base, (1, tq // LANES)) - cumT_ref[0, h]
        if cnt is not None:
            shift = shift + jnp.where(t >= cnt, -NEG, 0.0)
        if not masked:
            lhs = jnp.concatenate([k_ref[0, pair_idx, j], dk_ref[0, j]], axis=1)
            p_buf[...] = jnp.exp2(_mm(lhs, qa_ref[h]) - shift).astype(BF)
            return
        top = jnp.concatenate([k_ref[0, pair_idx, j, 0:half, :], dk_ref[0, j, 0:half, :]], axis=1)
        bot = jnp.concatenate([k_ref[0, pair_idx, j, half:, :], dk_ref[0, j, half:, :]], axis=1)
        s_top = _mm(top, qa_ref[h])
        s_bot = _mm(bot, qa_ref[h, :, half:])
        keep_top = (lax.broadcasted_iota(jnp.int32, (half, tq), 0)
                    <= lax.broadcasted_iota(jnp.int32, (half, tq), 1))
        keep_bot = (lax.broadcasted_iota(jnp.int32, (half, tq - half), 0)
                    <= lax.broadcasted_iota(jnp.int32, (half, tq - half), 1))
        shift_bot = jnp.tile(base, (1, (tq - half) // LANES)) - cumT_ref[0, h, :, half:]
        p_buf[0:half, :] = jnp.exp2(jnp.where(keep_top, s_top, NEG) - shift).astype(BF)
        p_buf[half:, half:] = jnp.exp2(jnp.where(keep_bot, s_bot, NEG) - shift_bot).astype(BF)

    def fixed_values(t, unit, p_buf, masked):
        h, j = unit(t)
        if not masked:
            o_ref[h] += _mm(vT_ref[0, h, j], p_buf[...])
            return
        o_ref[h] += _mm(vT_ref[0, h, j, :, 0:half], p_buf[0:half, :])
        o_ref[h, :, half:] += _mm(vT_ref[0, h, j, :, half:], p_buf[half:, half:])

    def fixed_sweep(unit, n, cnt, masked):
        bufs = (pa_ref, pb_ref, pc_ref)
        fixed_probs(0, unit, cnt, masked, bufs[0])
        fixed_probs(1, unit, cnt, masked, bufs[1])

        def tick(t, k, last_ab):
            fixed_values(t, unit, bufs[k], masked)
            if last_ab is None or t + 2 <= last_ab:
                fixed_probs(t + 2, unit, cnt, masked, bufs[(k + 2) % 3])

        def body(g, carry):
            for k in range(BODY_TICKS):
                tick(BODY_TICKS * g + k, k % 3, None)
            return carry

        n_loop = (n - 2) // BODY_TICKS
        lax.fori_loop(0, n_loop, body, 0)
        if isinstance(n, int):
            for t in range(BODY_TICKS * n_loop, n):
                tick(t, t % 3, n - 1)
        else:
            fixed_values(n - 2, unit, bufs[0], masked)
            fixed_values(n - 1, unit, bufs[1], masked)

    def listed_unit(t):
        code = ul_ref[0, 0, t]
        return code & (FOX_HEADS - 1), lax.shift_right_logical(code, 4)

    diag_unit = lambda t: (t, i)

    @pl.when(jnp.logical_and(fixed_ok == 1, n1 > 0))
    def _():
        cnt = cnt_ref[step]
        fixed_sweep(listed_unit, cnt + (2 - cnt) % BODY_TICKS, cnt, False)

    @pl.when(fixed_ok == 1)
    def _():
        fixed_sweep(diag_unit, FOX_HEADS, None, True)

    @pl.when(jnp.logical_and(fixed_ok == 0, n1 > 0))
    def _():
        sweep(listed_unit, n1, False)

    @pl.when(fixed_ok == 0)
    def _():
        sweep(diag_unit, FOX_HEADS, True)

    for h in range(FOX_HEADS):
        y_ref[0, h] = (o_ref[h, 0:FOX_DH, :] / o_ref[h, FOX_DH:FOX_DH + 1, :]).astype(BF)


def _attn_schedule(kst, qst):
    B, nt = kst.shape[0], kst.shape[1]
    cmax = kst[:, :, 0, :FOX_HEADS]
    cmin = kst[:, :, 1, :FOX_HEADS]
    kn = jnp.sqrt(kst[:, :, 2, :FOX_HEADS]) * NORM_SLACK
    qn = jnp.sqrt(qst[:, :, :, 0]) * NORM_SLACK
    upper = qn[:, :, None] * kn[:, None] + cmax[:, :, None] - cmin[:, None]
    lower = -qn * kn
    need = upper > lower[:, :, None] - UNDERFLOW_MARGIN
    ti = jnp.arange(nt)
    valid = (ti[None, :] < ti[:, None])[None, :, :, None]
    rank = jnp.where(valid & need, 0, jnp.where(valid, 1, 2)).reshape(B, nt, nt * FOX_HEADS)
    n_codes = nt * FOX_HEADS
    units = jnp.sort(rank * n_codes + jnp.arange(n_codes, dtype=rank.dtype), axis=-1) % n_codes
    units = units.astype(jnp.int32)
    cnt = jnp.sum(rank == 0, axis=-1).astype(jnp.int32)
    n1 = jnp.where(cnt > 0, jnp.maximum(cnt + (cnt & 1), 4), 0)

    top = qn * lax.cummax(kn, axis=1)
    ref = 0.5 * (top + lower) - REF_HEADROOM
    fixed_ok = jnp.all(top - lower <= REF_SPREAD_LIMIT, axis=-1).astype(jnp.int32)
    ref = jnp.broadcast_to(ref.reshape(B * nt, FOX_HEADS, 1, 1), (B * nt, FOX_HEADS, 1, LANES))
    return (n1.reshape(-1), cnt.reshape(-1), fixed_ok.reshape(-1),
            units.reshape(B * nt, 1, nt * FOX_HEADS), ref)


def _attention(qT, cumT, rb, kst, qst, k, dk, vT, tq):
    B, H, dh, S = qT.shape
    nt = S // tq
    n1, cnt, fixed_ok, units, ref = _attn_schedule(kst, qst)
    resident = dict(pipeline_mode=pl.Buffered(1))
    grid_spec = pltpu.PrefetchScalarGridSpec(
        num_scalar_prefetch=3,
        grid=(B, nt),
        in_specs=[
            pl.BlockSpec((1, 1, nt * H), lambda b, i, *_: (b * nt + i, 0, 0), memory_space=pltpu.SMEM),
            pl.BlockSpec((1, H, dh, tq), lambda b, i, *_: (b, 0, 0, i)),
            pl.BlockSpec((1, H, 1, tq), lambda b, i, *_: (b, 0, 0, i)),
            pl.BlockSpec((1, nt * H, 1, LANES), lambda b, i, *_: (b, 0, 0, 0)),
            pl.BlockSpec((1, H, 1, LANES), lambda b, i, *_: (b * nt + i, 0, 0, 0)),
            pl.BlockSpec((1, HEAD_PAIRS, nt, tq, LANES), lambda b, i, *_: (b, 0, 0, 0, 0), **resident),
            pl.BlockSpec((1, nt, tq, LANES), lambda b, i, *_: (b, 0, 0, 0), **resident),
            pl.BlockSpec((1, H, nt, VROWS, tq), lambda b, i, *_: (b, 0, 0, 0, 0), **resident),
        ],
        out_specs=pl.BlockSpec((1, H, dh, tq), lambda b, i, *_: (b, 0, 0, i)),
        scratch_shapes=[
            pltpu.VMEM((H, QA_ROWS, tq), BF),
            pltpu.VMEM((H, VROWS, tq), F32),
            pltpu.VMEM((H, 1, tq), F32),
            pltpu.VMEM((2, tq, tq), F32),
            pltpu.VMEM((tq, tq), BF),
            pltpu.VMEM((tq, tq), BF),
            pltpu.VMEM((tq, tq), BF),
        ],
    )
    return pl.pallas_call(
        _attn_kernel,
        grid_spec=grid_spec,
        out_shape=jax.ShapeDtypeStruct((B, H, dh, S), BF),
        compiler_params=pltpu.CompilerParams(
            dimension_semantics=("arbitrary", "arbitrary"), vmem_limit_bytes=VMEM_LIMIT),
        name="fox_attn",
    )(n1, cnt, fixed_ok, units, qT, cumT, rb, ref, k, dk, vT)


def _layer1_kernel(x_ref, yT_ref, w_inT_ref, mk_ref, mvT_ref, w_outT_ref, lng_ref, lnb_ref,
                   o_ref, yg_ref):
    tc = x_ref.shape[1] // TOKEN_SPLIT
    for c in range(TOKEN_SPLIT):
        cols = slice(c * tc, (c + 1) * tc)
        x = x_ref[0, cols, :]
        xb = x.astype(BF)
        qmT = _nt(w_inT_ref[0, D_MAIN:D_MIX, :], xb)
        gT = _nt(w_inT_ref[0, D_MIX:, :], xb)
        _finish_layer(x, yT_ref[0, :, cols].astype(F32), qmT, gT, mk_ref, mvT_ref, w_outT_ref,
                      lng_ref, lnb_ref, o_ref, yg_ref, cols)


def _layer1(x1, yT, w_inT, mk, mvT, w_outT, lng, lnb, tm):
    B, S, _ = x1.shape
    return pl.pallas_call(
        _layer1_kernel,
        grid=(B, S // tm),
        in_specs=[
            pl.BlockSpec((1, tm, D_MODEL), lambda b, t: (b, t, 0)),
            pl.BlockSpec((1, D_MAIN, tm), lambda b, t: (b, 0, t)),
            _layer_spec(w_inT, 1),
            pl.BlockSpec((1, 1, MEM_HEADS, N_MEM, MEM_DH), lambda b, t: (1, b, 0, 0, 0)),
            pl.BlockSpec((1, 1, MEM_HEADS, MEM_DH, N_MEM), lambda b, t: (1, b, 0, 0, 0)),
            _layer_spec(w_outT, 1),
            _layer_spec(lng, 1),
            _layer_spec(lnb, 1),
        ],
        out_specs=pl.BlockSpec((1, tm, D_MODEL), lambda b, t: (b, t, 0)),
        out_shape=jax.ShapeDtypeStruct((B, S, D_MODEL), F32),
        scratch_shapes=[pltpu.VMEM((D_MIX, tm), BF)],
        compiler_params=pltpu.CompilerParams(
            dimension_semantics=("arbitrary", "arbitrary"), vmem_limit_bytes=VMEM_LIMIT),
        name="layer1",
    )(x1, yT, w_inT, mk, mvT, w_outT, lng, lnb)


def kernel(x, mem, w_in, w_mem_kv, w_out, ln_g, ln_b, pool_w, pool_scale, w_kv_shared, b_forget):
    B, S, _ = x.shape
    tm = min(512, S)

    w_inT = jnp.swapaxes(w_in, 1, 2).astype(BF)
    w_outT = jnp.swapaxes(w_out, 1, 2).astype(BF)
    pool_wT = jnp.swapaxes(pool_w[0], 1, 2).astype(BF)
    ps = pool_scale[0].reshape(D_MAIN, 1)
    lng = ln_g.reshape(DEPTH, 1, D_MODEL)
    lnb = ln_b.reshape(DEPTH, 1, D_MODEL)
    wk = w_kv_shared[:, :D_MAIN].astype(BF)
    wvT = w_kv_shared[:, D_MAIN:2 * D_MAIN].T.astype(BF)
    wf = w_kv_shared[:, 2 * D_MAIN:]
    wf3 = jnp.concatenate([wf, wf, wf, jnp.zeros((D_MODEL, LANES - 3 * FOX_HEADS), F32)], axis=1).astype(BF)
    bias3 = jnp.concatenate([b_forget, b_forget, b_forget,
                             jnp.zeros((LANES - 3 * FOX_HEADS,), F32)]).reshape(1, LANES)

    mk, mvT = _mem_kv(mem, w_mem_kv.astype(BF))
    tl = 2 * tm if S % (2 * tm) == 0 else tm
    x1 = _layer0(x, w_inT, pool_wT, ps, mk, mvT, w_outT, lng, lnb, tl)
    k, dk, qT, vT, cumT, rb, kst, qst = _shared_kv(x1, wf3, bias3, wk, w_inT, wvT, tm)
    yT = _attention(qT, cumT, rb, kst, qst, k, dk, vT, tm)
    return _layer1(x1, yT.reshape(B, D_MAIN, S), w_inT, mk, mvT, w_outT, lng, lnb, tl)
```

```python
import numpy as np
import jax
import jax.numpy as jnp
from jax import lax
from jax.experimental import pallas as pl
from jax.experimental.pallas import tpu as pltpu

D_MODEL = 1024
N_MEM = 256
D_MAIN = 1024
POOL_WINDOWS = (2, 4, 8, 16)
POOL_GROUP = 256
FOX_HEADS = 16
FOX_DH = 64
MEM_HEADS = 4
MEM_DH = 128
D_MEM = MEM_HEADS * MEM_DH
D_MIX = D_MAIN + D_MEM
DEPTH = 2
ALPHA = (2 * DEPTH) ** 0.25
LN_EPS = 1e-5

LANES = 128
BF16_ROWS = 16
HEAD_PAIRS = FOX_HEADS // 2
QA_ROWS = 2 * LANES
VROWS = FOX_DH + BF16_ROWS
NEG = -1e30
LOG2E = 1.4426950408889634
UNDERFLOW_MARGIN = 160.0
NORM_SLACK = 1.02
BODY_TICKS = 6
REF_HEADROOM = 40.0
REF_SPREAD_LIMIT = 100.0
TOKEN_SPLIT = 2
VMEM_LIMIT = 56 * 1024 * 1024

BF = jnp.bfloat16
F32 = jnp.float32


def _mm(a, b):
    return jnp.dot(a, b, preferred_element_type=F32)


def _nt(a, b):
    return lax.dot_general(a, b, (((1,), (1,)), ((), ())), preferred_element_type=F32)


def _split3(v):
    hi = v.astype(BF)
    r1 = v - hi.astype(F32)
    mid = r1.astype(BF)
    lo = (r1 - mid.astype(F32)).astype(BF)
    return hi, mid, lo


def _silu(g):
    hg = 0.5 * g
    return hg + hg * jnp.tanh(hg)


def _const_spec(shape):
    nd = len(shape)
    return pl.BlockSpec(shape, lambda *_: (0,) * nd)


def _layer_spec(arr, layer):
    nd = arr.ndim
    return pl.BlockSpec((1,) + arr.shape[1:], lambda *_: (layer,) + (0,) * (nd - 1))


def _mem_kv_kernel(mem_ref, w_ref, mk_ref, mvT_ref):
    memb = mem_ref[0].astype(BF)
    mkv = _mm(memb, w_ref[0])
    for h in range(MEM_HEADS):
        mk_ref[0, 0, h] = mkv[:, h * MEM_DH:(h + 1) * MEM_DH].astype(BF)
        mv = mkv[:, D_MEM + h * MEM_DH:D_MEM + (h + 1) * MEM_DH]
        mvT_ref[0, 0, h] = mv.T.astype(BF)


def _mem_kv(mem, w_mem_kv_bf):
    B = mem.shape[0]
    L = w_mem_kv_bf.shape[0]
    return pl.pallas_call(
        _mem_kv_kernel,
        grid=(L, B),
        in_specs=[
            pl.BlockSpec((1, N_MEM, D_MODEL), lambda l, b: (b, 0, 0)),
            pl.BlockSpec((1, D_MODEL, 2 * D_MEM), lambda l, b: (l, 0, 0)),
        ],
        out_specs=[
            pl.BlockSpec((1, 1, MEM_HEADS, N_MEM, MEM_DH), lambda l, b: (l, b, 0, 0, 0)),
            pl.BlockSpec((1, 1, MEM_HEADS, MEM_DH, N_MEM), lambda l, b: (l, b, 0, 0, 0)),
        ],
        out_shape=[
            jax.ShapeDtypeStruct((L, B, MEM_HEADS, N_MEM, MEM_DH), BF),
            jax.ShapeDtypeStruct((L, B, MEM_HEADS, MEM_DH, N_MEM), BF),
        ],
        compiler_params=pltpu.CompilerParams(
            dimension_semantics=("arbitrary", "arbitrary"), vmem_limit_bytes=VMEM_LIMIT),
        name="mem_kv",
    )(mem, w_mem_kv_bf)


def _finish_layer(x, mainT, qmT, gT, mk_ref, mvT_ref, w_outT_ref, lng_ref, lnb_ref, o_ref, yg_ref, cols):
    yg_ref = yg_ref.at[:, cols]
    yg_ref[0:D_MAIN, :] = (mainT * _silu(gT[0:D_MAIN])).astype(BF)
    for h in range(MEM_HEADS):
        rows = slice(h * MEM_DH, (h + 1) * MEM_DH)
        qh = qmT[rows].astype(BF)
        lg = _mm(mk_ref[0, 0, h], qh) * (MEM_DH ** -0.5 * LOG2E)
        mx = jnp.max(lg, axis=0, keepdims=True)
        e = jnp.exp2(lg - mx)
        den = jnp.sum(e, axis=0, keepdims=True)
        ym = _mm(mvT_ref[0, 0, h], e.astype(BF)) * (1.0 / den)
        gm = gT[D_MAIN + h * MEM_DH:D_MAIN + (h + 1) * MEM_DH]
        yg_ref[D_MAIN + h * MEM_DH:D_MAIN + (h + 1) * MEM_DH, :] = (ym * _silu(gm)).astype(BF)
    outT = _mm(w_outT_ref[0], yg_ref[...])
    z = ALPHA * x + outT.T
    mu = jnp.mean(z, axis=-1, keepdims=True)
    zc = z - mu
    var = jnp.mean(zc * zc, axis=-1, keepdims=True)
    o_ref[0, cols, :] = zc * lax.rsqrt(var + LN_EPS) * lng_ref[0] + lnb_ref[0]


def _layer0_kernel(x_ref, w_inT_ref, band_ref, pool_wT_ref, ps_ref, mk_ref, mvT_ref, w_outT_ref,
                   lng_ref, lnb_ref, o_ref, tail_ref, main_ref, yg_ref):
    t = pl.program_id(1)
    x = x_ref[0]
    tm = x.shape[0]
    xb = x.astype(BF)
    uT = _nt(w_inT_ref[0, 0:D_MAIN, :], xb)
    qmT = _nt(w_inT_ref[0, D_MAIN:D_MIX, :], xb)
    gT = _nt(w_inT_ref[0, D_MIX:, :], xb)

    @pl.when(t == 0)
    def _():
        tail_ref[...] = jnp.zeros_like(tail_ref)

    tpos = t * tm + lax.broadcasted_iota(jnp.int32, (1, tm), 1)
    ub = uT.astype(BF)
    for g, w in enumerate(POOL_WINDOWS):
        rows = slice(g * POOL_GROUP, (g + 1) * POOL_GROUP)
        ext = jnp.concatenate([tail_ref[rows, :], ub[rows]], axis=1)
        wsum = jnp.concatenate(
            [_mm(ext[:, n * LANES:(n + 2) * LANES], band_ref[g]) for n in range(tm // LANES)], axis=1)
        inv_cnt = 1.0 / jnp.minimum(tpos + 1, w).astype(F32)
        pm = wsum * inv_cnt - uT[rows]
        main_ref[rows, :] = _mm(pool_wT_ref[g], pm.astype(BF)) * ps_ref[rows, :]
    tail_ref[...] = ub[:, tm - LANES:]

    _finish_layer(x, main_ref[...], qmT, gT, mk_ref, mvT_ref, w_outT_ref, lng_ref, lnb_ref,
                  o_ref, yg_ref, slice(0, tm))


def _pool_bands():
    s = np.arange(2 * LANES)[:, None]
    c = np.arange(LANES)[None, :] + LANES
    return np.stack([((s > c - w) & (s <= c)).astype(np.float32) for w in POOL_WINDOWS])


def _layer0(x, w_inT, pool_wT, ps, mk, mvT, w_outT, lng, lnb, tm):
    B, S, _ = x.shape
    band = jnp.asarray(_pool_bands(), BF)
    return pl.pallas_call(
        _layer0_kernel,
        grid=(B, S // tm),
        in_specs=[
            pl.BlockSpec((1, tm, D_MODEL), lambda b, t: (b, t, 0)),
            _layer_spec(w_inT, 0),
            _const_spec(band.shape),
            _const_spec(pool_wT.shape),
            _const_spec(ps.shape),
            pl.BlockSpec((1, 1, MEM_HEADS, N_MEM, MEM_DH), lambda b, t: (0, b, 0, 0, 0)),
            pl.BlockSpec((1, 1, MEM_HEADS, MEM_DH, N_MEM), lambda b, t: (0, b, 0, 0, 0)),
            _layer_spec(w_outT, 0),
            _layer_spec(lng, 0),
            _layer_spec(lnb, 0),
        ],
        out_specs=pl.BlockSpec((1, tm, D_MODEL), lambda b, t: (b, t, 0)),
        out_shape=jax.ShapeDtypeStruct((B, S, D_MODEL), F32),
        scratch_shapes=[
            pltpu.VMEM((D_MAIN, LANES), BF),
            pltpu.VMEM((D_MAIN, tm), F32),
            pltpu.VMEM((D_MIX, tm), BF),
        ],
        compiler_params=pltpu.CompilerParams(
            dimension_semantics=("arbitrary", "arbitrary"), vmem_limit_bytes=VMEM_LIMIT),
        name="layer0",
    )(x, w_inT, band, pool_wT, ps, mk, mvT, w_outT, lng, lnb)


def _shared_kv_kernel(x_ref, wf3_ref, bias3_ref, wk_ref, wqT_ref, wvT_ref, hsum_ref,
                      k_ref, dk_ref, qT_ref, vT_ref, cumT_ref, rb_ref, kst_ref, qst_ref, carry_ref):
    t = pl.program_id(1)

    @pl.when(t == 0)
    def _():
        carry_ref[...] = jnp.zeros_like(carry_ref)

    xb = x_ref[0].astype(BF)
    tm = xb.shape[0]
    lane = lax.broadcasted_iota(jnp.int32, (tm, LANES), 1)
    f3 = _mm(xb, wf3_ref[...]) + bias3_ref[...]
    lf = jnp.where(lane < 3 * FOX_HEADS, jax.nn.log_sigmoid(f3), 0.0)

    cb = min(256, tm)
    ri = lax.broadcasted_iota(jnp.int32, (cb, cb), 0)
    ci = lax.broadcasted_iota(jnp.int32, (cb, cb), 1)
    tri = jnp.where(ci <= ri, 1.0, 0.0).astype(BF)
    off = jnp.zeros((1, LANES), F32)
    blocks = []
    for i in range(tm // cb):
        hi, mid, lo = _split3(lf[i * cb:(i + 1) * cb])
        c = _mm(tri, hi) + _mm(tri, mid) + _mm(tri, lo) + off
        off = c[cb - 1:cb, :]
        blocks.append(c)
    c_loc = jnp.concatenate(blocks, axis=0) * LOG2E
    cum = c_loc + carry_ref[...]
    carry_ref[...] = cum[tm - 1:tm, :]

    hi, mid, lo = (v.astype(F32) for v in _split3(c_loc[0:1, :] - c_loc))
    dk_ref[0, 0] = jnp.where(lane < FOX_HEADS, hi,
                             jnp.where(lane < 2 * FOX_HEADS, mid,
                                       jnp.where(lane < 3 * FOX_HEADS, lo, 0.0))).astype(BF)
    kb = _mm(xb, wk_ref[...]).astype(BF)
    for g in range(HEAD_PAIRS):
        k_ref[0, g, 0] = kb[:, g * LANES:(g + 1) * LANES]

    qT = (_nt(wqT_ref[0], xb) * (FOX_DH ** -0.5 * LOG2E)).astype(BF)
    vT = _nt(wvT_ref[...], xb).astype(BF)

    kf32 = kb.astype(F32)
    kn2 = jnp.max(_mm((kf32 * kf32).astype(BF), hsum_ref[...]), axis=0, keepdims=True)
    cmax = jnp.max(cum, axis=0, keepdims=True)
    cmin = jnp.min(cum, axis=0, keepdims=True)
    srow = lax.broadcasted_iota(jnp.int32, (8, LANES), 0)
    kst_ref[0, 0] = jnp.where(srow == 0, cmax, jnp.where(srow == 1, cmin, jnp.where(srow == 2, kn2, 0.0)))
    qf32 = qT.astype(F32)
    qn2 = jnp.sum((qf32 * qf32).reshape(FOX_HEADS, FOX_DH, tm), axis=1)
    qst_ref[0, 0] = jnp.broadcast_to(jnp.max(qn2, axis=1, keepdims=True), (FOX_HEADS, LANES))
    ones_rows = jnp.where(lax.broadcasted_iota(jnp.int32, (BF16_ROWS, tm), 0) == 0, 1.0, 0.0).astype(BF)
    for h in range(FOX_HEADS):
        qT_ref[0, h] = qT[h * FOX_DH:(h + 1) * FOX_DH]
        vT_ref[0, h, 0, 0:FOX_DH, :] = vT[h * FOX_DH:(h + 1) * FOX_DH]
        vT_ref[0, h, 0, FOX_DH:, :] = ones_rows

    cT = cum.T
    for h in range(FOX_HEADS):
        cumT_ref[0, h] = cT[h:h + 1]
        rb_ref[0, h] = jnp.broadcast_to(cT[h:h + 1, 0:1], (1, LANES))


def _head_sum_matrix():
    g = np.zeros((D_MAIN, LANES), np.float32)
    for h in range(FOX_HEADS):
        g[h * FOX_DH:(h + 1) * FOX_DH, h] = 1.0
    return g


def _shared_kv(x1, wf3, bias3, wk, wqT, wvT, tm):
    B, S, _ = x1.shape
    nt = S // tm
    hsum = jnp.asarray(_head_sum_matrix(), BF)
    return pl.pallas_call(
        _shared_kv_kernel,
        grid=(B, nt),
        in_specs=[
            pl.BlockSpec((1, tm, D_MODEL), lambda b, t: (b, t, 0)),
            _const_spec(wf3.shape),
            _const_spec(bias3.shape),
            _const_spec(wk.shape),
            pl.BlockSpec((1, D_MAIN, D_MODEL), lambda b, t: (1, 0, 0)),
            _const_spec(wvT.shape),
            _const_spec(hsum.shape),
        ],
        out_specs=[
            pl.BlockSpec((1, HEAD_PAIRS, 1, tm, LANES), lambda b, t: (b, 0, t, 0, 0)),
            pl.BlockSpec((1, 1, tm, LANES), lambda b, t: (b, t, 0, 0)),
            pl.BlockSpec((1, FOX_HEADS, FOX_DH, tm), lambda b, t: (b, 0, 0, t)),
            pl.BlockSpec((1, FOX_HEADS, 1, VROWS, tm), lambda b, t: (b, 0, t, 0, 0)),
            pl.BlockSpec((1, FOX_HEADS, 1, tm), lambda b, t: (b, 0, 0, t)),
            pl.BlockSpec((1, FOX_HEADS, 1, LANES), lambda b, t: (b, t, 0, 0)),
            pl.BlockSpec((1, 1, 8, LANES), lambda b, t: (b, t, 0, 0)),
            pl.BlockSpec((1, 1, FOX_HEADS, LANES), lambda b, t: (b, t, 0, 0)),
        ],
        out_shape=[
            jax.ShapeDtypeStruct((B, HEAD_PAIRS, nt, tm, LANES), BF),
            jax.ShapeDtypeStruct((B, nt, tm, LANES), BF),
            jax.ShapeDtypeStruct((B, FOX_HEADS, FOX_DH, S), BF),
            jax.ShapeDtypeStruct((B, FOX_HEADS, nt, VROWS, tm), BF),
            jax.ShapeDtypeStruct((B, FOX_HEADS, 1, S), F32),
            jax.ShapeDtypeStruct((B, nt * FOX_HEADS, 1, LANES), F32),
            jax.ShapeDtypeStruct((B, nt, 8, LANES), F32),
            jax.ShapeDtypeStruct((B, nt, FOX_HEADS, LANES), F32),
        ],
        scratch_shapes=[pltpu.VMEM((1, LANES), F32)],
        compiler_params=pltpu.CompilerParams(
            dimension_semantics=("arbitrary", "arbitrary"), vmem_limit_bytes=VMEM_LIMIT),
        name="shared_kv",
    )(x1, wf3, bias3, wk, wqT, wvT, hsum)


def _attn_kernel(n1_ref, cnt_ref, fx_ref, ul_ref, qT_ref, cumT_ref, rb_ref, ref_ref, k_ref, dk_ref, vT_ref,
                 y_ref, qa_ref, o_ref, m_ref, s_ref, pa_ref, pb_ref, pc_ref):
    b = pl.program_id(0)
    i = pl.program_id(1)
    step = b * pl.num_programs(1) + i
    n1 = n1_ref[step]
    fixed_ok = fx_ref[step]
    tq = qa_ref.shape[2]
    tk = k_ref.shape[3]

    @pl.when(i == 0)
    def _():
        rowi = lax.broadcasted_iota(jnp.int32, (LANES, tq), 0)
        zeros = jnp.zeros((FOX_DH, tq), BF)
        for h in range(FOX_HEADS):
            lo = (h % 2) * FOX_DH
            qa_ref[h, FOX_DH - lo:2 * FOX_DH - lo, :] = zeros
            piece_rows = (rowi == h) | (rowi == FOX_HEADS + h) | (rowi == 2 * FOX_HEADS + h)
            qa_ref[h, LANES:, :] = jnp.where(piece_rows, 1.0, 0.0).astype(BF)

    for h in range(FOX_HEADS):
        lo = (h % 2) * FOX_DH
        qa_ref[h, lo:lo + FOX_DH, :] = qT_ref[0, h]
    m_ref[...] = jnp.full(m_ref.shape, NEG, F32)
    o_ref[...] = jnp.zeros_like(o_ref)

    def logits(h, j, m_prev, slot, masked):
        pair_idx = lax.shift_right_logical(jnp.int32(h), 1)
        lhs = jnp.concatenate([k_ref[0, pair_idx, j], dk_ref[0, j]], axis=1)
        s = _mm(lhs, qa_ref[h])
        if masked:
            r = lax.broadcasted_iota(jnp.int32, (tk, tq), 0)
            c = lax.broadcasted_iota(jnp.int32, (tk, tq), 1)
            s = jnp.where(r <= c, s, NEG)
        a = cumT_ref[0, h] - jnp.tile(rb_ref[0, j * FOX_HEADS + h], (1, tq // LANES))
        m_cur = jnp.maximum(m_prev, jnp.max(s, axis=0, keepdims=True) + a)
        s_ref[slot] = s
        return m_cur, m_cur - a, jnp.exp2(m_prev - m_cur)

    def probs(slot, shift):
        (pa_ref, pb_ref)[slot][...] = jnp.exp2(s_ref[slot] - shift).astype(BF)

    def values(h, j, slot, al):
        o_ref[h] = al * o_ref[h] + _mm(vT_ref[0, h, j], (pa_ref, pb_ref)[slot][...])

    def sweep(unit, n, masked):
        (h0, j0), (h1, j1), (h2, j2), (h3, j3) = unit(0), unit(1), unit(2), unit(3)
        m, sh_e, a_e = logits(h0, j0, m_ref[h0], 0, masked)
        m_ref[h0] = m
        m, sh_o, a_o = logits(h1, j1, m_ref[h1], 1, masked)
        m_ref[h1] = m
        probs(0, sh_e)
        a_c = a_e
        m, sh_e, a_e = logits(h2, j2, m_ref[h2], 0, masked)
        m_ref[h2] = m
        probs(1, sh_o)
        values(h0, j0, 0, a_c)
        a_c = a_o
        m, sh_o, a_o = logits(h3, j3, m_ref[h3], 1, masked)
        m_ref[h3] = m

        def pair(g, carry):
            sh_e, a_e, sh_o, a_o, a_c = carry
            t = 2 * g
            (hp, jp), (he, je) = unit(t - 1), unit(t)
            (he2, je2), (ho2, jo2) = unit(t + 2), unit(t + 3)
            mp_e, mp_o = m_ref[he2], m_ref[ho2]
            probs(0, sh_e)
            m_e2, sh_e2, a_e2 = logits(he2, je2, mp_e, 0, masked)
            values(hp, jp, 1, a_c)
            probs(1, sh_o)
            mp_o = jnp.where(ho2 == he2, m_e2, mp_o)
            m_o2, sh_o2, a_o2 = logits(ho2, jo2, mp_o, 1, masked)
            values(he, je, 0, a_e)
            m_ref[he2] = m_e2
            m_ref[ho2] = m_o2
            return sh_e2, a_e2, sh_o2, a_o2, a_o

        sh_e, a_e, sh_o, a_o, a_c = lax.fori_loop(1, n // 2 - 1, pair, (sh_e, a_e, sh_o, a_o, a_c))
        (hp, jp), (he, je), (ho, jo) = unit(n - 3), unit(n - 2), unit(n - 1)
        probs(0, sh_e)
        values(hp, jp, 1, a_c)
        probs(1, sh_o)
        values(he, je, 0, a_e)
        values(ho, jo, 1, a_o)

    half = tk // 2

    def fixed_probs(t, unit, cnt, masked, p_buf):
        h, j = unit(t)
        pair_idx = lax.shift_right_logical(jnp.int32(h), 1)
        base = rb_ref[0, j * FOX_HEADS + h] + ref_ref[0, h]
        shift = jnp.tile(base, (1, tq // LANES)) - cumT_ref[0, h]
        if cnt is not None:
            shift = shift + jnp.where(t >= cnt, -NEG, 0.0)
        if not masked:
            lhs = jnp.concatenate([k_ref[0, pair_idx, j], dk_ref[0, j]], axis=1)
            p_buf[...] = jnp.exp2(_mm(lhs, qa_ref[h]) - shift).astype(BF)
            return
        top = jnp.concatenate([k_ref[0, pair_idx, j, 0:half, :], dk_ref[0, j, 0:half, :]], axis=1)
        bot = jnp.concatenate([k_ref[0, pair_idx, j, half:, :], dk_ref[0, j, half:, :]], axis=1)
        s_top = _mm(top, qa_ref[h])
        s_bot = _mm(bot, qa_ref[h, :, half:])
        keep_top = (lax.broadcasted_iota(jnp.int32, (half, tq), 0)
                    <= lax.broadcasted_iota(jnp.int32, (half, tq), 1))
        keep_bot = (lax.broadcasted_iota(jnp.int32, (half, tq - half), 0)
                    <= lax.broadcasted_iota(jnp.int32, (half, tq - half), 1))
        shift_bot = jnp.tile(base, (1, (tq - half) // LANES)) - cumT_ref[0, h, :, half:]
        p_buf[0:half, :] = jnp.exp2(jnp.where(keep_top, s_top, NEG) - shift).astype(BF)
        p_buf[half:, half:] = jnp.exp2(jnp.where(keep_bot, s_bot, NEG) - shift_bot).astype(BF)

    def fixed_values(t, unit, p_buf, masked):
        h, j = unit(t)
        if not masked:
            o_ref[h] += _mm(vT_ref[0, h, j], p_buf[...])
            return
        o_ref[h] += _mm(vT_ref[0, h, j, :, 0:half], p_buf[0:half, :])
        o_ref[h, :, half:] += _mm(vT_ref[0, h, j, :, half:], p_buf[half:, half:])

    def fixed_sweep(unit, n, cnt, masked):
        bufs = (pa_ref, pb_ref, pc_ref)
        fixed_probs(0, unit, cnt, masked, bufs[0])
        fixed_probs(1, unit, cnt, masked, bufs[1])

        def tick(t, k, last_ab):
            fixed_values(t, unit, bufs[k], masked)
            if last_ab is None or t + 2 <= last_ab:
                fixed_probs(t + 2, unit, cnt, masked, bufs[(k + 2) % 3])

        def body(g, carry):
            for k in range(BODY_TICKS):
                tick(BODY_TICKS * g + k, k % 3, None)
            return carry

        n_loop = (n - 2) // BODY_TICKS
        lax.fori_loop(0, n_loop, body, 0)
        if isinstance(n, int):
            for t in range(BODY_TICKS * n_loop, n):
                tick(t, t % 3, n - 1)
        else:
            done = BODY_TICKS * n_loop

            def short_body(g, carry):
                for k in range(3):
                    tick(done + k, k, None)
                return carry

            lax.fori_loop(0, (n - 2 - done) // 3, short_body, 0)
            fixed_values(n - 2, unit, bufs[0], masked)
            fixed_values(n - 1, unit, bufs[1], masked)

    def listed_unit(t):
        code = ul_ref[0, 0, t]
        return code & (FOX_HEADS - 1), lax.shift_right_logical(code, 4)

    diag_unit = lambda t: (t, i)

    @pl.when(jnp.logical_and(fixed_ok == 1, n1 > 0))
    def _():
        cnt = cnt_ref[step]
        fixed_sweep(listed_unit, cnt + (2 - cnt) % 3, cnt, False)

    @pl.when(fixed_ok == 1)
    def _():
        fixed_sweep(diag_unit, FOX_HEADS, None, True)

    @pl.when(jnp.logical_and(fixed_ok == 0, n1 > 0))
    def _():
        sweep(listed_unit, n1, False)

    @pl.when(fixed_ok == 0)
    def _():
        sweep(diag_unit, FOX_HEADS, True)

    for h in range(FOX_HEADS):
        y_ref[0, h] = (o_ref[h, 0:FOX_DH, :] * (1.0 / o_ref[h, FOX_DH:FOX_DH + 1, :])).astype(BF)


def _attn_schedule(kst, qst):
    B, nt = kst.shape[0], kst.shape[1]
    cmax = kst[:, :, 0, :FOX_HEADS]
    cmin = kst[:, :, 1, :FOX_HEADS]
    kn = jnp.sqrt(kst[:, :, 2, :FOX_HEADS]) * NORM_SLACK
    qn = jnp.sqrt(qst[:, :, :, 0]) * NORM_SLACK
    upper = qn[:, :, None] * kn[:, None] + cmax[:, :, None] - cmin[:, None]
    lower = -qn * kn
    need = upper > lower[:, :, None] - UNDERFLOW_MARGIN
    ti = jnp.arange(nt)
    valid = (ti[None, :] < ti[:, None])[None, :, :, None]
    rank = jnp.where(valid & need, 0, jnp.where(valid, 1, 2)).reshape(B, nt, nt * FOX_HEADS)
    n_codes = nt * FOX_HEADS
    units = jnp.sort(rank * n_codes + jnp.arange(n_codes, dtype=rank.dtype), axis=-1) % n_codes
    units = units.astype(jnp.int32)
    cnt = jnp.sum(rank == 0, axis=-1).astype(jnp.int32)
    n1 = jnp.where(cnt > 0, jnp.maximum(cnt + (cnt & 1), 4), 0)

    top = qn * lax.cummax(kn, axis=1)
    ref = 0.5 * (top + lower) - REF_HEADROOM
    fixed_ok = jnp.all(top - lower <= REF_SPREAD_LIMIT, axis=-1).astype(jnp.int32)
    ref = jnp.broadcast_to(ref.reshape(B * nt, FOX_HEADS, 1, 1), (B * nt, FOX_HEADS, 1, LANES))
    return (n1.reshape(-1), cnt.reshape(-1), fixed_ok.reshape(-1),
            units.reshape(B * nt, 1, nt * FOX_HEADS), ref)


def _attention(qT, cumT, rb, kst, qst, k, dk, vT, tq):
    B, H, dh, S = qT.shape
    nt = S // tq
    n1, cnt, fixed_ok, units, ref = _attn_schedule(kst, qst)
    resident = dict(pipeline_mode=pl.Buffered(1))
    grid_spec = pltpu.PrefetchScalarGridSpec(
        num_scalar_prefetch=3,
        grid=(B, nt),
        in_specs=[
            pl.BlockSpec((1, 1, nt * H), lambda b, i, *_: (b * nt + i, 0, 0), memory_space=pltpu.SMEM),
            pl.BlockSpec((1, H, dh, tq), lambda b, i, *_: (b, 0, 0, i)),
            pl.BlockSpec((1, H, 1, tq), lambda b, i, *_: (b, 0, 0, i)),
            pl.BlockSpec((1, nt * H, 1, LANES), lambda b, i, *_: (b, 0, 0, 0)),
            pl.BlockSpec((1, H, 1, LANES), lambda b, i, *_: (b * nt + i, 0, 0, 0)),
            pl.BlockSpec((1, HEAD_PAIRS, nt, tq, LANES), lambda b, i, *_: (b, 0, 0, 0, 0), **resident),
            pl.BlockSpec((1, nt, tq, LANES), lambda b, i, *_: (b, 0, 0, 0), **resident),
            pl.BlockSpec((1, H, nt, VROWS, tq), lambda b, i, *_: (b, 0, 0, 0, 0), **resident),
        ],
        out_specs=pl.BlockSpec((1, H, dh, tq), lambda b, i, *_: (b, 0, 0, i)),
        scratch_shapes=[
            pltpu.VMEM((H, QA_ROWS, tq), BF),
            pltpu.VMEM((H, VROWS, tq), F32),
            pltpu.VMEM((H, 1, tq), F32),
            pltpu.VMEM((2, tq, tq), F32),
            pltpu.VMEM((tq, tq), BF),
            pltpu.VMEM((tq, tq), BF),
            pltpu.VMEM((tq, tq), BF),
        ],
    )
    return pl.pallas_call(
        _attn_kernel,
        grid_spec=grid_spec,
        out_shape=jax.ShapeDtypeStruct((B, H, dh, S), BF),
        compiler_params=pltpu.CompilerParams(
            dimension_semantics=("arbitrary", "arbitrary"), vmem_limit_bytes=VMEM_LIMIT),
        name="fox_attn",
    )(n1, cnt, fixed_ok, units, qT, cumT, rb, ref, k, dk, vT)


def _layer1_kernel(x_ref, yT_ref, w_inT_ref, mk_ref, mvT_ref, w_outT_ref, lng_ref, lnb_ref,
                   o_ref, yg_ref):
    tc = x_ref.shape[1] // TOKEN_SPLIT
    for c in range(TOKEN_SPLIT):
        cols = slice(c * tc, (c + 1) * tc)
        x = x_ref[0, cols, :]
        xb = x.astype(BF)
        qmT = _nt(w_inT_ref[0, D_MAIN:D_MIX, :], xb)
        gT = _nt(w_inT_ref[0, D_MIX:, :], xb)
        _finish_layer(x, yT_ref[0, :, cols].astype(F32), qmT, gT, mk_ref, mvT_ref, w_outT_ref,
                      lng_ref, lnb_ref, o_ref, yg_ref, cols)


def _layer1(x1, yT, w_inT, mk, mvT, w_outT, lng, lnb, tm):
    B, S, _ = x1.shape
    return pl.pallas_call(
        _layer1_kernel,
        grid=(B, S // tm),
        in_specs=[
            pl.BlockSpec((1, tm, D_MODEL), lambda b, t: (b, t, 0)),
            pl.BlockSpec((1, D_MAIN, tm), lambda b, t: (b, 0, t)),
            _layer_spec(w_inT, 1),
            pl.BlockSpec((1, 1, MEM_HEADS, N_MEM, MEM_DH), lambda b, t: (1, b, 0, 0, 0)),
            pl.BlockSpec((1, 1, MEM_HEADS, MEM_DH, N_MEM), lambda b, t: (1, b, 0, 0, 0)),
            _layer_spec(w_outT, 1),
            _layer_spec(lng, 1),
            _layer_spec(lnb, 1),
        ],
        out_specs=pl.BlockSpec((1, tm, D_MODEL), lambda b, t: (b, t, 0)),
        out_shape=jax.ShapeDtypeStruct((B, S, D_MODEL), F32),
        scratch_shapes=[pltpu.VMEM((D_MIX, tm), BF)],
        compiler_params=pltpu.CompilerParams(
            dimension_semantics=("arbitrary", "arbitrary"), vmem_limit_bytes=VMEM_LIMIT),
        name="layer1",
    )(x1, yT, w_inT, mk, mvT, w_outT, lng, lnb)


def kernel(x, mem, w_in, w_mem_kv, w_out, ln_g, ln_b, pool_w, pool_scale, w_kv_shared, b_forget):
    B, S, _ = x.shape
    tm = min(512, S)

    w_inT = jnp.swapaxes(w_in, 1, 2).astype(BF)
    w_outT = jnp.swapaxes(w_out, 1, 2).astype(BF)
    pool_wT = jnp.swapaxes(pool_w[0], 1, 2).astype(BF)
    ps = pool_scale[0].reshape(D_MAIN, 1)
    lng = ln_g.reshape(DEPTH, 1, D_MODEL)
    lnb = ln_b.reshape(DEPTH, 1, D_MODEL)
    wk = w_kv_shared[:, :D_MAIN].astype(BF)
    wvT = w_kv_shared[:, D_MAIN:2 * D_MAIN].T.astype(BF)
    wf = w_kv_shared[:, 2 * D_MAIN:]
    wf3 = jnp.concatenate([wf, wf, wf, jnp.zeros((D_MODEL, LANES - 3 * FOX_HEADS), F32)], axis=1).astype(BF)
    bias3 = jnp.concatenate([b_forget, b_forget, b_forget,
                             jnp.zeros((LANES - 3 * FOX_HEADS,), F32)]).reshape(1, LANES)

    mk, mvT = _mem_kv(mem, w_mem_kv.astype(BF))
    tl = 2 * tm if S % (2 * tm) == 0 else tm
    x1 = _layer0(x, w_inT, pool_wT, ps, mk, mvT, w_outT, lng, lnb, tl)
    k, dk, qT, vT, cumT, rb, kst, qst = _shared_kv(x1, wf3, bias3, wk, w_inT, wvT, tm)
    yT = _attention(qT, cumT, rb, kst, qst, k, dk, vT, tm)
    return _layer1(x1, yT.reshape(B, D_MAIN, S), w_inT, mk, mvT, w_outT, lng, lnb, tl)
```

```python
import numpy as np
import jax
import jax.numpy as jnp
from jax import lax
from jax.experimental import pallas as pl
from jax.experimental.pallas import tpu as pltpu

D_MODEL = 1024
N_MEM = 256
D_MAIN = 1024
POOL_WINDOWS = (2, 4, 8, 16)
POOL_GROUP = 256
FOX_HEADS = 16
FOX_DH = 64
MEM_HEADS = 4
MEM_DH = 128
D_MEM = MEM_HEADS * MEM_DH
D_MIX = D_MAIN + D_MEM
DEPTH = 2
ALPHA = (2 * DEPTH) ** 0.25
LN_EPS = 1e-5

LANES = 128
BF16_ROWS = 16
HEAD_PAIRS = FOX_HEADS // 2
QA_ROWS = 2 * LANES
VROWS = FOX_DH + BF16_ROWS
NEG = -1e30
LOG2E = 1.4426950408889634
UNDERFLOW_MARGIN = 152.0
NORM_SLACK = 1.02
BODY_TICKS = 6
REF_HEADROOM = 40.0
REF_SPREAD_LIMIT = 100.0
ATTN_TILE = 512
TOKEN_SPLIT = 2
VMEM_LIMIT = 56 * 1024 * 1024

BF = jnp.bfloat16
F32 = jnp.float32


def _mm(a, b):
    return jnp.dot(a, b, preferred_element_type=F32)


def _nt(a, b):
    return lax.dot_general(a, b, (((1,), (1,)), ((), ())), preferred_element_type=F32)


def _split3(v):
    hi = v.astype(BF)
    r1 = v - hi.astype(F32)
    mid = r1.astype(BF)
    lo = (r1 - mid.astype(F32)).astype(BF)
    return hi, mid, lo


def _silu(g):
    hg = 0.5 * g
    return hg + hg * jnp.tanh(hg)


def _const_spec(shape):
    nd = len(shape)
    return pl.BlockSpec(shape, lambda *_: (0,) * nd)


def _layer_spec(arr, layer):
    nd = arr.ndim
    return pl.BlockSpec((1,) + arr.shape[1:], lambda *_: (layer,) + (0,) * (nd - 1))


def _mem_kv_kernel(mem_ref, w_ref, mk_ref, mvT_ref):
    memb = mem_ref[0].astype(BF)
    mkv = _mm(memb, w_ref[0])
    for h in range(MEM_HEADS):
        mk_ref[0, 0, h] = mkv[:, h * MEM_DH:(h + 1) * MEM_DH].astype(BF)
        mv = mkv[:, D_MEM + h * MEM_DH:D_MEM + (h + 1) * MEM_DH]
        mvT_ref[0, 0, h] = mv.T.astype(BF)


def _mem_kv(mem, w_mem_kv_bf):
    B = mem.shape[0]
    L = w_mem_kv_bf.shape[0]
    return pl.pallas_call(
        _mem_kv_kernel,
        grid=(L, B),
        in_specs=[
            pl.BlockSpec((1, N_MEM, D_MODEL), lambda l, b: (b, 0, 0)),
            pl.BlockSpec((1, D_MODEL, 2 * D_MEM), lambda l, b: (l, 0, 0)),
        ],
        out_specs=[
            pl.BlockSpec((1, 1, MEM_HEADS, N_MEM, MEM_DH), lambda l, b: (l, b, 0, 0, 0)),
            pl.BlockSpec((1, 1, MEM_HEADS, MEM_DH, N_MEM), lambda l, b: (l, b, 0, 0, 0)),
        ],
        out_shape=[
            jax.ShapeDtypeStruct((L, B, MEM_HEADS, N_MEM, MEM_DH), BF),
            jax.ShapeDtypeStruct((L, B, MEM_HEADS, MEM_DH, N_MEM), BF),
        ],
        compiler_params=pltpu.CompilerParams(
            dimension_semantics=("arbitrary", "arbitrary"), vmem_limit_bytes=VMEM_LIMIT),
        name="mem_kv",
    )(mem, w_mem_kv_bf)


def _finish_layer(x, mainT, qmT, gT, mk_ref, mvT_ref, w_outT_ref, lng_ref, lnb_ref, o_ref, yg_ref, cols):
    yg_ref = yg_ref.at[:, cols]
    yg_ref[0:D_MAIN, :] = (mainT * _silu(gT[0:D_MAIN])).astype(BF)
    for h in range(MEM_HEADS):
        rows = slice(h * MEM_DH, (h + 1) * MEM_DH)
        qh = qmT[rows].astype(BF)
        lg = _mm(mk_ref[0, 0, h], qh) * (MEM_DH ** -0.5 * LOG2E)
        mx = jnp.max(lg, axis=0, keepdims=True)
        e = jnp.exp2(lg - mx)
        den = jnp.sum(e, axis=0, keepdims=True)
        ym = _mm(mvT_ref[0, 0, h], e.astype(BF)) * (1.0 / den)
        gm = gT[D_MAIN + h * MEM_DH:D_MAIN + (h + 1) * MEM_DH]
        yg_ref[D_MAIN + h * MEM_DH:D_MAIN + (h + 1) * MEM_DH, :] = (ym * _silu(gm)).astype(BF)
    outT = _mm(w_outT_ref[0], yg_ref[...])
    z = ALPHA * x + outT.T
    mu = jnp.mean(z, axis=-1, keepdims=True)
    zc = z - mu
    var = jnp.mean(zc * zc, axis=-1, keepdims=True)
    o_ref[0, cols, :] = zc * lax.rsqrt(var + LN_EPS) * lng_ref[0] + lnb_ref[0]


def _layer0_kernel(x_ref, w_inT_ref, band_ref, pool_wT_ref, ps_ref, mk_ref, mvT_ref, w_outT_ref,
                   lng_ref, lnb_ref, o_ref, tail_ref, main_ref, yg_ref):
    t = pl.program_id(1)
    x = x_ref[0]
    tm = x.shape[0]
    xb = x.astype(BF)
    uT = _nt(w_inT_ref[0, 0:D_MAIN, :], xb)
    qmT = _nt(w_inT_ref[0, D_MAIN:D_MIX, :], xb)
    gT = _nt(w_inT_ref[0, D_MIX:, :], xb)

    @pl.when(t == 0)
    def _():
        tail_ref[...] = jnp.zeros_like(tail_ref)

    tpos = t * tm + lax.broadcasted_iota(jnp.int32, (1, tm), 1)
    ub = uT.astype(BF)
    for g, w in enumerate(POOL_WINDOWS):
        rows = slice(g * POOL_GROUP, (g + 1) * POOL_GROUP)
        ext = jnp.concatenate([tail_ref[rows, :], ub[rows]], axis=1)
        wsum = jnp.concatenate(
            [_mm(ext[:, n * LANES:(n + 2) * LANES], band_ref[g]) for n in range(tm // LANES)], axis=1)
        inv_cnt = 1.0 / jnp.minimum(tpos + 1, w).astype(F32)
        pm = wsum * inv_cnt - uT[rows]
        main_ref[rows, :] = _mm(pool_wT_ref[g], pm.astype(BF)) * ps_ref[rows, :]
    tail_ref[...] = ub[:, tm - LANES:]

    _finish_layer(x, main_ref[...], qmT, gT, mk_ref, mvT_ref, w_outT_ref, lng_ref, lnb_ref,
                  o_ref, yg_ref, slice(0, tm))


def _pool_bands():
    s = np.arange(2 * LANES)[:, None]
    c = np.arange(LANES)[None, :] + LANES
    return np.stack([((s > c - w) & (s <= c)).astype(np.float32) for w in POOL_WINDOWS])


def _layer0(x, w_inT, pool_wT, ps, mk, mvT, w_outT, lng, lnb, tm):
    B, S, _ = x.shape
    band = jnp.asarray(_pool_bands(), BF)
    return pl.pallas_call(
        _layer0_kernel,
        grid=(B, S // tm),
        in_specs=[
            pl.BlockSpec((1, tm, D_MODEL), lambda b, t: (b, t, 0)),
            _layer_spec(w_inT, 0),
            _const_spec(band.shape),
            _const_spec(pool_wT.shape),
            _const_spec(ps.shape),
            pl.BlockSpec((1, 1, MEM_HEADS, N_MEM, MEM_DH), lambda b, t: (0, b, 0, 0, 0)),
            pl.BlockSpec((1, 1, MEM_HEADS, MEM_DH, N_MEM), lambda b, t: (0, b, 0, 0, 0)),
            _layer_spec(w_outT, 0),
            _layer_spec(lng, 0),
            _layer_spec(lnb, 0),
        ],
        out_specs=pl.BlockSpec((1, tm, D_MODEL), lambda b, t: (b, t, 0)),
        out_shape=jax.ShapeDtypeStruct((B, S, D_MODEL), F32),
        scratch_shapes=[
            pltpu.VMEM((D_MAIN, LANES), BF),
            pltpu.VMEM((D_MAIN, tm), F32),
            pltpu.VMEM((D_MIX, tm), BF),
        ],
        compiler_params=pltpu.CompilerParams(
            dimension_semantics=("arbitrary", "arbitrary"), vmem_limit_bytes=VMEM_LIMIT),
        name="layer0",
    )(x, w_inT, band, pool_wT, ps, mk, mvT, w_outT, lng, lnb)


def _shared_kv_kernel(x_ref, wf3_ref, bias3_ref, wk_ref, wqT_ref, wvT_ref, hsum_ref,
                      k_ref, dk_ref, qT_ref, vT_ref, cumT_ref, rb_ref, kst_ref, qst_ref, carry_ref):
    t = pl.program_id(1)

    @pl.when(t == 0)
    def _():
        carry_ref[...] = jnp.zeros_like(carry_ref)

    xb = x_ref[0].astype(BF)
    tm = xb.shape[0]
    lane = lax.broadcasted_iota(jnp.int32, (tm, LANES), 1)
    f3 = _mm(xb, wf3_ref[...]) + bias3_ref[...]
    lf = jnp.where(lane < 3 * FOX_HEADS, jax.nn.log_sigmoid(f3), 0.0)

    cb = min(256, tm)
    ri = lax.broadcasted_iota(jnp.int32, (cb, cb), 0)
    ci = lax.broadcasted_iota(jnp.int32, (cb, cb), 1)
    tri = jnp.where(ci <= ri, 1.0, 0.0).astype(BF)
    off = jnp.zeros((1, LANES), F32)
    blocks = []
    for i in range(tm // cb):
        hi, mid, lo = _split3(lf[i * cb:(i + 1) * cb])
        c = _mm(tri, hi) + _mm(tri, mid) + _mm(tri, lo) + off
        off = c[cb - 1:cb, :]
        blocks.append(c)
    c_loc = jnp.concatenate(blocks, axis=0) * LOG2E
    cum = c_loc + carry_ref[...]
    carry_ref[...] = cum[tm - 1:tm, :]

    hi, mid, lo = (v.astype(F32) for v in _split3(c_loc[0:1, :] - c_loc))
    dk_ref[0, 0] = jnp.where(lane < FOX_HEADS, hi,
                             jnp.where(lane < 2 * FOX_HEADS, mid,
                                       jnp.where(lane < 3 * FOX_HEADS, lo, 0.0))).astype(BF)
    kb = _mm(xb, wk_ref[...]).astype(BF)
    for g in range(HEAD_PAIRS):
        k_ref[0, g, 0] = kb[:, g * LANES:(g + 1) * LANES]

    qT = (_nt(wqT_ref[0], xb) * (FOX_DH ** -0.5 * LOG2E)).astype(BF)
    vT = _nt(wvT_ref[...], xb).astype(BF)

    kf32 = kb.astype(F32)
    kn2 = jnp.max(_mm((kf32 * kf32).astype(BF), hsum_ref[...]), axis=0, keepdims=True)
    cmax = jnp.max(cum, axis=0, keepdims=True)
    cmin = jnp.min(cum, axis=0, keepdims=True)
    srow = lax.broadcasted_iota(jnp.int32, (8, LANES), 0)
    kst_ref[0, 0] = jnp.where(srow == 0, cmax, jnp.where(srow == 1, cmin, jnp.where(srow == 2, kn2, 0.0)))
    qf32 = qT.astype(F32)
    qn2 = jnp.sum((qf32 * qf32).reshape(FOX_HEADS, FOX_DH, tm), axis=1)
    qst_ref[0, 0] = jnp.broadcast_to(jnp.max(qn2, axis=1, keepdims=True), (FOX_HEADS, LANES))
    ones_rows = jnp.where(lax.broadcasted_iota(jnp.int32, (BF16_ROWS, tm), 0) == 0, 1.0, 0.0).astype(BF)
    for h in range(FOX_HEADS):
        qT_ref[0, h] = qT[h * FOX_DH:(h + 1) * FOX_DH]
        vT_ref[0, h, 0, 0:FOX_DH, :] = vT[h * FOX_DH:(h + 1) * FOX_DH]
        vT_ref[0, h, 0, FOX_DH:, :] = ones_rows

    cT = cum.T
    for h in range(FOX_HEADS):
        cumT_ref[0, h] = cT[h:h + 1]
        rb_ref[0, h] = jnp.broadcast_to(cT[h:h + 1, 0:1], (1, LANES))


def _head_sum_matrix():
    g = np.zeros((D_MAIN, LANES), np.float32)
    for h in range(FOX_HEADS):
        g[h * FOX_DH:(h + 1) * FOX_DH, h] = 1.0
    return g


def _shared_kv(x1, wf3, bias3, wk, wqT, wvT, tm):
    B, S, _ = x1.shape
    nt = S // tm
    hsum = jnp.asarray(_head_sum_matrix(), BF)
    return pl.pallas_call(
        _shared_kv_kernel,
        grid=(B, nt),
        in_specs=[
            pl.BlockSpec((1, tm, D_MODEL), lambda b, t: (b, t, 0)),
            _const_spec(wf3.shape),
            _const_spec(bias3.shape),
            _const_spec(wk.shape),
            pl.BlockSpec((1, D_MAIN, D_MODEL), lambda b, t: (1, 0, 0)),
            _const_spec(wvT.shape),
            _const_spec(hsum.shape),
        ],
        out_specs=[
            pl.BlockSpec((1, HEAD_PAIRS, 1, tm, LANES), lambda b, t: (b, 0, t, 0, 0)),
            pl.BlockSpec((1, 1, tm, LANES), lambda b, t: (b, t, 0, 0)),
            pl.BlockSpec((1, FOX_HEADS, FOX_DH, tm), lambda b, t: (b, 0, 0, t)),
            pl.BlockSpec((1, FOX_HEADS, 1, VROWS, tm), lambda b, t: (b, 0, t, 0, 0)),
            pl.BlockSpec((1, FOX_HEADS, 1, tm), lambda b, t: (b, 0, 0, t)),
            pl.BlockSpec((1, FOX_HEADS, 1, LANES), lambda b, t: (b, t, 0, 0)),
            pl.BlockSpec((1, 1, 8, LANES), lambda b, t: (b, t, 0, 0)),
            pl.BlockSpec((1, 1, FOX_HEADS, LANES), lambda b, t: (b, t, 0, 0)),
        ],
        out_shape=[
            jax.ShapeDtypeStruct((B, HEAD_PAIRS, nt, tm, LANES), BF),
            jax.ShapeDtypeStruct((B, nt, tm, LANES), BF),
            jax.ShapeDtypeStruct((B, FOX_HEADS, FOX_DH, S), BF),
            jax.ShapeDtypeStruct((B, FOX_HEADS, nt, VROWS, tm), BF),
            jax.ShapeDtypeStruct((B, FOX_HEADS, 1, S), F32),
            jax.ShapeDtypeStruct((B, nt * FOX_HEADS, 1, LANES), F32),
            jax.ShapeDtypeStruct((B, nt, 8, LANES), F32),
            jax.ShapeDtypeStruct((B, nt, FOX_HEADS, LANES), F32),
        ],
        scratch_shapes=[pltpu.VMEM((1, LANES), F32)],
        compiler_params=pltpu.CompilerParams(
            dimension_semantics=("arbitrary", "arbitrary"), vmem_limit_bytes=VMEM_LIMIT),
        name="shared_kv",
    )(x1, wf3, bias3, wk, wqT, wvT, hsum)


def _attn_kernel(n1_ref, cnt_ref, fx_ref, ul_ref, qT_ref, cumT_ref, rb_ref, ref_ref, k_ref, dk_ref, vT_ref,
                 y_ref, qa_ref, o_ref, m_ref, s_ref, pa_ref, pb_ref, pc_ref):
    b = pl.program_id(0)
    i = pl.program_id(1)
    step = b * pl.num_programs(1) + i
    n1 = n1_ref[step]
    fixed_ok = fx_ref[step]
    tq = qa_ref.shape[2]
    tk = k_ref.shape[3]

    @pl.when(i == 0)
    def _():
        rowi = lax.broadcasted_iota(jnp.int32, (LANES, tq), 0)
        zeros = jnp.zeros((FOX_DH, tq), BF)
        for h in range(FOX_HEADS):
            lo = (h % 2) * FOX_DH
            qa_ref[h, FOX_DH - lo:2 * FOX_DH - lo, :] = zeros
            piece_rows = (rowi == h) | (rowi == FOX_HEADS + h) | (rowi == 2 * FOX_HEADS + h)
            qa_ref[h, LANES:, :] = jnp.where(piece_rows, 1.0, 0.0).astype(BF)

    for h in range(FOX_HEADS):
        lo = (h % 2) * FOX_DH
        qa_ref[h, lo:lo + FOX_DH, :] = qT_ref[0, h]
    m_ref[...] = jnp.full(m_ref.shape, NEG, F32)
    o_ref[...] = jnp.zeros_like(o_ref)

    def logits(h, j, m_prev, slot, masked):
        pair_idx = lax.shift_right_logical(jnp.int32(h), 1)
        lhs = jnp.concatenate([k_ref[0, pair_idx, j], dk_ref[0, j]], axis=1)
        s = _mm(lhs, qa_ref[h])
        if masked:
            r = lax.broadcasted_iota(jnp.int32, (tk, tq), 0)
            c = lax.broadcasted_iota(jnp.int32, (tk, tq), 1)
            s = jnp.where(r <= c, s, NEG)
        a = cumT_ref[0, h] - jnp.tile(rb_ref[0, j * FOX_HEADS + h], (1, tq // LANES))
        m_cur = jnp.maximum(m_prev, jnp.max(s, axis=0, keepdims=True) + a)
        s_ref[slot] = s
        return m_cur, m_cur - a, jnp.exp2(m_prev - m_cur)

    def probs(slot, shift):
        (pa_ref, pb_ref)[slot][...] = jnp.exp2(s_ref[slot] - shift).astype(BF)

    def values(h, j, slot, al):
        o_ref[h] = al * o_ref[h] + _mm(vT_ref[0, h, j], (pa_ref, pb_ref)[slot][...])

    def sweep(unit, n, masked):
        (h0, j0), (h1, j1), (h2, j2), (h3, j3) = unit(0), unit(1), unit(2), unit(3)
        m, sh_e, a_e = logits(h0, j0, m_ref[h0], 0, masked)
        m_ref[h0] = m
        m, sh_o, a_o = logits(h1, j1, m_ref[h1], 1, masked)
        m_ref[h1] = m
        probs(0, sh_e)
        a_c = a_e
        m, sh_e, a_e = logits(h2, j2, m_ref[h2], 0, masked)
        m_ref[h2] = m
        probs(1, sh_o)
        values(h0, j0, 0, a_c)
        a_c = a_o
        m, sh_o, a_o = logits(h3, j3, m_ref[h3], 1, masked)
        m_ref[h3] = m

        def pair(g, carry):
            sh_e, a_e, sh_o, a_o, a_c = carry
            t = 2 * g
            (hp, jp), (he, je) = unit(t - 1), unit(t)
            (he2, je2), (ho2, jo2) = unit(t + 2), unit(t + 3)
            mp_e, mp_o = m_ref[he2], m_ref[ho2]
            probs(0, sh_e)
            m_e2, sh_e2, a_e2 = logits(he2, je2, mp_e, 0, masked)
            values(hp, jp, 1, a_c)
            probs(1, sh_o)
            mp_o = jnp.where(ho2 == he2, m_e2, mp_o)
            m_o2, sh_o2, a_o2 = logits(ho2, jo2, mp_o, 1, masked)
            values(he, je, 0, a_e)
            m_ref[he2] = m_e2
            m_ref[ho2] = m_o2
            return sh_e2, a_e2, sh_o2, a_o2, a_o

        sh_e, a_e, sh_o, a_o, a_c = lax.fori_loop(1, n // 2 - 1, pair, (sh_e, a_e, sh_o, a_o, a_c))
        (hp, jp), (he, je), (ho, jo) = unit(n - 3), unit(n - 2), unit(n - 1)
        probs(0, sh_e)
        values(hp, jp, 1, a_c)
        probs(1, sh_o)
        values(he, je, 0, a_e)
        values(ho, jo, 1, a_o)

    half = tk // 2

    def fixed_probs(t, unit, cnt, masked, p_buf):
        h, j = unit(t)
        pair_idx = lax.shift_right_logical(jnp.int32(h), 1)
        base = rb_ref[0, j * FOX_HEADS + h] + ref_ref[0, h]
        shift = jnp.tile(base, (1, tq // LANES)) - cumT_ref[0, h]
        if cnt is not None:
            shift = shift + jnp.where(t >= cnt, -NEG, 0.0)
        if not masked:
            lhs = jnp.concatenate([k_ref[0, pair_idx, j], dk_ref[0, j]], axis=1)
            p_buf[...] = jnp.exp2(_mm(lhs, qa_ref[h]) - shift).astype(BF)
            return
        top = jnp.concatenate([k_ref[0, pair_idx, j, 0:half, :], dk_ref[0, j, 0:half, :]], axis=1)
        bot = jnp.concatenate([k_ref[0, pair_idx, j, half:, :], dk_ref[0, j, half:, :]], axis=1)
        s_top = _mm(top, qa_ref[h])
        s_bot = _mm(bot, qa_ref[h, :, half:])
        keep_top = (lax.broadcasted_iota(jnp.int32, (half, tq), 0)
                    <= lax.broadcasted_iota(jnp.int32, (half, tq), 1))
        keep_bot = (lax.broadcasted_iota(jnp.int32, (half, tq - half), 0)
                    <= lax.broadcasted_iota(jnp.int32, (half, tq - half), 1))
        shift_bot = jnp.tile(base, (1, (tq - half) // LANES)) - cumT_ref[0, h, :, half:]
        p_buf[0:half, :] = jnp.exp2(jnp.where(keep_top, s_top, NEG) - shift).astype(BF)
        p_buf[half:, half:] = jnp.exp2(jnp.where(keep_bot, s_bot, NEG) - shift_bot).astype(BF)

    def fixed_values(t, unit, p_buf, masked):
        h, j = unit(t)
        if not masked:
            o_ref[h] += _mm(vT_ref[0, h, j], p_buf[...])
            return
        o_ref[h] += _mm(vT_ref[0, h, j, :, 0:half], p_buf[0:half, :])
        o_ref[h, :, half:] += _mm(vT_ref[0, h, j, :, half:], p_buf[half:, half:])

    def fixed_sweep(unit, n, cnt, masked):
        bufs = (pa_ref, pb_ref, pc_ref)
        fixed_probs(0, unit, cnt, masked, bufs[0])
        fixed_probs(1, unit, cnt, masked, bufs[1])

        def tick(t, k, last_ab):
            fixed_values(t, unit, bufs[k], masked)
            if last_ab is None or t + 2 <= last_ab:
                fixed_probs(t + 2, unit, cnt, masked, bufs[(k + 2) % 3])

        def body(g, carry):
            for k in range(BODY_TICKS):
                tick(BODY_TICKS * g + k, k % 3, None)
            return carry

        n_loop = (n - 2) // BODY_TICKS
        lax.fori_loop(0, n_loop, body, 0)
        if isinstance(n, int):
            for t in range(BODY_TICKS * n_loop, n):
                tick(t, t % 3, n - 1)
        else:
            done = BODY_TICKS * n_loop

            def short_body(g, carry):
                for k in range(3):
                    tick(done + k, k, None)
                return carry

            lax.fori_loop(0, (n - 2 - done) // 3, short_body, 0)
            fixed_values(n - 2, unit, bufs[0], masked)
            fixed_values(n - 1, unit, bufs[1], masked)

    def listed_unit(t):
        code = ul_ref[0, 0, t]
        return code & (FOX_HEADS - 1), lax.shift_right_logical(code, 4)

    diag_unit = lambda t: (t, i)

    @pl.when(jnp.logical_and(fixed_ok == 1, n1 > 0))
    def _():
        cnt = cnt_ref[step]
        fixed_sweep(listed_unit, cnt + (2 - cnt) % 3, cnt, False)

    @pl.when(fixed_ok == 1)
    def _():
        fixed_sweep(diag_unit, FOX_HEADS, None, True)

    @pl.when(jnp.logical_and(fixed_ok == 0, n1 > 0))
    def _():
        sweep(listed_unit, n1, False)

    @pl.when(fixed_ok == 0)
    def _():
        sweep(diag_unit, FOX_HEADS, True)

    for h in range(FOX_HEADS):
        y_ref[0, h] = (o_ref[h, 0:FOX_DH, :] * (1.0 / o_ref[h, FOX_DH:FOX_DH + 1, :])).astype(BF)


def _attn_schedule(kst, qst):
    B, nt = kst.shape[0], kst.shape[1]
    cmax = kst[:, :, 0, :FOX_HEADS]
    cmin = kst[:, :, 1, :FOX_HEADS]
    kn = jnp.sqrt(kst[:, :, 2, :FOX_HEADS]) * NORM_SLACK
    qn = jnp.sqrt(qst[:, :, :, 0]) * NORM_SLACK
    upper = qn[:, :, None] * kn[:, None] + cmax[:, :, None] - cmin[:, None]
    lower = -qn * kn
    need = upper > lower[:, :, None] - UNDERFLOW_MARGIN
    ti = jnp.arange(nt)
    valid = (ti[None, :] < ti[:, None])[None, :, :, None]
    rank = jnp.where(valid & need, 0, jnp.where(valid, 1, 2)).reshape(B, nt, nt * FOX_HEADS)
    n_codes = nt * FOX_HEADS
    units = jnp.sort(rank * n_codes + jnp.arange(n_codes, dtype=rank.dtype), axis=-1) % n_codes
    units = units.astype(jnp.int32)
    cnt = jnp.sum(rank == 0, axis=-1).astype(jnp.int32)
    n1 = jnp.where(cnt > 0, jnp.maximum(cnt + (cnt & 1), 4), 0)

    top = qn * lax.cummax(kn, axis=1)
    ref = 0.5 * (top + lower) - REF_HEADROOM
    fixed_ok = jnp.all(top - lower <= REF_SPREAD_LIMIT, axis=-1).astype(jnp.int32)
    ref = jnp.broadcast_to(ref.reshape(B * nt, FOX_HEADS, 1, 1), (B * nt, FOX_HEADS, 1, LANES))
    return (n1.reshape(-1), cnt.reshape(-1), fixed_ok.reshape(-1),
            units.reshape(B * nt, 1, nt * FOX_HEADS), ref)


def _attention(qT, cumT, rb, kst, qst, k, dk, vT, tq):
    B, H, dh, S = qT.shape
    nt = S // tq
    n1, cnt, fixed_ok, units, ref = _attn_schedule(kst, qst)
    resident = dict(pipeline_mode=pl.Buffered(1))
    grid_spec = pltpu.PrefetchScalarGridSpec(
        num_scalar_prefetch=3,
        grid=(B, nt),
        in_specs=[
            pl.BlockSpec((1, 1, nt * H), lambda b, i, *_: (b * nt + i, 0, 0), memory_space=pltpu.SMEM),
            pl.BlockSpec((1, H, dh, tq), lambda b, i, *_: (b, 0, 0, i)),
            pl.BlockSpec((1, H, 1, tq), lambda b, i, *_: (b, 0, 0, i)),
            pl.BlockSpec((1, nt * H, 1, LANES), lambda b, i, *_: (b, 0, 0, 0)),
            pl.BlockSpec((1, H, 1, LANES), lambda b, i, *_: (b * nt + i, 0, 0, 0)),
            pl.BlockSpec((1, HEAD_PAIRS, nt, tq, LANES), lambda b, i, *_: (b, 0, 0, 0, 0), **resident),
            pl.BlockSpec((1, nt, tq, LANES), lambda b, i, *_: (b, 0, 0, 0), **resident),
            pl.BlockSpec((1, H, nt, VROWS, tq), lambda b, i, *_: (b, 0, 0, 0, 0), **resident),
        ],
        out_specs=pl.BlockSpec((1, H, dh, tq), lambda b, i, *_: (b, 0, 0, i)),
        scratch_shapes=[
            pltpu.VMEM((H, QA_ROWS, tq), BF),
            pltpu.VMEM((H, VROWS, tq), F32),
            pltpu.VMEM((H, 1, tq), F32),
            pltpu.VMEM((2, tq, tq), F32),
            pltpu.VMEM((tq, tq), BF),
            pltpu.VMEM((tq, tq), BF),
            pltpu.VMEM((tq, tq), BF),
        ],
    )
    return pl.pallas_call(
        _attn_kernel,
        grid_spec=grid_spec,
        out_shape=jax.ShapeDtypeStruct((B, H, dh, S), BF),
        compiler_params=pltpu.CompilerParams(
            dimension_semantics=("arbitrary", "arbitrary"), vmem_limit_bytes=VMEM_LIMIT),
        name="fox_attn",
    )(n1, cnt, fixed_ok, units, qT, cumT, rb, ref, k, dk, vT)


def _layer1_kernel(x_ref, yT_ref, w_inT_ref, mk_ref, mvT_ref, w_outT_ref, lng_ref, lnb_ref,
                   o_ref, yg_ref):
    tc = x_ref.shape[1] // TOKEN_SPLIT
    for c in range(TOKEN_SPLIT):
        cols = slice(c * tc, (c + 1) * tc)
        x = x_ref[0, cols, :]
        xb = x.astype(BF)
        qmT = _nt(w_inT_ref[0, D_MAIN:D_MIX, :], xb)
        gT = _nt(w_inT_ref[0, D_MIX:, :], xb)
        _finish_layer(x, yT_ref[0, :, cols].astype(F32), qmT, gT, mk_ref, mvT_ref, w_outT_ref,
                      lng_ref, lnb_ref, o_ref, yg_ref, cols)


def _layer1(x1, yT, w_inT, mk, mvT, w_outT, lng, lnb, tm):
    B, S, _ = x1.shape
    return pl.pallas_call(
        _layer1_kernel,
        grid=(B, S // tm),
        in_specs=[
            pl.BlockSpec((1, tm, D_MODEL), lambda b, t: (b, t, 0)),
            pl.BlockSpec((1, D_MAIN, tm), lambda b, t: (b, 0, t)),
            _layer_spec(w_inT, 1),
            pl.BlockSpec((1, 1, MEM_HEADS, N_MEM, MEM_DH), lambda b, t: (1, b, 0, 0, 0)),
            pl.BlockSpec((1, 1, MEM_HEADS, MEM_DH, N_MEM), lambda b, t: (1, b, 0, 0, 0)),
            _layer_spec(w_outT, 1),
            _layer_spec(lng, 1),
            _layer_spec(lnb, 1),
        ],
        out_specs=pl.BlockSpec((1, tm, D_MODEL), lambda b, t: (b, t, 0)),
        out_shape=jax.ShapeDtypeStruct((B, S, D_MODEL), F32),
        scratch_shapes=[pltpu.VMEM((D_MIX, tm), BF)],
        compiler_params=pltpu.CompilerParams(
            dimension_semantics=("arbitrary", "arbitrary"), vmem_limit_bytes=VMEM_LIMIT),
        name="layer1",
    )(x1, yT, w_inT, mk, mvT, w_outT, lng, lnb)


def kernel(x, mem, w_in, w_mem_kv, w_out, ln_g, ln_b, pool_w, pool_scale, w_kv_shared, b_forget):
    B, S, _ = x.shape
    tm = min(ATTN_TILE, S)
    assert S % tm == 0 and tm % (2 * LANES) == 0, (S, tm)

    w_inT = jnp.swapaxes(w_in, 1, 2).astype(BF)
    w_outT = jnp.swapaxes(w_out, 1, 2).astype(BF)
    pool_wT = jnp.swapaxes(pool_w[0], 1, 2).astype(BF)
    ps = pool_scale[0].reshape(D_MAIN, 1)
    lng = ln_g.reshape(DEPTH, 1, D_MODEL)
    lnb = ln_b.reshape(DEPTH, 1, D_MODEL)
    wk = w_kv_shared[:, :D_MAIN].astype(BF)
    wvT = w_kv_shared[:, D_MAIN:2 * D_MAIN].T.astype(BF)
    wf = w_kv_shared[:, 2 * D_MAIN:]
    wf3 = jnp.concatenate([wf, wf, wf, jnp.zeros((D_MODEL, LANES - 3 * FOX_HEADS), F32)], axis=1).astype(BF)
    bias3 = jnp.concatenate([b_forget, b_forget, b_forget,
                             jnp.zeros((LANES - 3 * FOX_HEADS,), F32)]).reshape(1, LANES)

    mk, mvT = _mem_kv(mem, w_mem_kv.astype(BF))
    tl = 2 * tm if S % (2 * tm) == 0 else tm
    x1 = _layer0(x, w_inT, pool_wT, ps, mk, mvT, w_outT, lng, lnb, tl)
    k, dk, qT, vT, cumT, rb, kst, qst = _shared_kv(x1, wf3, bias3, wk, w_inT, wvT, tm)
    yT = _attention(qT, cumT, rb, kst, qst, k, dk, vT, tm)
    return _layer1(x1, yT.reshape(B, D_MAIN, S), w_inT, mk, mvT, w_outT, lng, lnb, tl)
```

```python
import numpy as np
import jax
import jax.numpy as jnp
from jax import lax
from jax.experimental import pallas as pl
from jax.experimental.pallas import tpu as pltpu

D_MODEL = 1024
N_MEM = 256
D_MAIN = 1024
POOL_WINDOWS = (2, 4, 8, 16)
POOL_GROUP = 256
FOX_HEADS = 16
FOX_DH = 64
MEM_HEADS = 4
MEM_DH = 128
D_MEM = MEM_HEADS * MEM_DH
D_MIX = D_MAIN + D_MEM
DEPTH = 2
ALPHA = (2 * DEPTH) ** 0.25
LN_EPS = 1e-5

LANES = 128
BF16_ROWS = 16
HEAD_PAIRS = FOX_HEADS // 2
QA_ROWS = 2 * LANES
VROWS = FOX_DH + BF16_ROWS
NEG = -1e30
LOG2E = 1.4426950408889634
UNDERFLOW_MARGIN = 152.0
NORM_SLACK = 1.02
BODY_TICKS = 6
REF_HEADROOM = 40.0
REF_SPREAD_LIMIT = 100.0
ATTN_TILE = 512
VMEM_LIMIT = 56 * 1024 * 1024

BF = jnp.bfloat16
F32 = jnp.float32


def _mm(a, b):
    return jnp.dot(a, b, preferred_element_type=F32)


def _nt(a, b):
    return lax.dot_general(a, b, (((1,), (1,)), ((), ())), preferred_element_type=F32)


def _split3(v):
    hi = v.astype(BF)
    r1 = v - hi.astype(F32)
    mid = r1.astype(BF)
    lo = (r1 - mid.astype(F32)).astype(BF)
    return hi, mid, lo


def _silu(g):
    hg = 0.5 * g
    return hg + hg * jnp.tanh(hg)


def _const_spec(shape):
    nd = len(shape)
    return pl.BlockSpec(shape, lambda *_: (0,) * nd)


def _layer_spec(arr, layer):
    nd = arr.ndim
    return pl.BlockSpec((1,) + arr.shape[1:], lambda *_: (layer,) + (0,) * (nd - 1))


def _mem_kv_kernel(mem_ref, w_ref, mk_ref, mvT_ref):
    memb = mem_ref[0].astype(BF)
    mkv = _mm(memb, w_ref[0])
    for h in range(MEM_HEADS):
        mk_ref[0, 0, h] = mkv[:, h * MEM_DH:(h + 1) * MEM_DH].astype(BF)
        mv = mkv[:, D_MEM + h * MEM_DH:D_MEM + (h + 1) * MEM_DH]
        mvT_ref[0, 0, h] = mv.T.astype(BF)


def _mem_kv(mem, w_mem_kv_bf):
    B = mem.shape[0]
    L = w_mem_kv_bf.shape[0]
    return pl.pallas_call(
        _mem_kv_kernel,
        grid=(L, B),
        in_specs=[
            pl.BlockSpec((1, N_MEM, D_MODEL), lambda l, b: (b, 0, 0)),
            pl.BlockSpec((1, D_MODEL, 2 * D_MEM), lambda l, b: (l, 0, 0)),
        ],
        out_specs=[
            pl.BlockSpec((1, 1, MEM_HEADS, N_MEM, MEM_DH), lambda l, b: (l, b, 0, 0, 0)),
            pl.BlockSpec((1, 1, MEM_HEADS, MEM_DH, N_MEM), lambda l, b: (l, b, 0, 0, 0)),
        ],
        out_shape=[
            jax.ShapeDtypeStruct((L, B, MEM_HEADS, N_MEM, MEM_DH), BF),
            jax.ShapeDtypeStruct((L, B, MEM_HEADS, MEM_DH, N_MEM), BF),
        ],
        compiler_params=pltpu.CompilerParams(
            dimension_semantics=("arbitrary", "arbitrary"), vmem_limit_bytes=VMEM_LIMIT),
        name="mem_kv",
    )(mem, w_mem_kv_bf)


def _finish_layer(x, mainT, qmT, gT, mk_ref, mvT_ref, w_outT_ref, lng_ref, lnb_ref, o_ref, yg_ref):
    yg_ref[0:D_MAIN, :] = (mainT * _silu(gT[0:D_MAIN])).astype(BF)
    for h in range(MEM_HEADS):
        rows = slice(h * MEM_DH, (h + 1) * MEM_DH)
        qh = qmT[rows].astype(BF)
        lg = _mm(mk_ref[0, 0, h], qh) * (MEM_DH ** -0.5 * LOG2E)
        mx = jnp.max(lg, axis=0, keepdims=True)
        e = jnp.exp2(lg - mx)
        den = jnp.sum(e, axis=0, keepdims=True)
        ym = _mm(mvT_ref[0, 0, h], e.astype(BF)) * (1.0 / den)
        gm = gT[D_MAIN + h * MEM_DH:D_MAIN + (h + 1) * MEM_DH]
        yg_ref[D_MAIN + h * MEM_DH:D_MAIN + (h + 1) * MEM_DH, :] = (ym * _silu(gm)).astype(BF)
    outT = _mm(w_outT_ref[0], yg_ref[...])
    z = ALPHA * x + outT.T
    mu = jnp.mean(z, axis=-1, keepdims=True)
    zc = z - mu
    var = jnp.mean(zc * zc, axis=-1, keepdims=True)
    o_ref[0] = zc * lax.rsqrt(var + LN_EPS) * lng_ref[0] + lnb_ref[0]


def _layer0_kernel(x_ref, w_inT_ref, band_ref, pool_wT_ref, ps_ref, mk_ref, mvT_ref, w_outT_ref,
                   lng_ref, lnb_ref, o_ref, tail_ref, main_ref, yg_ref):
    t = pl.program_id(1)
    x = x_ref[0]
    tm = x.shape[0]
    xb = x.astype(BF)
    uT = _nt(w_inT_ref[0, 0:D_MAIN, :], xb)
    qmT = _nt(w_inT_ref[0, D_MAIN:D_MIX, :], xb)
    gT = _nt(w_inT_ref[0, D_MIX:, :], xb)

    @pl.when(t == 0)
    def _():
        tail_ref[...] = jnp.zeros_like(tail_ref)

    tpos = t * tm + lax.broadcasted_iota(jnp.int32, (1, tm), 1)
    ub = uT.astype(BF)
    for g, w in enumerate(POOL_WINDOWS):
        rows = slice(g * POOL_GROUP, (g + 1) * POOL_GROUP)
        ext = jnp.concatenate([tail_ref[rows, :], ub[rows]], axis=1)
        wsum = jnp.concatenate(
            [_mm(ext[:, n * LANES:(n + 2) * LANES], band_ref[g]) for n in range(tm // LANES)], axis=1)
        inv_cnt = 1.0 / jnp.minimum(tpos + 1, w).astype(F32)
        pm = wsum * inv_cnt - uT[rows]
        main_ref[rows, :] = _mm(pool_wT_ref[g], pm.astype(BF)) * ps_ref[rows, :]
    tail_ref[...] = ub[:, tm - LANES:]

    _finish_layer(x, main_ref[...], qmT, gT, mk_ref, mvT_ref, w_outT_ref, lng_ref, lnb_ref,
                  o_ref, yg_ref)


def _pool_bands():
    s = np.arange(2 * LANES)[:, None]
    c = np.arange(LANES)[None, :] + LANES
    return np.stack([((s > c - w) & (s <= c)).astype(np.float32) for w in POOL_WINDOWS])


def _layer0(x, w_inT, pool_wT, ps, mk, mvT, w_outT, lng, lnb, tm):
    B, S, _ = x.shape
    band = jnp.asarray(_pool_bands(), BF)
    return pl.pallas_call(
        _layer0_kernel,
        grid=(B, S // tm),
        in_specs=[
            pl.BlockSpec((1, tm, D_MODEL), lambda b, t: (b, t, 0)),
            _layer_spec(w_inT, 0),
            _const_spec(band.shape),
            _const_spec(pool_wT.shape),
            _const_spec(ps.shape),
            pl.BlockSpec((1, 1, MEM_HEADS, N_MEM, MEM_DH), lambda b, t: (0, b, 0, 0, 0)),
            pl.BlockSpec((1, 1, MEM_HEADS, MEM_DH, N_MEM), lambda b, t: (0, b, 0, 0, 0)),
            _layer_spec(w_outT, 0),
            _layer_spec(lng, 0),
            _layer_spec(lnb, 0),
        ],
        out_specs=pl.BlockSpec((1, tm, D_MODEL), lambda b, t: (b, t, 0)),
        out_shape=jax.ShapeDtypeStruct((B, S, D_MODEL), F32),
        scratch_shapes=[
            pltpu.VMEM((D_MAIN, LANES), BF),
            pltpu.VMEM((D_MAIN, tm), F32),
            pltpu.VMEM((D_MIX, tm), BF),
        ],
        compiler_params=pltpu.CompilerParams(
            dimension_semantics=("arbitrary", "arbitrary"), vmem_limit_bytes=VMEM_LIMIT),
        name="layer0",
    )(x, w_inT, band, pool_wT, ps, mk, mvT, w_outT, lng, lnb)


def _shared_kv_kernel(x_ref, wf3_ref, bias3_ref, wk_ref, wqT_ref, wvT_ref, hsum_ref,
                      k_ref, dk_ref, qT_ref, vT_ref, cumT_ref, rb_ref, kst_ref, qst_ref, carry_ref):
    t = pl.program_id(1)

    @pl.when(t == 0)
    def _():
        carry_ref[...] = jnp.zeros_like(carry_ref)

    xb = x_ref[0].astype(BF)
    tm = xb.shape[0]
    lane = lax.broadcasted_iota(jnp.int32, (tm, LANES), 1)
    f3 = _mm(xb, wf3_ref[...]) + bias3_ref[...]
    lf = jnp.where(lane < 3 * FOX_HEADS, jax.nn.log_sigmoid(f3), 0.0)

    cb = min(256, tm)
    ri = lax.broadcasted_iota(jnp.int32, (cb, cb), 0)
    ci = lax.broadcasted_iota(jnp.int32, (cb, cb), 1)
    tri = jnp.where(ci <= ri, 1.0, 0.0).astype(BF)
    off = jnp.zeros((1, LANES), F32)
    blocks = []
    for i in range(tm // cb):
        hi, mid, lo = _split3(lf[i * cb:(i + 1) * cb])
        c = _mm(tri, hi) + _mm(tri, mid) + _mm(tri, lo) + off
        off = c[cb - 1:cb, :]
        blocks.append(c)
    c_loc = jnp.concatenate(blocks, axis=0) * LOG2E
    cum = c_loc + carry_ref[...]
    carry_ref[...] = cum[tm - 1:tm, :]

    hi, mid, lo = (v.astype(F32) for v in _split3(c_loc[0:1, :] - c_loc))
    dk_ref[0, 0] = jnp.where(lane < FOX_HEADS, hi,
                             jnp.where(lane < 2 * FOX_HEADS, mid,
                                       jnp.where(lane < 3 * FOX_HEADS, lo, 0.0))).astype(BF)
    kb = _mm(xb, wk_ref[...]).astype(BF)
    for g in range(HEAD_PAIRS):
        k_ref[0, g, 0] = kb[:, g * LANES:(g + 1) * LANES]

    qT = (_nt(wqT_ref[0], xb) * (FOX_DH ** -0.5 * LOG2E)).astype(BF)
    vT = _nt(wvT_ref[...], xb).astype(BF)

    kf32 = kb.astype(F32)
    kn2 = jnp.max(_mm((kf32 * kf32).astype(BF), hsum_ref[...]), axis=0, keepdims=True)
    cmax = jnp.max(cum, axis=0, keepdims=True)
    cmin = jnp.min(cum, axis=0, keepdims=True)
    srow = lax.broadcasted_iota(jnp.int32, (8, LANES), 0)
    kst_ref[0, 0] = jnp.where(srow == 0, cmax, jnp.where(srow == 1, cmin, jnp.where(srow == 2, kn2, 0.0)))
    qf32 = qT.astype(F32)
    qn2 = jnp.sum((qf32 * qf32).reshape(FOX_HEADS, FOX_DH, tm), axis=1)
    qst_ref[0, 0] = jnp.broadcast_to(jnp.max(qn2, axis=1, keepdims=True), (FOX_HEADS, LANES))
    ones_rows = jnp.where(lax.broadcasted_iota(jnp.int32, (BF16_ROWS, tm), 0) == 0, 1.0, 0.0).astype(BF)
    for h in range(FOX_HEADS):
        qT_ref[0, h] = qT[h * FOX_DH:(h + 1) * FOX_DH]
        vT_ref[0, h, 0, 0:FOX_DH, :] = vT[h * FOX_DH:(h + 1) * FOX_DH]
        vT_ref[0, h, 0, FOX_DH:, :] = ones_rows

    cT = cum.T
    for h in range(FOX_HEADS):
        cumT_ref[0, h] = cT[h:h + 1]
        rb_ref[0, h] = jnp.broadcast_to(cT[h:h + 1, 0:1], (1, LANES))


def _head_sum_matrix():
    g = np.zeros((D_MAIN, LANES), np.float32)
    for h in range(FOX_HEADS):
        g[h * FOX_DH:(h + 1) * FOX_DH, h] = 1.0
    return g


def _shared_kv(x1, wf3, bias3, wk, wqT, wvT, tm):
    B, S, _ = x1.shape
    nt = S // tm
    hsum = jnp.asarray(_head_sum_matrix(), BF)
    return pl.pallas_call(
        _shared_kv_kernel,
        grid=(B, nt),
        in_specs=[
            pl.BlockSpec((1, tm, D_MODEL), lambda b, t: (b, t, 0)),
            _const_spec(wf3.shape),
            _const_spec(bias3.shape),
            _const_spec(wk.shape),
            pl.BlockSpec((1, D_MAIN, D_MODEL), lambda b, t: (1, 0, 0)),
            _const_spec(wvT.shape),
            _const_spec(hsum.shape),
        ],
        out_specs=[
            pl.BlockSpec((1, HEAD_PAIRS, 1, tm, LANES), lambda b, t: (b, 0, t, 0, 0)),
            pl.BlockSpec((1, 1, tm, LANES), lambda b, t: (b, t, 0, 0)),
            pl.BlockSpec((1, FOX_HEADS, FOX_DH, tm), lambda b, t: (b, 0, 0, t)),
            pl.BlockSpec((1, FOX_HEADS, 1, VROWS, tm), lambda b, t: (b, 0, t, 0, 0)),
            pl.BlockSpec((1, FOX_HEADS, 1, tm), lambda b, t: (b, 0, 0, t)),
            pl.BlockSpec((1, FOX_HEADS, 1, LANES), lambda b, t: (b, t, 0, 0)),
            pl.BlockSpec((1, 1, 8, LANES), lambda b, t: (b, t, 0, 0)),
            pl.BlockSpec((1, 1, FOX_HEADS, LANES), lambda b, t: (b, t, 0, 0)),
        ],
        out_shape=[
            jax.ShapeDtypeStruct((B, HEAD_PAIRS, nt, tm, LANES), BF),
            jax.ShapeDtypeStruct((B, nt, tm, LANES), BF),
            jax.ShapeDtypeStruct((B, FOX_HEADS, FOX_DH, S), BF),
            jax.ShapeDtypeStruct((B, FOX_HEADS, nt, VROWS, tm), BF),
            jax.ShapeDtypeStruct((B, FOX_HEADS, 1, S), F32),
            jax.ShapeDtypeStruct((B, nt * FOX_HEADS, 1, LANES), F32),
            jax.ShapeDtypeStruct((B, nt, 8, LANES), F32),
            jax.ShapeDtypeStruct((B, nt, FOX_HEADS, LANES), F32),
        ],
        scratch_shapes=[pltpu.VMEM((1, LANES), F32)],
        compiler_params=pltpu.CompilerParams(
            dimension_semantics=("arbitrary", "arbitrary"), vmem_limit_bytes=VMEM_LIMIT),
        name="shared_kv",
    )(x1, wf3, bias3, wk, wqT, wvT, hsum)


def _attn_kernel(n1_ref, cnt_ref, fx_ref, ul_ref, qT_ref, cumT_ref, rb_ref, ref_ref, k_ref, dk_ref, vT_ref,
                 y_ref, qa_ref, o_ref, m_ref, s_ref, pa_ref, pb_ref, pc_ref):
    b = pl.program_id(0)
    i = pl.program_id(1)
    step = b * pl.num_programs(1) + i
    n1 = n1_ref[step]
    fixed_ok = fx_ref[step]
    tq = qa_ref.shape[2]
    tk = k_ref.shape[3]

    @pl.when(i == 0)
    def _():
        rowi = lax.broadcasted_iota(jnp.int32, (LANES, tq), 0)
        zeros = jnp.zeros((FOX_DH, tq), BF)
        for h in range(FOX_HEADS):
            lo = (h % 2) * FOX_DH
            qa_ref[h, FOX_DH - lo:2 * FOX_DH - lo, :] = zeros
            piece_rows = (rowi == h) | (rowi == FOX_HEADS + h) | (rowi == 2 * FOX_HEADS + h)
            qa_ref[h, LANES:, :] = jnp.where(piece_rows, 1.0, 0.0).astype(BF)

    for h in range(FOX_HEADS):
        lo = (h % 2) * FOX_DH
        qa_ref[h, lo:lo + FOX_DH, :] = qT_ref[0, h]
    m_ref[...] = jnp.full(m_ref.shape, NEG, F32)
    o_ref[...] = jnp.zeros_like(o_ref)

    def logits(h, j, m_prev, slot, masked):
        pair_idx = lax.shift_right_logical(jnp.int32(h), 1)
        lhs = jnp.concatenate([k_ref[0, pair_idx, j], dk_ref[0, j]], axis=1)
        s = _mm(lhs, qa_ref[h])
        if masked:
            r = lax.broadcasted_iota(jnp.int32, (tk, tq), 0)
            c = lax.broadcasted_iota(jnp.int32, (tk, tq), 1)
            s = jnp.where(r <= c, s, NEG)
        a = cumT_ref[0, h] - jnp.tile(rb_ref[0, j * FOX_HEADS + h], (1, tq // LANES))
        m_cur = jnp.maximum(m_prev, jnp.max(s, axis=0, keepdims=True) + a)
        s_ref[slot] = s
        return m_cur, m_cur - a, jnp.exp2(m_prev - m_cur)

    def probs(slot, shift):
        (pa_ref, pb_ref)[slot][...] = jnp.exp2(s_ref[slot] - shift).astype(BF)

    def values(h, j, slot, al):
        o_ref[h] = al * o_ref[h] + _mm(vT_ref[0, h, j], (pa_ref, pb_ref)[slot][...])

    def sweep(unit, n, masked):
        (h0, j0), (h1, j1), (h2, j2), (h3, j3) = unit(0), unit(1), unit(2), unit(3)
        m, sh_e, a_e = logits(h0, j0, m_ref[h0], 0, masked)
        m_ref[h0] = m
        m, sh_o, a_o = logits(h1, j1, m_ref[h1], 1, masked)
        m_ref[h1] = m
        probs(0, sh_e)
        a_c = a_e
        m, sh_e, a_e = logits(h2, j2, m_ref[h2], 0, masked)
        m_ref[h2] = m
        probs(1, sh_o)
        values(h0, j0, 0, a_c)
        a_c = a_o
        m, sh_o, a_o = logits(h3, j3, m_ref[h3], 1, masked)
        m_ref[h3] = m

        def pair(g, carry):
            sh_e, a_e, sh_o, a_o, a_c = carry
            t = 2 * g
            (hp, jp), (he, je) = unit(t - 1), unit(t)
            (he2, je2), (ho2, jo2) = unit(t + 2), unit(t + 3)
            mp_e, mp_o = m_ref[he2], m_ref[ho2]
            probs(0, sh_e)
            m_e2, sh_e2, a_e2 = logits(he2, je2, mp_e, 0, masked)
            values(hp, jp, 1, a_c)
            probs(1, sh_o)
            mp_o = jnp.where(ho2 == he2, m_e2, mp_o)
            m_o2, sh_o2, a_o2 = logits(ho2, jo2, mp_o, 1, masked)
            values(he, je, 0, a_e)
            m_ref[he2] = m_e2
            m_ref[ho2] = m_o2
            return sh_e2, a_e2, sh_o2, a_o2, a_o

        sh_e, a_e, sh_o, a_o, a_c = lax.fori_loop(1, n // 2 - 1, pair, (sh_e, a_e, sh_o, a_o, a_c))
        (hp, jp), (he, je), (ho, jo) = unit(n - 3), unit(n - 2), unit(n - 1)
        probs(0, sh_e)
        values(hp, jp, 1, a_c)
        probs(1, sh_o)
        values(he, je, 0, a_e)
        values(ho, jo, 1, a_o)

    half = tk // 2

    def fixed_probs(t, unit, cnt, masked, p_buf):
        h, j = unit(t)
        pair_idx = lax.shift_right_logical(jnp.int32(h), 1)
        base = rb_ref[0, j * FOX_HEADS + h] + ref_ref[0, h]
        shift = jnp.tile(base, (1, tq // LANES)) - cumT_ref[0, h]
        if cnt is not None:
            shift = shift + jnp.where(t >= cnt, -NEG, 0.0)
        if not masked:
            lhs = jnp.concatenate([k_ref[0, pair_idx, j], dk_ref[0, j]], axis=1)
            p_buf[...] = jnp.exp2(_mm(lhs, qa_ref[h]) - shift).astype(BF)
            return
        top = jnp.concatenate([k_ref[0, pair_idx, j, 0:half, :], dk_ref[0, j, 0:half, :]], axis=1)
        bot = jnp.concatenate([k_ref[0, pair_idx, j, half:, :], dk_ref[0, j, half:, :]], axis=1)
        s_top = _mm(top, qa_ref[h])
        s_bot = _mm(bot, qa_ref[h, :, half:])
        keep_top = (lax.broadcasted_iota(jnp.int32, (half, tq), 0)
                    <= lax.broadcasted_iota(jnp.int32, (half, tq), 1))
        keep_bot = (lax.broadcasted_iota(jnp.int32, (half, tq - half), 0)
                    <= lax.broadcasted_iota(jnp.int32, (half, tq - half), 1))
        shift_bot = jnp.tile(base, (1, (tq - half) // LANES)) - cumT_ref[0, h, :, half:]
        p_buf[0:half, :] = jnp.exp2(jnp.where(keep_top, s_top, NEG) - shift).astype(BF)
        p_buf[half:, half:] = jnp.exp2(jnp.where(keep_bot, s_bot, NEG) - shift_bot).astype(BF)

    def fixed_values(t, unit, p_buf, masked):
        h, j = unit(t)
        if not masked:
            o_ref[h] += _mm(vT_ref[0, h, j], p_buf[...])
            return
        o_ref[h] += _mm(vT_ref[0, h, j, :, 0:half], p_buf[0:half, :])
        o_ref[h, :, half:] += _mm(vT_ref[0, h, j, :, half:], p_buf[half:, half:])

    def fixed_sweep(unit, n, cnt, masked):
        bufs = (pa_ref, pb_ref, pc_ref)
        fixed_probs(0, unit, cnt, masked, bufs[0])
        fixed_probs(1, unit, cnt, masked, bufs[1])

        def tick(t, k, last_ab):
            fixed_values(t, unit, bufs[k], masked)
            if last_ab is None or t + 2 <= last_ab:
                fixed_probs(t + 2, unit, cnt, masked, bufs[(k + 2) % 3])

        def body(g, carry):
            for k in range(BODY_TICKS):
                tick(BODY_TICKS * g + k, k % 3, None)
            return carry

        n_loop = (n - 2) // BODY_TICKS
        lax.fori_loop(0, n_loop, body, 0)
        if isinstance(n, int):
            for t in range(BODY_TICKS * n_loop, n):
                tick(t, t % 3, n - 1)
        else:
            done = BODY_TICKS * n_loop

            def short_body(g, carry):
                for k in range(3):
                    tick(done + k, k, None)
                return carry

            lax.fori_loop(0, (n - 2 - done) // 3, short_body, 0)
            fixed_values(n - 2, unit, bufs[0], masked)
            fixed_values(n - 1, unit, bufs[1], masked)

    def listed_unit(t):
        code = ul_ref[0, 0, t]
        return code & (FOX_HEADS - 1), lax.shift_right_logical(code, 4)

    diag_unit = lambda t: (t, i)

    @pl.when(jnp.logical_and(fixed_ok == 1, n1 > 0))
    def _():
        cnt = cnt_ref[step]
        fixed_sweep(listed_unit, cnt + (2 - cnt) % 3, cnt, False)

    @pl.when(fixed_ok == 1)
    def _():
        fixed_sweep(diag_unit, FOX_HEADS, None, True)

    @pl.when(jnp.logical_and(fixed_ok == 0, n1 > 0))
    def _():
        sweep(listed_unit, n1, False)

    @pl.when(fixed_ok == 0)
    def _():
        sweep(diag_unit, FOX_HEADS, True)

    for h in range(FOX_HEADS):
        y_ref[0, h] = (o_ref[h, 0:FOX_DH, :] * (1.0 / o_ref[h, FOX_DH:FOX_DH + 1, :])).astype(BF)


def _attn_schedule(kst, qst):
    B, nt = kst.shape[0], kst.shape[1]
    cmax = kst[:, :, 0, :FOX_HEADS]
    cmin = kst[:, :, 1, :FOX_HEADS]
    kn = jnp.sqrt(kst[:, :, 2, :FOX_HEADS]) * NORM_SLACK
    qn = jnp.sqrt(qst[:, :, :, 0]) * NORM_SLACK
    upper = qn[:, :, None] * kn[:, None] + cmax[:, :, None] - cmin[:, None]
    lower = -qn * kn
    need = upper > lower[:, :, None] - UNDERFLOW_MARGIN
    ti = jnp.arange(nt)
    valid = (ti[None, :] < ti[:, None])[None, :, :, None]
    rank = jnp.where(valid & need, 0, jnp.where(valid, 1, 2)).reshape(B, nt, nt * FOX_HEADS)
    n_codes = nt * FOX_HEADS
    units = jnp.sort(rank * n_codes + jnp.arange(n_codes, dtype=rank.dtype), axis=-1) % n_codes
    units = units.astype(jnp.int32)
    cnt = jnp.sum(rank == 0, axis=-1).astype(jnp.int32)
    n1 = jnp.where(cnt > 0, jnp.maximum(cnt + (cnt & 1), 4), 0)

    top = qn * lax.cummax(kn, axis=1)
    ref = 0.5 * (top + lower) - REF_HEADROOM
    fixed_ok = jnp.all(top - lower <= REF_SPREAD_LIMIT, axis=-1).astype(jnp.int32)
    ref = jnp.broadcast_to(ref.reshape(B * nt, FOX_HEADS, 1, 1), (B * nt, FOX_HEADS, 1, LANES))
    return (n1.reshape(-1), cnt.reshape(-1), fixed_ok.reshape(-1),
            units.reshape(B * nt, 1, nt * FOX_HEADS), ref)


def _attention(qT, cumT, rb, kst, qst, k, dk, vT, tq):
    B, H, dh, S = qT.shape
    nt = S // tq
    n1, cnt, fixed_ok, units, ref = _attn_schedule(kst, qst)
    resident = dict(pipeline_mode=pl.Buffered(1))
    grid_spec = pltpu.PrefetchScalarGridSpec(
        num_scalar_prefetch=3,
        grid=(B, nt),
        in_specs=[
            pl.BlockSpec((1, 1, nt * H), lambda b, i, *_: (b * nt + i, 0, 0), memory_space=pltpu.SMEM),
            pl.BlockSpec((1, H, dh, tq), lambda b, i, *_: (b, 0, 0, i)),
            pl.BlockSpec((1, H, 1, tq), lambda b, i, *_: (b, 0, 0, i)),
            pl.BlockSpec((1, nt * H, 1, LANES), lambda b, i, *_: (b, 0, 0, 0)),
            pl.BlockSpec((1, H, 1, LANES), lambda b, i, *_: (b * nt + i, 0, 0, 0)),
            pl.BlockSpec((1, HEAD_PAIRS, nt, tq, LANES), lambda b, i, *_: (b, 0, 0, 0, 0), **resident),
            pl.BlockSpec((1, nt, tq, LANES), lambda b, i, *_: (b, 0, 0, 0), **resident),
            pl.BlockSpec((1, H, nt, VROWS, tq), lambda b, i, *_: (b, 0, 0, 0, 0), **resident),
        ],
        out_specs=pl.BlockSpec((1, H, dh, tq), lambda b, i, *_: (b, 0, 0, i)),
        scratch_shapes=[
            pltpu.VMEM((H, QA_ROWS, tq), BF),
            pltpu.VMEM((H, VROWS, tq), F32),
            pltpu.VMEM((H, 1, tq), F32),
            pltpu.VMEM((2, tq, tq), F32),
            pltpu.VMEM((tq, tq), BF),
            pltpu.VMEM((tq, tq), BF),
            pltpu.VMEM((tq, tq), BF),
        ],
    )
    return pl.pallas_call(
        _attn_kernel,
        grid_spec=grid_spec,
        out_shape=jax.ShapeDtypeStruct((B, H, dh, S), BF),
        compiler_params=pltpu.CompilerParams(
            dimension_semantics=("arbitrary", "arbitrary"), vmem_limit_bytes=VMEM_LIMIT),
        name="fox_attn",
    )(n1, cnt, fixed_ok, units, qT, cumT, rb, ref, k, dk, vT)


def _layer1_kernel(x_ref, yT_ref, w_inT_ref, mk_ref, mvT_ref, w_outT_ref, lng_ref, lnb_ref,
                   o_ref, yg_ref):
    x = x_ref[0]
    xb = x.astype(BF)
    qmT = _nt(w_inT_ref[0, D_MAIN:D_MIX, :], xb)
    gT = _nt(w_inT_ref[0, D_MIX:, :], xb)
    _finish_layer(x, yT_ref[0].astype(F32), qmT, gT, mk_ref, mvT_ref, w_outT_ref, lng_ref, lnb_ref,
                  o_ref, yg_ref)


def _layer1(x1, yT, w_inT, mk, mvT, w_outT, lng, lnb, tm):
    B, S, _ = x1.shape
    return pl.pallas_call(
        _layer1_kernel,
        grid=(B, S // tm),
        in_specs=[
            pl.BlockSpec((1, tm, D_MODEL), lambda b, t: (b, t, 0)),
            pl.BlockSpec((1, D_MAIN, tm), lambda b, t: (b, 0, t)),
            _layer_spec(w_inT, 1),
            pl.BlockSpec((1, 1, MEM_HEADS, N_MEM, MEM_DH), lambda b, t: (1, b, 0, 0, 0)),
            pl.BlockSpec((1, 1, MEM_HEADS, MEM_DH, N_MEM), lambda b, t: (1, b, 0, 0, 0)),
            _layer_spec(w_outT, 1),
            _layer_spec(lng, 1),
            _layer_spec(lnb, 1),
        ],
        out_specs=pl.BlockSpec((1, tm, D_MODEL), lambda b, t: (b, t, 0)),
        out_shape=jax.ShapeDtypeStruct((B, S, D_MODEL), F32),
        scratch_shapes=[pltpu.VMEM((D_MIX, tm), BF)],
        compiler_params=pltpu.CompilerParams(
            dimension_semantics=("arbitrary", "arbitrary"), vmem_limit_bytes=VMEM_LIMIT),
        name="layer1",
    )(x1, yT, w_inT, mk, mvT, w_outT, lng, lnb)


def kernel(x, mem, w_in, w_mem_kv, w_out, ln_g, ln_b, pool_w, pool_scale, w_kv_shared, b_forget):
    B, S, _ = x.shape
    tm = min(ATTN_TILE, S)
    assert S % tm == 0 and tm % (2 * LANES) == 0, (S, tm)

    w_inT = jnp.swapaxes(w_in, 1, 2).astype(BF)
    w_outT = jnp.swapaxes(w_out, 1, 2).astype(BF)
    pool_wT = jnp.swapaxes(pool_w[0], 1, 2).astype(BF)
    ps = pool_scale[0].reshape(D_MAIN, 1)
    lng = ln_g.reshape(DEPTH, 1, D_MODEL)
    lnb = ln_b.reshape(DEPTH, 1, D_MODEL)
    wk = w_kv_shared[:, :D_MAIN].astype(BF)
    wvT = w_kv_shared[:, D_MAIN:2 * D_MAIN].T.astype(BF)
    wf = w_kv_shared[:, 2 * D_MAIN:]
    wf3 = jnp.concatenate([wf, wf, wf, jnp.zeros((D_MODEL, LANES - 3 * FOX_HEADS), F32)], axis=1).astype(BF)
    bias3 = jnp.concatenate([b_forget, b_forget, b_forget,
                             jnp.zeros((LANES - 3 * FOX_HEADS,), F32)]).reshape(1, LANES)

    mk, mvT = _mem_kv(mem, w_mem_kv.astype(BF))
    tl = 2 * tm if S % (2 * tm) == 0 else tm
    x1 = _layer0(x, w_inT, pool_wT, ps, mk, mvT, w_outT, lng, lnb, tl)
    k, dk, qT, vT, cumT, rb, kst, qst = _shared_kv(x1, wf3, bias3, wk, w_inT, wvT, tm)
    yT = _attention(qT, cumT, rb, kst, qst, k, dk, vT, tm)
    return _layer1(x1, yT.reshape(B, D_MAIN, S), w_inT, mk, mvT, w_outT, lng, lnb, tl)
```

```python
import numpy as np
import jax
import jax.numpy as jnp
from jax import lax
from jax.experimental import pallas as pl
from jax.experimental.pallas import tpu as pltpu

D_MODEL = 1024
N_MEM = 256
D_MAIN = 1024
POOL_WINDOWS = (2, 4, 8, 16)
POOL_GROUP = 256
FOX_HEADS = 16
FOX_DH = 64
MEM_HEADS = 4
MEM_DH = 128
D_MEM = MEM_HEADS * MEM_DH
D_MIX = D_MAIN + D_MEM
DEPTH = 2
ALPHA = (2 * DEPTH) ** 0.25
LN_EPS = 1e-5

LANES = 128
BF16_ROWS = 16
HEAD_PAIRS = FOX_HEADS // 2
QA_ROWS = 2 * LANES
VROWS = FOX_DH + BF16_ROWS
NEG = -1e30
LOG2E = 1.4426950408889634
UNDERFLOW_MARGIN = 152.0
NORM_SLACK = 1.02
BODY_TICKS = 6
REF_HEADROOM = 40.0
REF_SPREAD_LIMIT = 100.0
ATTN_TILE = 512
VMEM_LIMIT = 56 * 1024 * 1024

BF = jnp.bfloat16
F32 = jnp.float32


def _mm(a, b):
    return jnp.dot(a, b, preferred_element_type=F32)


def _nt(a, b):
    return lax.dot_general(a, b, (((1,), (1,)), ((), ())), preferred_element_type=F32)


def _split3(v):
    hi = v.astype(BF)
    r1 = v - hi.astype(F32)
    mid = r1.astype(BF)
    lo = (r1 - mid.astype(F32)).astype(BF)
    return hi, mid, lo


def _silu(g):
    hg = 0.5 * g
    return hg + hg * jnp.tanh(hg)


def _const_spec(shape):
    nd = len(shape)
    return pl.BlockSpec(shape, lambda *_: (0,) * nd)


def _layer_spec(arr, layer):
    nd = arr.ndim
    return pl.BlockSpec((1,) + arr.shape[1:], lambda *_: (layer,) + (0,) * (nd - 1))


def _mem_kv_kernel(mem_ref, w_ref, mk_ref, mvT_ref):
    memb = mem_ref[0].astype(BF)
    mkv = _mm(memb, w_ref[0])
    for h in range(MEM_HEADS):
        mk_ref[0, 0, h] = mkv[:, h * MEM_DH:(h + 1) * MEM_DH].astype(BF)
        mv = mkv[:, D_MEM + h * MEM_DH:D_MEM + (h + 1) * MEM_DH]
        mvT_ref[0, 0, h] = mv.T.astype(BF)


def _mem_kv(mem, w_mem_kv_bf):
    B = mem.shape[0]
    L = w_mem_kv_bf.shape[0]
    return pl.pallas_call(
        _mem_kv_kernel,
        grid=(L, B),
        in_specs=[
            pl.BlockSpec((1, N_MEM, D_MODEL), lambda l, b: (b, 0, 0)),
            pl.BlockSpec((1, D_MODEL, 2 * D_MEM), lambda l, b: (l, 0, 0)),
        ],
        out_specs=[
            pl.BlockSpec((1, 1, MEM_HEADS, N_MEM, MEM_DH), lambda l, b: (l, b, 0, 0, 0)),
            pl.BlockSpec((1, 1, MEM_HEADS, MEM_DH, N_MEM), lambda l, b: (l, b, 0, 0, 0)),
        ],
        out_shape=[
            jax.ShapeDtypeStruct((L, B, MEM_HEADS, N_MEM, MEM_DH), BF),
            jax.ShapeDtypeStruct((L, B, MEM_HEADS, MEM_DH, N_MEM), BF),
        ],
        compiler_params=pltpu.CompilerParams(
            dimension_semantics=("arbitrary", "arbitrary"), vmem_limit_bytes=VMEM_LIMIT),
        name="mem_kv",
    )(mem, w_mem_kv_bf)


def _finish_layer(x, mainT, qmT, gT, mk_ref, mvT_ref, w_outT_ref, lng_ref, lnb_ref, o_ref, yg_ref):
    yg_ref[0:D_MAIN, :] = (mainT * _silu(gT[0:D_MAIN])).astype(BF)
    for h in range(MEM_HEADS):
        rows = slice(h * MEM_DH, (h + 1) * MEM_DH)
        qh = qmT[rows].astype(BF)
        lg = _mm(mk_ref[0, 0, h], qh) * (MEM_DH ** -0.5 * LOG2E)
        mx = jnp.max(lg, axis=0, keepdims=True)
        e = jnp.exp2(lg - mx)
        den = jnp.sum(e, axis=0, keepdims=True)
        ym = _mm(mvT_ref[0, 0, h], e.astype(BF)) * (1.0 / den)
        gm = gT[D_MAIN + h * MEM_DH:D_MAIN + (h + 1) * MEM_DH]
        yg_ref[D_MAIN + h * MEM_DH:D_MAIN + (h + 1) * MEM_DH, :] = (ym * _silu(gm)).astype(BF)
    outT = _mm(w_outT_ref[0], yg_ref[...])
    z = ALPHA * x + outT.T
    mu = jnp.mean(z, axis=-1, keepdims=True)
    zc = z - mu
    var = jnp.mean(zc * zc, axis=-1, keepdims=True)
    o_ref[0] = zc * lax.rsqrt(var + LN_EPS) * lng_ref[0] + lnb_ref[0]


def _layer0_kernel(x_ref, w_inT_ref, band_ref, pool_wT_ref, ps_ref, mk_ref, mvT_ref, w_outT_ref,
                   lng_ref, lnb_ref, o_ref, tail_ref, main_ref, yg_ref):
    t = pl.program_id(1)
    x = x_ref[0]
    tm = x.shape[0]
    xb = x.astype(BF)
    uT = _nt(w_inT_ref[0, 0:D_MAIN, :], xb)
    qmT = _nt(w_inT_ref[0, D_MAIN:D_MIX, :], xb)
    gT = _nt(w_inT_ref[0, D_MIX:, :], xb)

    @pl.when(t == 0)
    def _():
        tail_ref[...] = jnp.zeros_like(tail_ref)

    tpos = t * tm + lax.broadcasted_iota(jnp.int32, (1, tm), 1)
    ub = uT.astype(BF)
    for g, w in enumerate(POOL_WINDOWS):
        rows = slice(g * POOL_GROUP, (g + 1) * POOL_GROUP)
        ext = jnp.concatenate([tail_ref[rows, :], ub[rows]], axis=1)
        wsum = jnp.concatenate(
            [_mm(ext[:, n * LANES:(n + 2) * LANES], band_ref[g]) for n in range(tm // LANES)], axis=1)
        inv_cnt = 1.0 / jnp.minimum(tpos + 1, w).astype(F32)
        pm = wsum * inv_cnt - uT[rows]
        main_ref[rows, :] = _mm(pool_wT_ref[g], pm.astype(BF)) * ps_ref[rows, :]
    tail_ref[...] = ub[:, tm - LANES:]

    _finish_layer(x, main_ref[...], qmT, gT, mk_ref, mvT_ref, w_outT_ref, lng_ref, lnb_ref,
                  o_ref, yg_ref)


def _pool_bands():
    s = np.arange(2 * LANES)[:, None]
    c = np.arange(LANES)[None, :] + LANES
    return np.stack([((s > c - w) & (s <= c)).astype(np.float32) for w in POOL_WINDOWS])


def _layer0(x, w_inT, pool_wT, ps, mk, mvT, w_outT, lng, lnb, tm):
    B, S, _ = x.shape
    band = jnp.asarray(_pool_bands(), BF)
    return pl.pallas_call(
        _layer0_kernel,
        grid=(B, S // tm),
        in_specs=[
            pl.BlockSpec((1, tm, D_MODEL), lambda b, t: (b, t, 0)),
            _layer_spec(w_inT, 0),
            _const_spec(band.shape),
            _const_spec(pool_wT.shape),
            _const_spec(ps.shape),
            pl.BlockSpec((1, 1, MEM_HEADS, N_MEM, MEM_DH), lambda b, t: (0, b, 0, 0, 0)),
            pl.BlockSpec((1, 1, MEM_HEADS, MEM_DH, N_MEM), lambda b, t: (0, b, 0, 0, 0)),
            _layer_spec(w_outT, 0),
            _layer_spec(lng, 0),
            _layer_spec(lnb, 0),
        ],
        out_specs=pl.BlockSpec((1, tm, D_MODEL), lambda b, t: (b, t, 0)),
        out_shape=jax.ShapeDtypeStruct((B, S, D_MODEL), F32),
        scratch_shapes=[
            pltpu.VMEM((D_MAIN, LANES), BF),
            pltpu.VMEM((D_MAIN, tm), F32),
            pltpu.VMEM((D_MIX, tm), BF),
        ],
        compiler_params=pltpu.CompilerParams(
            dimension_semantics=("arbitrary", "arbitrary"), vmem_limit_bytes=VMEM_LIMIT),
        name="layer0",
    )(x, w_inT, band, pool_wT, ps, mk, mvT, w_outT, lng, lnb)


def _shared_kv_kernel(x_ref, wf3_ref, bias3_ref, wk_ref, wqT_ref, wvT_ref, hsum_ref,
                      k_ref, dk_ref, qT_ref, vT_ref, cumT_ref, rb_ref, kst_ref, qst_ref, carry_ref):
    t = pl.program_id(1)

    @pl.when(t == 0)
    def _():
        carry_ref[...] = jnp.zeros_like(carry_ref)

    tm = dk_ref.shape[2]
    for sub in range(dk_ref.shape[1]):
        _shared_kv_tile(sub, slice(sub * tm, (sub + 1) * tm), x_ref, wf3_ref, bias3_ref, wk_ref, wqT_ref,
                        wvT_ref, hsum_ref, k_ref, dk_ref, qT_ref, vT_ref, cumT_ref, rb_ref, kst_ref,
                        qst_ref, carry_ref)


def _shared_kv_tile(sub, cols, x_ref, wf3_ref, bias3_ref, wk_ref, wqT_ref, wvT_ref, hsum_ref,
                    k_ref, dk_ref, qT_ref, vT_ref, cumT_ref, rb_ref, kst_ref, qst_ref, carry_ref):
    xb = x_ref[0, cols, :].astype(BF)
    tm = xb.shape[0]
    lane = lax.broadcasted_iota(jnp.int32, (tm, LANES), 1)
    f3 = _mm(xb, wf3_ref[...]) + bias3_ref[...]
    lf = jnp.where(lane < 3 * FOX_HEADS, jax.nn.log_sigmoid(f3), 0.0)

    cb = min(256, tm)
    ri = lax.broadcasted_iota(jnp.int32, (cb, cb), 0)
    ci = lax.broadcasted_iota(jnp.int32, (cb, cb), 1)
    tri = jnp.where(ci <= ri, 1.0, 0.0).astype(BF)
    off = jnp.zeros((1, LANES), F32)
    blocks = []
    for i in range(tm // cb):
        hi, mid, lo = _split3(lf[i * cb:(i + 1) * cb])
        c = _mm(tri, hi) + _mm(tri, mid) + _mm(tri, lo) + off
        off = c[cb - 1:cb, :]
        blocks.append(c)
    c_loc = jnp.concatenate(blocks, axis=0) * LOG2E
    cum = c_loc + carry_ref[...]
    carry_ref[...] = cum[tm - 1:tm, :]

    hi, mid, lo = (v.astype(F32) for v in _split3(c_loc[0:1, :] - c_loc))
    dk_ref[0, sub] = jnp.where(lane < FOX_HEADS, hi,
                               jnp.where(lane < 2 * FOX_HEADS, mid,
                                         jnp.where(lane < 3 * FOX_HEADS, lo, 0.0))).astype(BF)
    kb = _mm(xb, wk_ref[...]).astype(BF)
    for g in range(HEAD_PAIRS):
        k_ref[0, g, sub] = kb[:, g * LANES:(g + 1) * LANES]

    qT = (_nt(wqT_ref[0], xb) * (FOX_DH ** -0.5 * LOG2E)).astype(BF)
    vT = _nt(wvT_ref[...], xb).astype(BF)

    kf32 = kb.astype(F32)
    kn2 = jnp.max(_mm((kf32 * kf32).astype(BF), hsum_ref[...]), axis=0, keepdims=True)
    cmax = jnp.max(cum, axis=0, keepdims=True)
    cmin = jnp.min(cum, axis=0, keepdims=True)
    srow = lax.broadcasted_iota(jnp.int32, (8, LANES), 0)
    kst_ref[0, sub] = jnp.where(srow == 0, cmax, jnp.where(srow == 1, cmin, jnp.where(srow == 2, kn2, 0.0)))
    qf32 = qT.astype(F32)
    qn2 = jnp.sum((qf32 * qf32).reshape(FOX_HEADS, FOX_DH, tm), axis=1)
    qst_ref[0, sub] = jnp.broadcast_to(jnp.max(qn2, axis=1, keepdims=True), (FOX_HEADS, LANES))
    ones_rows = jnp.where(lax.broadcasted_iota(jnp.int32, (BF16_ROWS, tm), 0) == 0, 1.0, 0.0).astype(BF)
    for h in range(FOX_HEADS):
        qT_ref[0, h, :, cols] = qT[h * FOX_DH:(h + 1) * FOX_DH]
        vT_ref[0, h, sub, 0:FOX_DH, :] = vT[h * FOX_DH:(h + 1) * FOX_DH]
        vT_ref[0, h, sub, FOX_DH:, :] = ones_rows

    cT = cum.T
    for h in range(FOX_HEADS):
        cumT_ref[0, h, :, cols] = cT[h:h + 1]
        rb_ref[0, sub * FOX_HEADS + h] = jnp.broadcast_to(cT[h:h + 1, 0:1], (1, LANES))


def _head_sum_matrix():
    g = np.zeros((D_MAIN, LANES), np.float32)
    for h in range(FOX_HEADS):
        g[h * FOX_DH:(h + 1) * FOX_DH, h] = 1.0
    return g


def _shared_kv(x1, wf3, bias3, wk, wqT, wvT, tm):
    B, S, _ = x1.shape
    nt = S // tm
    ns = max(d for d in (4, 2, 1) if nt % d == 0)
    hsum = jnp.asarray(_head_sum_matrix(), BF)
    return pl.pallas_call(
        _shared_kv_kernel,
        grid=(B, nt // ns),
        in_specs=[
            pl.BlockSpec((1, ns * tm, D_MODEL), lambda b, t: (b, t, 0)),
            _const_spec(wf3.shape),
            _const_spec(bias3.shape),
            _const_spec(wk.shape),
            pl.BlockSpec((1, D_MAIN, D_MODEL), lambda b, t: (1, 0, 0)),
            _const_spec(wvT.shape),
            _const_spec(hsum.shape),
        ],
        out_specs=[
            pl.BlockSpec((1, HEAD_PAIRS, ns, tm, LANES), lambda b, t: (b, 0, t, 0, 0)),
            pl.BlockSpec((1, ns, tm, LANES), lambda b, t: (b, t, 0, 0)),
            pl.BlockSpec((1, FOX_HEADS, FOX_DH, ns * tm), lambda b, t: (b, 0, 0, t)),
            pl.BlockSpec((1, FOX_HEADS, ns, VROWS, tm), lambda b, t: (b, 0, t, 0, 0)),
            pl.BlockSpec((1, FOX_HEADS, 1, ns * tm), lambda b, t: (b, 0, 0, t)),
            pl.BlockSpec((1, ns * FOX_HEADS, 1, LANES), lambda b, t: (b, t, 0, 0)),
            pl.BlockSpec((1, ns, 8, LANES), lambda b, t: (b, t, 0, 0)),
            pl.BlockSpec((1, ns, FOX_HEADS, LANES), lambda b, t: (b, t, 0, 0)),
        ],
        out_shape=[
            jax.ShapeDtypeStruct((B, HEAD_PAIRS, nt, tm, LANES), BF),
            jax.ShapeDtypeStruct((B, nt, tm, LANES), BF),
            jax.ShapeDtypeStruct((B, FOX_HEADS, FOX_DH, S), BF),
            jax.ShapeDtypeStruct((B, FOX_HEADS, nt, VROWS, tm), BF),
            jax.ShapeDtypeStruct((B, FOX_HEADS, 1, S), F32),
            jax.ShapeDtypeStruct((B, nt * FOX_HEADS, 1, LANES), F32),
            jax.ShapeDtypeStruct((B, nt, 8, LANES), F32),
            jax.ShapeDtypeStruct((B, nt, FOX_HEADS, LANES), F32),
        ],
        scratch_shapes=[pltpu.VMEM((1, LANES), F32)],
        compiler_params=pltpu.CompilerParams(
            dimension_semantics=("arbitrary", "arbitrary"), vmem_limit_bytes=VMEM_LIMIT),
        name="shared_kv",
    )(x1, wf3, bias3, wk, wqT, wvT, hsum)


def _attn_kernel(n1_ref, cnt_ref, fx_ref, ul_ref, qT_ref, cumT_ref, rb_ref, ref_ref, k_ref, dk_ref, vT_ref,
                 y_ref, qa_ref, o_ref, m_ref, s_ref, pa_ref, pb_ref, pc_ref):
    b = pl.program_id(0)
    i = pl.program_id(1)
    step = b * pl.num_programs(1) + i
    n1 = n1_ref[step]
    fixed_ok = fx_ref[step]
    tq = qa_ref.shape[2]
    tk = k_ref.shape[3]

    @pl.when(i == 0)
    def _():
        rowi = lax.broadcasted_iota(jnp.int32, (LANES, tq), 0)
        zeros = jnp.zeros((FOX_DH, tq), BF)
        for h in range(FOX_HEADS):
            lo = (h % 2) * FOX_DH
            qa_ref[h, FOX_DH - lo:2 * FOX_DH - lo, :] = zeros
            piece_rows = (rowi == h) | (rowi == FOX_HEADS + h) | (rowi == 2 * FOX_HEADS + h)
            qa_ref[h, LANES:, :] = jnp.where(piece_rows, 1.0, 0.0).astype(BF)

    for h in range(FOX_HEADS):
        lo = (h % 2) * FOX_DH
        qa_ref[h, lo:lo + FOX_DH, :] = qT_ref[0, h]
    m_ref[...] = jnp.full(m_ref.shape, NEG, F32)
    o_ref[...] = jnp.zeros_like(o_ref)

    def logits(h, j, m_prev, slot, masked):
        pair_idx = lax.shift_right_logical(jnp.int32(h), 1)
        lhs = jnp.concatenate([k_ref[0, pair_idx, j], dk_ref[0, j]], axis=1)
        s = _mm(lhs, qa_ref[h])
        if masked:
            r = lax.broadcasted_iota(jnp.int32, (tk, tq), 0)
            c = lax.broadcasted_iota(jnp.int32, (tk, tq), 1)
            s = jnp.where(r <= c, s, NEG)
        a = cumT_ref[0, h] - jnp.tile(rb_ref[0, j * FOX_HEADS + h], (1, tq // LANES))
        m_cur = jnp.maximum(m_prev, jnp.max(s, axis=0, keepdims=True) + a)
        s_ref[slot] = s
        return m_cur, m_cur - a, jnp.exp2(m_prev - m_cur)

    def probs(slot, shift):
        (pa_ref, pb_ref)[slot][...] = jnp.exp2(s_ref[slot] - shift).astype(BF)

    def values(h, j, slot, al):
        o_ref[h] = al * o_ref[h] + _mm(vT_ref[0, h, j], (pa_ref, pb_ref)[slot][...])

    def sweep(unit, n, masked):
        (h0, j0), (h1, j1), (h2, j2), (h3, j3) = unit(0), unit(1), unit(2), unit(3)
        m, sh_e, a_e = logits(h0, j0, m_ref[h0], 0, masked)
        m_ref[h0] = m
        m, sh_o, a_o = logits(h1, j1, m_ref[h1], 1, masked)
        m_ref[h1] = m
        probs(0, sh_e)
        a_c = a_e
        m, sh_e, a_e = logits(h2, j2, m_ref[h2], 0, masked)
        m_ref[h2] = m
        probs(1, sh_o)
        values(h0, j0, 0, a_c)
        a_c = a_o
        m, sh_o, a_o = logits(h3, j3, m_ref[h3], 1, masked)
        m_ref[h3] = m

        def pair(g, carry):
            sh_e, a_e, sh_o, a_o, a_c = carry
            t = 2 * g
            (hp, jp), (he, je) = unit(t - 1), unit(t)
            (he2, je2), (ho2, jo2) = unit(t + 2), unit(t + 3)
            mp_e, mp_o = m_ref[he2], m_ref[ho2]
            probs(0, sh_e)
            m_e2, sh_e2, a_e2 = logits(he2, je2, mp_e, 0, masked)
            values(hp, jp, 1, a_c)
            probs(1, sh_o)
            mp_o = jnp.where(ho2 == he2, m_e2, mp_o)
            m_o2, sh_o2, a_o2 = logits(ho2, jo2, mp_o, 1, masked)
            values(he, je, 0, a_e)
            m_ref[he2] = m_e2
            m_ref[ho2] = m_o2
            return sh_e2, a_e2, sh_o2, a_o2, a_o

        sh_e, a_e, sh_o, a_o, a_c = lax.fori_loop(1, n // 2 - 1, pair, (sh_e, a_e, sh_o, a_o, a_c))
        (hp, jp), (he, je), (ho, jo) = unit(n - 3), unit(n - 2), unit(n - 1)
        probs(0, sh_e)
        values(hp, jp, 1, a_c)
        probs(1, sh_o)
        values(he, je, 0, a_e)
        values(ho, jo, 1, a_o)

    half = tk // 2

    def fixed_probs(t, unit, cnt, masked, p_buf):
        h, j = unit(t)
        pair_idx = lax.shift_right_logical(jnp.int32(h), 1)
        base = rb_ref[0, j * FOX_HEADS + h] + ref_ref[0, h]
        shift = jnp.tile(base, (1, tq // LANES)) - cumT_ref[0, h]
        if cnt is not None:
            shift = shift + jnp.where(t >= cnt, -NEG, 0.0)
        if not masked:
            lhs = jnp.concatenate([k_ref[0, pair_idx, j], dk_ref[0, j]], axis=1)
            p_buf[...] = jnp.exp2(_mm(lhs, qa_ref[h]) - shift).astype(BF)
            return
        top = jnp.concatenate([k_ref[0, pair_idx, j, 0:half, :], dk_ref[0, j, 0:half, :]], axis=1)
        bot = jnp.concatenate([k_ref[0, pair_idx, j, half:, :], dk_ref[0, j, half:, :]], axis=1)
        s_top = _mm(top, qa_ref[h])
        s_bot = _mm(bot, qa_ref[h, :, half:])
        keep_top = (lax.broadcasted_iota(jnp.int32, (half, tq), 0)
                    <= lax.broadcasted_iota(jnp.int32, (half, tq), 1))
        keep_bot = (lax.broadcasted_iota(jnp.int32, (half, tq - half), 0)
                    <= lax.broadcasted_iota(jnp.int32, (half, tq - half), 1))
        shift_bot = jnp.tile(base, (1, (tq - half) // LANES)) - cumT_ref[0, h, :, half:]
        p_buf[0:half, :] = jnp.exp2(jnp.where(keep_top, s_top, NEG) - shift).astype(BF)
        p_buf[half:, half:] = jnp.exp2(jnp.where(keep_bot, s_bot, NEG) - shift_bot).astype(BF)

    def fixed_values(t, unit, p_buf, masked):
        h, j = unit(t)
        if not masked:
            o_ref[h] += _mm(vT_ref[0, h, j], p_buf[...])
            return
        o_ref[h] += _mm(vT_ref[0, h, j, :, 0:half], p_buf[0:half, :])
        o_ref[h, :, half:] += _mm(vT_ref[0, h, j, :, half:], p_buf[half:, half:])

    def fixed_sweep(unit, n, cnt, masked):
        bufs = (pa_ref, pb_ref, pc_ref)
        fixed_probs(0, unit, cnt, masked, bufs[0])
        fixed_probs(1, unit, cnt, masked, bufs[1])

        def tick(t, k, last_ab):
            fixed_values(t, unit, bufs[k], masked)
            if last_ab is None or t + 2 <= last_ab:
                fixed_probs(t + 2, unit, cnt, masked, bufs[(k + 2) % 3])

        def body(g, carry):
            for k in range(BODY_TICKS):
                tick(BODY_TICKS * g + k, k % 3, None)
            return carry

        n_loop = (n - 2) // BODY_TICKS
        lax.fori_loop(0, n_loop, body, 0)
        if isinstance(n, int):
            for t in range(BODY_TICKS * n_loop, n):
                tick(t, t % 3, n - 1)
        else:
            done = BODY_TICKS * n_loop

            def short_body(g, carry):
                for k in range(3):
                    tick(done + k, k, None)
                return carry

            lax.fori_loop(0, (n - 2 - done) // 3, short_body, 0)
            fixed_values(n - 2, unit, bufs[0], masked)
            fixed_values(n - 1, unit, bufs[1], masked)

    def listed_unit(t):
        code = ul_ref[0, 0, t]
        return code & (FOX_HEADS - 1), lax.shift_right_logical(code, 4)

    diag_unit = lambda t: (t, i)

    @pl.when(jnp.logical_and(fixed_ok == 1, n1 > 0))
    def _():
        cnt = cnt_ref[step]
        fixed_sweep(listed_unit, cnt + (2 - cnt) % 3, cnt, False)

    @pl.when(fixed_ok == 1)
    def _():
        fixed_sweep(diag_unit, FOX_HEADS, None, True)

    @pl.when(jnp.logical_and(fixed_ok == 0, n1 > 0))
    def _():
        sweep(listed_unit, n1, False)

    @pl.when(fixed_ok == 0)
    def _():
        sweep(diag_unit, FOX_HEADS, True)

    for h in range(FOX_HEADS):
        y_ref[0, h] = (o_ref[h, 0:FOX_DH, :] * (1.0 / o_ref[h, FOX_DH:FOX_DH + 1, :])).astype(BF)


def _attn_schedule(kst, qst):
    B, nt = kst.shape[0], kst.shape[1]
    cmax = kst[:, :, 0, :FOX_HEADS]
    cmin = kst[:, :, 1, :FOX_HEADS]
    kn = jnp.sqrt(kst[:, :, 2, :FOX_HEADS]) * NORM_SLACK
    qn = jnp.sqrt(qst[:, :, :, 0]) * NORM_SLACK
    upper = qn[:, :, None] * kn[:, None] + cmax[:, :, None] - cmin[:, None]
    lower = -qn * kn
    need = upper > lower[:, :, None] - UNDERFLOW_MARGIN
    ti = jnp.arange(nt)
    valid = (ti[None, :] < ti[:, None])[None, :, :, None]
    rank = jnp.where(valid & need, 0, jnp.where(valid, 1, 2)).reshape(B, nt, nt * FOX_HEADS)
    n_codes = nt * FOX_HEADS
    units = jnp.sort(rank * n_codes + jnp.arange(n_codes, dtype=rank.dtype), axis=-1) % n_codes
    units = units.astype(jnp.int32)
    cnt = jnp.sum(rank == 0, axis=-1).astype(jnp.int32)
    n1 = jnp.where(cnt > 0, jnp.maximum(cnt + (cnt & 1), 4), 0)

    top = qn * lax.cummax(kn, axis=1)
    ref = 0.5 * (top + lower) - REF_HEADROOM
    fixed_ok = jnp.all(top - lower <= REF_SPREAD_LIMIT, axis=-1).astype(jnp.int32)
    ref = jnp.broadcast_to(ref.reshape(B * nt, FOX_HEADS, 1, 1), (B * nt, FOX_HEADS, 1, LANES))
    return (n1.reshape(-1), cnt.reshape(-1), fixed_ok.reshape(-1),
            units.reshape(B * nt, 1, nt * FOX_HEADS), ref)


def _attention(qT, cumT, rb, kst, qst, k, dk, vT, tq):
    B, H, dh, S = qT.shape
    nt = S // tq
    n1, cnt, fixed_ok, units, ref = _attn_schedule(kst, qst)
    resident = dict(pipeline_mode=pl.Buffered(1))
    grid_spec = pltpu.PrefetchScalarGridSpec(
        num_scalar_prefetch=3,
        grid=(B, nt),
        in_specs=[
            pl.BlockSpec((1, 1, nt * H), lambda b, i, *_: (b * nt + i, 0, 0), memory_space=pltpu.SMEM),
            pl.BlockSpec((1, H, dh, tq), lambda b, i, *_: (b, 0, 0, i)),
            pl.BlockSpec((1, H, 1, tq), lambda b, i, *_: (b, 0, 0, i)),
            pl.BlockSpec((1, nt * H, 1, LANES), lambda b, i, *_: (b, 0, 0, 0)),
            pl.BlockSpec((1, H, 1, LANES), lambda b, i, *_: (b * nt + i, 0, 0, 0)),
            pl.BlockSpec((1, HEAD_PAIRS, nt, tq, LANES), lambda b, i, *_: (b, 0, 0, 0, 0), **resident),
            pl.BlockSpec((1, nt, tq, LANES), lambda b, i, *_: (b, 0, 0, 0), **resident),
            pl.BlockSpec((1, H, nt, VROWS, tq), lambda b, i, *_: (b, 0, 0, 0, 0), **resident),
        ],
        out_specs=pl.BlockSpec((1, H, dh, tq), lambda b, i, *_: (b, 0, 0, i)),
        scratch_shapes=[
            pltpu.VMEM((H, QA_ROWS, tq), BF),
            pltpu.VMEM((H, VROWS, tq), F32),
            pltpu.VMEM((H, 1, tq), F32),
            pltpu.VMEM((2, tq, tq), F32),
            pltpu.VMEM((tq, tq), BF),
            pltpu.VMEM((tq, tq), BF),
            pltpu.VMEM((tq, tq), BF),
        ],
    )
    return pl.pallas_call(
        _attn_kernel,
        grid_spec=grid_spec,
        out_shape=jax.ShapeDtypeStruct((B, H, dh, S), BF),
        compiler_params=pltpu.CompilerParams(
            dimension_semantics=("arbitrary", "arbitrary"), vmem_limit_bytes=VMEM_LIMIT),
        name="fox_attn",
    )(n1, cnt, fixed_ok, units, qT, cumT, rb, ref, k, dk, vT)


def _layer1_kernel(x_ref, yT_ref, w_inT_ref, mk_ref, mvT_ref, w_outT_ref, lng_ref, lnb_ref,
                   o_ref, yg_ref):
    x = x_ref[0]
    xb = x.astype(BF)
    qmT = _nt(w_inT_ref[0, D_MAIN:D_MIX, :], xb)
    gT = _nt(w_inT_ref[0, D_MIX:, :], xb)
    _finish_layer(x, yT_ref[0].astype(F32), qmT, gT, mk_ref, mvT_ref, w_outT_ref, lng_ref, lnb_ref,
                  o_ref, yg_ref)


def _layer1(x1, yT, w_inT, mk, mvT, w_outT, lng, lnb, tm):
    B, S, _ = x1.shape
    return pl.pallas_call(
        _layer1_kernel,
        grid=(B, S // tm),
        in_specs=[
            pl.BlockSpec((1, tm, D_MODEL), lambda b, t: (b, t, 0)),
            pl.BlockSpec((1, D_MAIN, tm), lambda b, t: (b, 0, t)),
            _layer_spec(w_inT, 1),
            pl.BlockSpec((1, 1, MEM_HEADS, N_MEM, MEM_DH), lambda b, t: (1, b, 0, 0, 0)),
            pl.BlockSpec((1, 1, MEM_HEADS, MEM_DH, N_MEM), lambda b, t: (1, b, 0, 0, 0)),
            _layer_spec(w_outT, 1),
            _layer_spec(lng, 1),
            _layer_spec(lnb, 1),
        ],
        out_specs=pl.BlockSpec((1, tm, D_MODEL), lambda b, t: (b, t, 0)),
        out_shape=jax.ShapeDtypeStruct((B, S, D_MODEL), F32),
        scratch_shapes=[pltpu.VMEM((D_MIX, tm), BF)],
        compiler_params=pltpu.CompilerParams(
            dimension_semantics=("arbitrary", "arbitrary"), vmem_limit_bytes=VMEM_LIMIT),
        name="layer1",
    )(x1, yT, w_inT, mk, mvT, w_outT, lng, lnb)


def kernel(x, mem, w_in, w_mem_kv, w_out, ln_g, ln_b, pool_w, pool_scale, w_kv_shared, b_forget):
    B, S, _ = x.shape
    tm = min(ATTN_TILE, S)
    assert S % tm == 0 and tm % (2 * LANES) == 0, (S, tm)

    w_inT = jnp.swapaxes(w_in, 1, 2).astype(BF)
    w_outT = jnp.swapaxes(w_out, 1, 2).astype(BF)
    pool_wT = jnp.swapaxes(pool_w[0], 1, 2).astype(BF)
    ps = pool_scale[0].reshape(D_MAIN, 1)
    lng = ln_g.reshape(DEPTH, 1, D_MODEL)
    lnb = ln_b.reshape(DEPTH, 1, D_MODEL)
    wk = w_kv_shared[:, :D_MAIN].astype(BF)
    wvT = w_kv_shared[:, D_MAIN:2 * D_MAIN].T.astype(BF)
    wf = w_kv_shared[:, 2 * D_MAIN:]
    wf3 = jnp.concatenate([wf, wf, wf, jnp.zeros((D_MODEL, LANES - 3 * FOX_HEADS), F32)], axis=1).astype(BF)
    bias3 = jnp.concatenate([b_forget, b_forget, b_forget,
                             jnp.zeros((LANES - 3 * FOX_HEADS,), F32)]).reshape(1, LANES)

    mk, mvT = _mem_kv(mem, w_mem_kv.astype(BF))
    tl = 2 * tm if S % (2 * tm) == 0 else tm
    x1 = _layer0(x, w_inT, pool_wT, ps, mk, mvT, w_outT, lng, lnb, tl)
    k, dk, qT, vT, cumT, rb, kst, qst = _shared_kv(x1, wf3, bias3, wk, w_inT, wvT, tm)
    yT = _attention(qT, cumT, rb, kst, qst, k, dk, vT, tm)
    return _layer1(x1, yT.reshape(B, D_MAIN, S), w_inT, mk, mvT, w_outT, lng, lnb, tl)
```

```python
import numpy as np
import jax
import jax.numpy as jnp
from jax import lax
from jax.experimental import pallas as pl
from jax.experimental.pallas import tpu as pltpu

D_MODEL = 1024
N_MEM = 256
D_MAIN = 1024
POOL_WINDOWS = (2, 4, 8, 16)
POOL_GROUP = 256
FOX_HEADS = 16
FOX_DH = 64
MEM_HEADS = 4
MEM_DH = 128
D_MEM = MEM_HEADS * MEM_DH
D_MIX = D_MAIN + D_MEM
DEPTH = 2
ALPHA = (2 * DEPTH) ** 0.25
LN_EPS = 1e-5

LANES = 128
BF16_ROWS = 16
HEAD_PAIRS = FOX_HEADS // 2
QA_ROWS = 2 * LANES
VROWS = FOX_DH + BF16_ROWS
NEG = -1e30
LOG2E = 1.4426950408889634
UNDERFLOW_MARGIN = 152.0
NORM_SLACK = 1.02
BODY_TICKS = 6
REF_HEADROOM = 40.0
REF_SPREAD_LIMIT = 100.0
ATTN_TILE = 512
VMEM_LIMIT = 56 * 1024 * 1024

BF = jnp.bfloat16
F32 = jnp.float32


def _mm(a, b):
    return jnp.dot(a, b, preferred_element_type=F32)


def _nt(a, b):
    return lax.dot_general(a, b, (((1,), (1,)), ((), ())), preferred_element_type=F32)


def _split3(v):
    hi = v.astype(BF)
    r1 = v - hi.astype(F32)
    mid = r1.astype(BF)
    lo = (r1 - mid.astype(F32)).astype(BF)
    return hi, mid, lo


def _silu(g):
    hg = 0.5 * g
    return hg + hg * jnp.tanh(hg)


def _const_spec(shape):
    nd = len(shape)
    return pl.BlockSpec(shape, lambda *_: (0,) * nd)


def _layer_spec(arr, layer):
    nd = arr.ndim
    return pl.BlockSpec((1,) + arr.shape[1:], lambda *_: (layer,) + (0,) * (nd - 1))


def _mem_kv_kernel(mem_ref, w_ref, mk_ref, mvT_ref):
    memb = mem_ref[0].astype(BF)
    mkv = _mm(memb, w_ref[0])
    for h in range(MEM_HEADS):
        mk_ref[0, 0, h] = mkv[:, h * MEM_DH:(h + 1) * MEM_DH].astype(BF)
        mv = mkv[:, D_MEM + h * MEM_DH:D_MEM + (h + 1) * MEM_DH]
        mvT_ref[0, 0, h] = mv.T.astype(BF)


def _mem_kv(mem, w_mem_kv_bf):
    B = mem.shape[0]
    L = w_mem_kv_bf.shape[0]
    return pl.pallas_call(
        _mem_kv_kernel,
        grid=(L, B),
        in_specs=[
            pl.BlockSpec((1, N_MEM, D_MODEL), lambda l, b: (b, 0, 0)),
            pl.BlockSpec((1, D_MODEL, 2 * D_MEM), lambda l, b: (l, 0, 0)),
        ],
        out_specs=[
            pl.BlockSpec((1, 1, MEM_HEADS, N_MEM, MEM_DH), lambda l, b: (l, b, 0, 0, 0)),
            pl.BlockSpec((1, 1, MEM_HEADS, MEM_DH, N_MEM), lambda l, b: (l, b, 0, 0, 0)),
        ],
        out_shape=[
            jax.ShapeDtypeStruct((L, B, MEM_HEADS, N_MEM, MEM_DH), BF),
            jax.ShapeDtypeStruct((L, B, MEM_HEADS, MEM_DH, N_MEM), BF),
        ],
        compiler_params=pltpu.CompilerParams(
            dimension_semantics=("arbitrary", "arbitrary"), vmem_limit_bytes=VMEM_LIMIT),
        name="mem_kv",
    )(mem, w_mem_kv_bf)


def _finish_layer(x, mainT, qmT, gT, mk_ref, mvT_ref, w_outT_ref, lng_ref, lnb_ref, o_ref, yg_ref):
    yg_ref[0:D_MAIN, :] = (mainT * _silu(gT[0:D_MAIN])).astype(BF)
    for h in range(MEM_HEADS):
        rows = slice(h * MEM_DH, (h + 1) * MEM_DH)
        qh = qmT[rows].astype(BF)
        lg = _mm(mk_ref[0, 0, h], qh) * (MEM_DH ** -0.5 * LOG2E)
        mx = jnp.max(lg, axis=0, keepdims=True)
        e = jnp.exp2(lg - mx)
        den = jnp.sum(e, axis=0, keepdims=True)
        ym = _mm(mvT_ref[0, 0, h], e.astype(BF)) * (1.0 / den)
        gm = gT[D_MAIN + h * MEM_DH:D_MAIN + (h + 1) * MEM_DH]
        yg_ref[D_MAIN + h * MEM_DH:D_MAIN + (h + 1) * MEM_DH, :] = (ym * _silu(gm)).astype(BF)
    outT = _mm(w_outT_ref[0], yg_ref[...])
    z = ALPHA * x + outT.T
    mu = jnp.mean(z, axis=-1, keepdims=True)
    zc = z - mu
    var = jnp.mean(zc * zc, axis=-1, keepdims=True)
    o_ref[0] = zc * lax.rsqrt(var + LN_EPS) * lng_ref[0] + lnb_ref[0]


def _layer0_kernel(x_ref, w_inT_ref, band_ref, pool_wT_ref, ps_ref, mk_ref, mvT_ref, w_outT_ref,
                   lng_ref, lnb_ref, o_ref, tail_ref, main_ref, yg_ref):
    t = pl.program_id(1)
    x = x_ref[0]
    tm = x.shape[0]
    xb = x.astype(BF)
    uT = _nt(w_inT_ref[0, 0:D_MAIN, :], xb)
    qmT = _nt(w_inT_ref[0, D_MAIN:D_MIX, :], xb)
    gT = _nt(w_inT_ref[0, D_MIX:, :], xb)

    @pl.when(t == 0)
    def _():
        tail_ref[...] = jnp.zeros_like(tail_ref)

    tpos = t * tm + lax.broadcasted_iota(jnp.int32, (1, tm), 1)
    ub = uT.astype(BF)
    for g, w in enumerate(POOL_WINDOWS):
        rows = slice(g * POOL_GROUP, (g + 1) * POOL_GROUP)
        ext = jnp.concatenate([tail_ref[rows, :], ub[rows]], axis=1)
        wsum = jnp.concatenate(
            [_mm(ext[:, n * LANES:(n + 2) * LANES], band_ref[g]) for n in range(tm // LANES)], axis=1)
        inv_cnt = 1.0 / jnp.minimum(tpos + 1, w).astype(F32)
        pm = wsum * inv_cnt - uT[rows]
        main_ref[rows, :] = _mm(pool_wT_ref[g], pm.astype(BF)) * ps_ref[rows, :]
    tail_ref[...] = ub[:, tm - LANES:]

    _finish_layer(x, main_ref[...], qmT, gT, mk_ref, mvT_ref, w_outT_ref, lng_ref, lnb_ref,
                  o_ref, yg_ref)


def _pool_bands():
    s = np.arange(2 * LANES)[:, None]
    c = np.arange(LANES)[None, :] + LANES
    return np.stack([((s > c - w) & (s <= c)).astype(np.float32) for w in POOL_WINDOWS])


def _layer0(x, w_inT, pool_wT, ps, mk, mvT, w_outT, lng, lnb, tm):
    B, S, _ = x.shape
    band = jnp.asarray(_pool_bands(), BF)
    return pl.pallas_call(
        _layer0_kernel,
        grid=(B, S // tm),
        in_specs=[
            pl.BlockSpec((1, tm, D_MODEL), lambda b, t: (b, t, 0)),
            _layer_spec(w_inT, 0),
            _const_spec(band.shape),
            _const_spec(pool_wT.shape),
            _const_spec(ps.shape),
            pl.BlockSpec((1, 1, MEM_HEADS, N_MEM, MEM_DH), lambda b, t: (0, b, 0, 0, 0)),
            pl.BlockSpec((1, 1, MEM_HEADS, MEM_DH, N_MEM), lambda b, t: (0, b, 0, 0, 0)),
            _layer_spec(w_outT, 0),
            _layer_spec(lng, 0),
            _layer_spec(lnb, 0),
        ],
        out_specs=pl.BlockSpec((1, tm, D_MODEL), lambda b, t: (b, t, 0)),
        out_shape=jax.ShapeDtypeStruct((B, S, D_MODEL), F32),
        scratch_shapes=[
            pltpu.VMEM((D_MAIN, LANES), BF),
            pltpu.VMEM((D_MAIN, tm), F32),
            pltpu.VMEM((D_MIX, tm), BF),
        ],
        compiler_params=pltpu.CompilerParams(
            dimension_semantics=("arbitrary", "arbitrary"), vmem_limit_bytes=VMEM_LIMIT),
        name="layer0",
    )(x, w_inT, band, pool_wT, ps, mk, mvT, w_outT, lng, lnb)


def _shared_kv_kernel(x_ref, wf3_ref, bias3_ref, wk_ref, wqT_ref, wvT_ref, hsum_ref,
                      k_ref, dk_ref, qT_ref, vT_ref, cumT_ref, rb_ref, kst_ref, qst_ref, carry_ref):
    t = pl.program_id(1)

    @pl.when(t == 0)
    def _():
        carry_ref[...] = jnp.zeros_like(carry_ref)

    tm = dk_ref.shape[2]
    for sub in range(dk_ref.shape[1]):
        _shared_kv_tile(sub, slice(sub * tm, (sub + 1) * tm), x_ref, wf3_ref, bias3_ref, wk_ref, wqT_ref,
                        wvT_ref, hsum_ref, k_ref, dk_ref, qT_ref, vT_ref, cumT_ref, rb_ref, kst_ref,
                        qst_ref, carry_ref)


def _shared_kv_tile(sub, cols, x_ref, wf3_ref, bias3_ref, wk_ref, wqT_ref, wvT_ref, hsum_ref,
                    k_ref, dk_ref, qT_ref, vT_ref, cumT_ref, rb_ref, kst_ref, qst_ref, carry_ref):
    xb = x_ref[0, cols, :].astype(BF)
    tm = xb.shape[0]
    lane = lax.broadcasted_iota(jnp.int32, (tm, LANES), 1)
    f3 = _mm(xb, wf3_ref[...]) + bias3_ref[...]
    lf = jnp.where(lane < 3 * FOX_HEADS, jax.nn.log_sigmoid(f3), 0.0)

    cb = min(256, tm)
    ri = lax.broadcasted_iota(jnp.int32, (cb, cb), 0)
    ci = lax.broadcasted_iota(jnp.int32, (cb, cb), 1)
    tri = jnp.where(ci <= ri, 1.0, 0.0).astype(BF)
    off = jnp.zeros((1, LANES), F32)
    blocks = []
    for i in range(tm // cb):
        hi, mid, lo = _split3(lf[i * cb:(i + 1) * cb])
        c = _mm(tri, hi) + _mm(tri, mid) + _mm(tri, lo) + off
        off = c[cb - 1:cb, :]
        blocks.append(c)
    c_loc = jnp.concatenate(blocks, axis=0) * LOG2E
    cum = c_loc + carry_ref[...]
    carry_ref[...] = cum[tm - 1:tm, :]

    hi, mid, lo = (v.astype(F32) for v in _split3(c_loc[0:1, :] - c_loc))
    dk_ref[0, sub] = jnp.where(lane < FOX_HEADS, hi,
                               jnp.where(lane < 2 * FOX_HEADS, mid,
                                         jnp.where(lane < 3 * FOX_HEADS, lo, 0.0))).astype(BF)
    kb = _mm(xb, wk_ref[...]).astype(BF)
    for g in range(HEAD_PAIRS):
        k_ref[0, g, sub] = kb[:, g * LANES:(g + 1) * LANES]

    qT = (_nt(wqT_ref[0], xb) * (FOX_DH ** -0.5 * LOG2E)).astype(BF)
    vT = _nt(wvT_ref[...], xb).astype(BF)

    kf32 = kb.astype(F32)
    kn2 = jnp.max(_mm((kf32 * kf32).astype(BF), hsum_ref[...]), axis=0, keepdims=True)
    cmax = jnp.max(cum, axis=0, keepdims=True)
    cmin = jnp.min(cum, axis=0, keepdims=True)
    srow = lax.broadcasted_iota(jnp.int32, (8, LANES), 0)
    kst_ref[0, sub] = jnp.where(srow == 0, cmax, jnp.where(srow == 1, cmin, jnp.where(srow == 2, kn2, 0.0)))
    qf32 = qT.astype(F32)
    qn2 = jnp.sum((qf32 * qf32).reshape(FOX_HEADS, FOX_DH, tm), axis=1)
    qst_ref[0, sub] = jnp.broadcast_to(jnp.max(qn2, axis=1, keepdims=True), (FOX_HEADS, LANES))
    ones_rows = jnp.where(lax.broadcasted_iota(jnp.int32, (BF16_ROWS, tm), 0) == 0, 1.0, 0.0).astype(BF)
    for h in range(FOX_HEADS):
        qT_ref[0, h, :, cols] = qT[h * FOX_DH:(h + 1) * FOX_DH]
        vT_ref[0, h, sub, 0:FOX_DH, :] = vT[h * FOX_DH:(h + 1) * FOX_DH]
        vT_ref[0, h, sub, FOX_DH:, :] = ones_rows

    cT = cum.T
    for h in range(FOX_HEADS):
        cumT_ref[0, h, :, cols] = cT[h:h + 1]
        rb_ref[0, sub * FOX_HEADS + h] = jnp.broadcast_to(cT[h:h + 1, 0:1], (1, LANES))


def _head_sum_matrix():
    g = np.zeros((D_MAIN, LANES), np.float32)
    for h in range(FOX_HEADS):
        g[h * FOX_DH:(h + 1) * FOX_DH, h] = 1.0
    return g


def _shared_kv(x1, wf3, bias3, wk, wqT, wvT, tm):
    B, S, _ = x1.shape
    nt = S // tm
    ns = max(d for d in (4, 2, 1) if nt % d == 0)
    hsum = jnp.asarray(_head_sum_matrix(), BF)
    return pl.pallas_call(
        _shared_kv_kernel,
        grid=(B, nt // ns),
        in_specs=[
            pl.BlockSpec((1, ns * tm, D_MODEL), lambda b, t: (b, t, 0)),
            _const_spec(wf3.shape),
            _const_spec(bias3.shape),
            _const_spec(wk.shape),
            pl.BlockSpec((1, D_MAIN, D_MODEL), lambda b, t: (1, 0, 0)),
            _const_spec(wvT.shape),
            _const_spec(hsum.shape),
        ],
        out_specs=[
            pl.BlockSpec((1, HEAD_PAIRS, ns, tm, LANES), lambda b, t: (b, 0, t, 0, 0)),
            pl.BlockSpec((1, ns, tm, LANES), lambda b, t: (b, t, 0, 0)),
            pl.BlockSpec((1, FOX_HEADS, FOX_DH, ns * tm), lambda b, t: (b, 0, 0, t)),
            pl.BlockSpec((1, FOX_HEADS, ns, VROWS, tm), lambda b, t: (b, 0, t, 0, 0)),
            pl.BlockSpec((1, FOX_HEADS, 1, ns * tm), lambda b, t: (b, 0, 0, t)),
            pl.BlockSpec((1, ns * FOX_HEADS, 1, LANES), lambda b, t: (b, t, 0, 0)),
            pl.BlockSpec((1, ns, 8, LANES), lambda b, t: (b, t, 0, 0)),
            pl.BlockSpec((1, ns, FOX_HEADS, LANES), lambda b, t: (b, t, 0, 0)),
        ],
        out_shape=[
            jax.ShapeDtypeStruct((B, HEAD_PAIRS, nt, tm, LANES), BF),
            jax.ShapeDtypeStruct((B, nt, tm, LANES), BF),
            jax.ShapeDtypeStruct((B, FOX_HEADS, FOX_DH, S), BF),
            jax.ShapeDtypeStruct((B, FOX_HEADS, nt, VROWS, tm), BF),
            jax.ShapeDtypeStruct((B, FOX_HEADS, 1, S), F32),
            jax.ShapeDtypeStruct((B, nt * FOX_HEADS, 1, LANES), F32),
            jax.ShapeDtypeStruct((B, nt, 8, LANES), F32),
            jax.ShapeDtypeStruct((B, nt, FOX_HEADS, LANES), F32),
        ],
        scratch_shapes=[pltpu.VMEM((1, LANES), F32)],
        compiler_params=pltpu.CompilerParams(
            dimension_semantics=("arbitrary", "arbitrary"), vmem_limit_bytes=VMEM_LIMIT),
        name="shared_kv",
    )(x1, wf3, bias3, wk, wqT, wvT, hsum)


def _attn_kernel(n1_ref, cnt_ref, fx_ref, ul_ref, qT_ref, cumT_ref, rb_ref, ref_ref, k_ref, dk_ref, vT_ref,
                 y_ref, qa_ref, o_ref, m_ref, s_ref, pa_ref, pb_ref, pc_ref):
    b = pl.program_id(0)
    i = pl.program_id(1)
    step = b * pl.num_programs(1) + i
    n1 = n1_ref[step]
    fixed_ok = fx_ref[step]
    tq = qa_ref.shape[2]
    tk = k_ref.shape[3]

    @pl.when(i == 0)
    def _():
        rowi = lax.broadcasted_iota(jnp.int32, (LANES, tq), 0)
        zeros = jnp.zeros((FOX_DH, tq), BF)
        for h in range(FOX_HEADS):
            lo = (h % 2) * FOX_DH
            qa_ref[h, FOX_DH - lo:2 * FOX_DH - lo, :] = zeros
            piece_rows = (rowi == h) | (rowi == FOX_HEADS + h) | (rowi == 2 * FOX_HEADS + h)
            qa_ref[h, LANES:, :] = jnp.where(piece_rows, 1.0, 0.0).astype(BF)

    for h in range(FOX_HEADS):
        lo = (h % 2) * FOX_DH
        qa_ref[h, lo:lo + FOX_DH, :] = qT_ref[0, h]
    m_ref[...] = jnp.full(m_ref.shape, NEG, F32)
    o_ref[...] = jnp.zeros_like(o_ref)

    def logits(h, j, m_prev, slot, masked):
        pair_idx = lax.shift_right_logical(jnp.int32(h), 1)
        lhs = jnp.concatenate([k_ref[0, pair_idx, j], dk_ref[0, j]], axis=1)
        s = _mm(lhs, qa_ref[h])
        if masked:
            r = lax.broadcasted_iota(jnp.int32, (tk, tq), 0)
            c = lax.broadcasted_iota(jnp.int32, (tk, tq), 1)
            s = jnp.where(r <= c, s, NEG)
        a = cumT_ref[0, h] - jnp.tile(rb_ref[0, j * FOX_HEADS + h], (1, tq // LANES))
        m_cur = jnp.maximum(m_prev, jnp.max(s, axis=0, keepdims=True) + a)
        s_ref[slot] = s
        return m_cur, m_cur - a, jnp.exp2(m_prev - m_cur)

    def probs(slot, shift):
        (pa_ref, pb_ref)[slot][...] = jnp.exp2(s_ref[slot] - shift).astype(BF)

    def values(h, j, slot, al):
        o_ref[h] = al * o_ref[h] + _mm(vT_ref[0, h, j], (pa_ref, pb_ref)[slot][...])

    def sweep(unit, n, masked):
        (h0, j0), (h1, j1), (h2, j2), (h3, j3) = unit(0), unit(1), unit(2), unit(3)
        m, sh_e, a_e = logits(h0, j0, m_ref[h0], 0, masked)
        m_ref[h0] = m
        m, sh_o, a_o = logits(h1, j1, m_ref[h1], 1, masked)
        m_ref[h1] = m
        probs(0, sh_e)
        a_c = a_e
        m, sh_e, a_e = logits(h2, j2, m_ref[h2], 0, masked)
        m_ref[h2] = m
        probs(1, sh_o)
        values(h0, j0, 0, a_c)
        a_c = a_o
        m, sh_o, a_o = logits(h3, j3, m_ref[h3], 1, masked)
        m_ref[h3] = m

        def pair(g, carry):
            sh_e, a_e, sh_o, a_o, a_c = carry
            t = 2 * g
            (hp, jp), (he, je) = unit(t - 1), unit(t)
            (he2, je2), (ho2, jo2) = unit(t + 2), unit(t + 3)
            mp_e, mp_o = m_ref[he2], m_ref[ho2]
            probs(0, sh_e)
            m_e2, sh_e2, a_e2 = logits(he2, je2, mp_e, 0, masked)
            values(hp, jp, 1, a_c)
            probs(1, sh_o)
            mp_o = jnp.where(ho2 == he2, m_e2, mp_o)
            m_o2, sh_o2, a_o2 = logits(ho2, jo2, mp_o, 1, masked)
            values(he, je, 0, a_e)
            m_ref[he2] = m_e2
            m_ref[ho2] = m_o2
            return sh_e2, a_e2, sh_o2, a_o2, a_o

        sh_e, a_e, sh_o, a_o, a_c = lax.fori_loop(1, n // 2 - 1, pair, (sh_e, a_e, sh_o, a_o, a_c))
        (hp, jp), (he, je), (ho, jo) = unit(n - 3), unit(n - 2), unit(n - 1)
        probs(0, sh_e)
        values(hp, jp, 1, a_c)
        probs(1, sh_o)
        values(he, je, 0, a_e)
        values(ho, jo, 1, a_o)

    half = tk // 2

    def fixed_probs(t, unit, cnt, masked, p_buf):
        h, j = unit(t)
        pair_idx = lax.shift_right_logical(jnp.int32(h), 1)
        base = rb_ref[0, j * FOX_HEADS + h] + ref_ref[0, h]
        shift = jnp.tile(base, (1, tq // LANES)) - cumT_ref[0, h]
        if cnt is not None:
            shift = shift + jnp.where(t >= cnt, -NEG, 0.0)
        if not masked:
            lhs = jnp.concatenate([k_ref[0, pair_idx, j], dk_ref[0, j]], axis=1)
            p_buf[...] = jnp.exp2(_mm(lhs, qa_ref[h]) - shift).astype(BF)
            return
        top = jnp.concatenate([k_ref[0, pair_idx, j, 0:half, :], dk_ref[0, j, 0:half, :]], axis=1)
        bot = jnp.concatenate([k_ref[0, pair_idx, j, half:, :], dk_ref[0, j, half:, :]], axis=1)
        s_top = _mm(top, qa_ref[h])
        s_bot = _mm(bot, qa_ref[h, :, half:])
        keep_top = (lax.broadcasted_iota(jnp.int32, (half, tq), 0)
                    <= lax.broadcasted_iota(jnp.int32, (half, tq), 1))
        keep_bot = (lax.broadcasted_iota(jnp.int32, (half, tq - half), 0)
                    <= lax.broadcasted_iota(jnp.int32, (half, tq - half), 1))
        shift_bot = jnp.tile(base, (1, (tq - half) // LANES)) - cumT_ref[0, h, :, half:]
        p_buf[0:half, :] = jnp.exp2(jnp.where(keep_top, s_top, NEG) - shift).astype(BF)
        p_buf[half:, half:] = jnp.exp2(jnp.where(keep_bot, s_bot, NEG) - shift_bot).astype(BF)

    def fixed_values(t, unit, p_buf, masked):
        h, j = unit(t)
        if not masked:
            o_ref[h] += _mm(vT_ref[0, h, j], p_buf[...])
            return
        o_ref[h] += _mm(vT_ref[0, h, j, :, 0:half], p_buf[0:half, :])
        o_ref[h, :, half:] += _mm(vT_ref[0, h, j, :, half:], p_buf[half:, half:])

    def fixed_sweep(unit, n, cnt, masked):
        bufs = (pa_ref, pb_ref, pc_ref)
        fixed_probs(0, unit, cnt, masked, bufs[0])
        fixed_probs(1, unit, cnt, masked, bufs[1])

        def tick(t, k, last_ab):
            fixed_values(t, unit, bufs[k], masked)
            if last_ab is None or t + 2 <= last_ab:
                fixed_probs(t + 2, unit, cnt, masked, bufs[(k + 2) % 3])

        def body(g, carry):
            for k in range(BODY_TICKS):
                tick(BODY_TICKS * g + k, k % 3, None)
            return carry

        if isinstance(n, int):
            for t in range(n):
                tick(t, t % 3, n - 1)
        else:
            n_loop = (n - 2) // BODY_TICKS
            lax.fori_loop(0, n_loop, body, 0)
            done = BODY_TICKS * n_loop

            def short_body(g, carry):
                for k in range(3):
                    tick(done + k, k, None)
                return carry

            lax.fori_loop(0, (n - 2 - done) // 3, short_body, 0)
            fixed_values(n - 2, unit, bufs[0], masked)
            fixed_values(n - 1, unit, bufs[1], masked)

    def listed_unit(t):
        code = ul_ref[0, 0, t]
        return code & (FOX_HEADS - 1), lax.shift_right_logical(code, 4)

    diag_unit = lambda t: (t, i)

    @pl.when(jnp.logical_and(fixed_ok == 1, n1 > 0))
    def _():
        cnt = cnt_ref[step]
        fixed_sweep(listed_unit, cnt + (2 - cnt) % 3, cnt, False)

    @pl.when(fixed_ok == 1)
    def _():
        fixed_sweep(diag_unit, FOX_HEADS, None, True)

    @pl.when(jnp.logical_and(fixed_ok == 0, n1 > 0))
    def _():
        sweep(listed_unit, n1, False)

    @pl.when(fixed_ok == 0)
    def _():
        sweep(diag_unit, FOX_HEADS, True)

    for h in range(FOX_HEADS):
        y_ref[0, h] = (o_ref[h, 0:FOX_DH, :] * (1.0 / o_ref[h, FOX_DH:FOX_DH + 1, :])).astype(BF)


def _attn_schedule(kst, qst):
    B, nt = kst.shape[0], kst.shape[1]
    cmax = kst[:, :, 0, :FOX_HEADS]
    cmin = kst[:, :, 1, :FOX_HEADS]
    kn = jnp.sqrt(kst[:, :, 2, :FOX_HEADS]) * NORM_SLACK
    qn = jnp.sqrt(qst[:, :, :, 0]) * NORM_SLACK
    upper = qn[:, :, None] * kn[:, None] + cmax[:, :, None] - cmin[:, None]
    lower = -qn * kn
    need = upper > lower[:, :, None] - UNDERFLOW_MARGIN
    ti = jnp.arange(nt)
    valid = (ti[None, :] < ti[:, None])[None, :, :, None]
    rank = jnp.where(valid & need, 0, jnp.where(valid, 1, 2)).reshape(B, nt, nt * FOX_HEADS)
    n_codes = nt * FOX_HEADS
    units = jnp.sort(rank * n_codes + jnp.arange(n_codes, dtype=rank.dtype), axis=-1) % n_codes
    units = units.astype(jnp.int32)
    cnt = jnp.sum(rank == 0, axis=-1).astype(jnp.int32)
    n1 = jnp.where(cnt > 0, jnp.maximum(cnt + (cnt & 1), 4), 0)

    top = qn * lax.cummax(kn, axis=1)
    ref = 0.5 * (top + lower) - REF_HEADROOM
    fixed_ok = jnp.all(top - lower <= REF_SPREAD_LIMIT, axis=-1).astype(jnp.int32)
    ref = jnp.broadcast_to(ref.reshape(B * nt, FOX_HEADS, 1, 1), (B * nt, FOX_HEADS, 1, LANES))
    return (n1.reshape(-1), cnt.reshape(-1), fixed_ok.reshape(-1),
            units.reshape(B * nt, 1, nt * FOX_HEADS), ref)


def _attention(qT, cumT, rb, kst, qst, k, dk, vT, tq):
    B, H, dh, S = qT.shape
    nt = S // tq
    n1, cnt, fixed_ok, units, ref = _attn_schedule(kst, qst)
    resident = dict(pipeline_mode=pl.Buffered(1))
    grid_spec = pltpu.PrefetchScalarGridSpec(
        num_scalar_prefetch=3,
        grid=(B, nt),
        in_specs=[
            pl.BlockSpec((1, 1, nt * H), lambda b, i, *_: (b * nt + i, 0, 0), memory_space=pltpu.SMEM),
            pl.BlockSpec((1, H, dh, tq), lambda b, i, *_: (b, 0, 0, i)),
            pl.BlockSpec((1, H, 1, tq), lambda b, i, *_: (b, 0, 0, i)),
            pl.BlockSpec((1, nt * H, 1, LANES), lambda b, i, *_: (b, 0, 0, 0)),
            pl.BlockSpec((1, H, 1, LANES), lambda b, i, *_: (b * nt + i, 0, 0, 0)),
            pl.BlockSpec((1, HEAD_PAIRS, nt, tq, LANES), lambda b, i, *_: (b, 0, 0, 0, 0), **resident),
            pl.BlockSpec((1, nt, tq, LANES), lambda b, i, *_: (b, 0, 0, 0), **resident),
            pl.BlockSpec((1, H, nt, VROWS, tq), lambda b, i, *_: (b, 0, 0, 0, 0), **resident),
        ],
        out_specs=pl.BlockSpec((1, H, dh, tq), lambda b, i, *_: (b, 0, 0, i)),
        scratch_shapes=[
            pltpu.VMEM((H, QA_ROWS, tq), BF),
            pltpu.VMEM((H, VROWS, tq), F32),
            pltpu.VMEM((H, 1, tq), F32),
            pltpu.VMEM((2, tq, tq), F32),
            pltpu.VMEM((tq, tq), BF),
            pltpu.VMEM((tq, tq), BF),
            pltpu.VMEM((tq, tq), BF),
        ],
    )
    return pl.pallas_call(
        _attn_kernel,
        grid_spec=grid_spec,
        out_shape=jax.ShapeDtypeStruct((B, H, dh, S), BF),
        compiler_params=pltpu.CompilerParams(
            dimension_semantics=("arbitrary", "arbitrary"), vmem_limit_bytes=VMEM_LIMIT),
        name="fox_attn",
    )(n1, cnt, fixed_ok, units, qT, cumT, rb, ref, k, dk, vT)


def _layer1_kernel(x_ref, yT_ref, w_inT_ref, mk_ref, mvT_ref, w_outT_ref, lng_ref, lnb_ref,
                   o_ref, yg_ref):
    x = x_ref[0]
    xb = x.astype(BF)
    qmT = _nt(w_inT_ref[0, D_MAIN:D_MIX, :], xb)
    gT = _nt(w_inT_ref[0, D_MIX:, :], xb)
    _finish_layer(x, yT_ref[0].astype(F32), qmT, gT, mk_ref, mvT_ref, w_outT_ref, lng_ref, lnb_ref,
                  o_ref, yg_ref)


def _layer1(x1, yT, w_inT, mk, mvT, w_outT, lng, lnb, tm):
    B, S, _ = x1.shape
    return pl.pallas_call(
        _layer1_kernel,
        grid=(B, S // tm),
        in_specs=[
            pl.BlockSpec((1, tm, D_MODEL), lambda b, t: (b, t, 0)),
            pl.BlockSpec((1, D_MAIN, tm), lambda b, t: (b, 0, t)),
            _layer_spec(w_inT, 1),
            pl.BlockSpec((1, 1, MEM_HEADS, N_MEM, MEM_DH), lambda b, t: (1, b, 0, 0, 0)),
            pl.BlockSpec((1, 1, MEM_HEADS, MEM_DH, N_MEM), lambda b, t: (1, b, 0, 0, 0)),
            _layer_spec(w_outT, 1),
            _layer_spec(lng, 1),
            _layer_spec(lnb, 1),
        ],
        out_specs=pl.BlockSpec((1, tm, D_MODEL), lambda b, t: (b, t, 0)),
        out_shape=jax.ShapeDtypeStruct((B, S, D_MODEL), F32),
        scratch_shapes=[pltpu.VMEM((D_MIX, tm), BF)],
        compiler_params=pltpu.CompilerParams(
            dimension_semantics=("arbitrary", "arbitrary"), vmem_limit_bytes=VMEM_LIMIT),
        name="layer1",
    )(x1, yT, w_inT, mk, mvT, w_outT, lng, lnb)


def kernel(x, mem, w_in, w_mem_kv, w_out, ln_g, ln_b, pool_w, pool_scale, w_kv_shared, b_forget):
    B, S, _ = x.shape
    tm = min(ATTN_TILE, S)
    assert S % tm == 0 and tm % (2 * LANES) == 0, (S, tm)

    w_inT = jnp.swapaxes(w_in, 1, 2).astype(BF)
    w_outT = jnp.swapaxes(w_out, 1, 2).astype(BF)
    pool_wT = jnp.swapaxes(pool_w[0], 1, 2).astype(BF)
    ps = pool_scale[0].reshape(D_MAIN, 1)
    lng = ln_g.reshape(DEPTH, 1, D_MODEL)
    lnb = ln_b.reshape(DEPTH, 1, D_MODEL)
    wk = w_kv_shared[:, :D_MAIN].astype(BF)
    wvT = w_kv_shared[:, D_MAIN:2 * D_MAIN].T.astype(BF)
    wf = w_kv_shared[:, 2 * D_MAIN:]
    wf3 = jnp.concatenate([wf, wf, wf, jnp.zeros((D_MODEL, LANES - 3 * FOX_HEADS), F32)], axis=1).astype(BF)
    bias3 = jnp.concatenate([b_forget, b_forget, b_forget,
                             jnp.zeros((LANES - 3 * FOX_HEADS,), F32)]).reshape(1, LANES)

    mk, mvT = _mem_kv(mem, w_mem_kv.astype(BF))
    tl = 2 * tm if S % (2 * tm) == 0 else tm
    x1 = _layer0(x, w_inT, pool_wT, ps, mk, mvT, w_outT, lng, lnb, tl)
    k, dk, qT, vT, cumT, rb, kst, qst = _shared_kv(x1, wf3, bias3, wk, w_inT, wvT, tm)
    yT = _attention(qT, cumT, rb, kst, qst, k, dk, vT, tm)
    return _layer1(x1, yT.reshape(B, D_MAIN, S), w_inT, mk, mvT, w_outT, lng, lnb, tl)
```
